```python
import math
import jax, jax.numpy as jnp
from jax import lax
import numpy as np

D_MODEL = 1024
BATCH = 4
SEQ = 4096
DEPTH = 2
DEC_BATCH = 16
DEC_SEQ = 16
PAST_LEN = 2048

CHUNK = 64
N_MIXERS = 2
N_RWKV = (DEPTH + 1) // 2
N_FOX = DEPTH // 2
HEAD_A = 64
N_HEADS_A = D_MODEL // HEAD_A
DECAY_LORA = max(32, int(round(1.8 * D_MODEL ** 0.5 / 32)) * 32)
AAA_LORA = max(32, int(round(1.8 * D_MODEL ** 0.5 / 32)) * 32)
GATE_LORA = max(32, int(round(0.6 * D_MODEL ** 0.8 / 32)) * 32)
HEAD_B = 64
N_HEADS_B = D_MODEL // HEAD_B
D_FF = 4 * D_MODEL
Q_BLOCK = 128
ALPHA = (2 * DEPTH) ** 0.25
BETA = (8 * DEPTH) ** -0.25
LN_EPS = 1e-5
GN_EPS = 64e-5

kernel_name = "rwkv7_fox_hybrid_stream_step"


def layer_norm(x, g, b):
    xf = x.astype(jnp.float32)
    mu = jnp.mean(xf, axis=-1, keepdims=True)
    var = jnp.mean(jnp.square(xf - mu), axis=-1, keepdims=True)
    return ((xf - mu) * lax.rsqrt(var + LN_EPS) * g + b).astype(x.dtype)


def sq_relu_mlp(x, w1, w2):
    return jnp.square(jax.nn.relu(x @ w1)) @ w2


def rwkv_time_mix(x, shift_prev, wkv0, mu, w0, w1, w2, a0, a1, a2, g1, g2,
                  k_k, k_a, r_k, w_r, w_k, w_v, w_o, lnx_g, lnx_b):
    B, T, D = x.shape
    x_prev = jnp.concatenate([shift_prev[:, None].astype(x.dtype), x[:, :-1]], axis=1)
    xx = x_prev - x
    xr, xw, xk, xv, xa, xg = (x + xx * mu[i] for i in range(6))
    r = xr @ w_r
    w_raw = (w0 + jnp.tanh(xw @ w1) @ w2).astype(jnp.float32)
    w_log = -jax.nn.softplus(-w_raw) - 0.5
    k = xk @ w_k
    v = xv @ w_v
    a = jax.nn.sigmoid(a0 + (xa @ a1) @ a2)
    g = jax.nn.sigmoid(xg @ g1) @ g2

    def heads(t):
        return t.reshape(B, T, N_HEADS_A, HEAD_A).astype(jnp.float32)

    r, w_log, k, v, a = heads(r), heads(w_log), heads(k), heads(v), heads(a)
    kk = k * k_k.reshape(N_HEADS_A, HEAD_A).astype(jnp.float32)
    kk = kk / jnp.maximum(jnp.sqrt(jnp.sum(jnp.square(kk), axis=-1, keepdims=True)), 1e-12)
    k = k * (1.0 + (a - 1.0) * k_a.reshape(N_HEADS_A, HEAD_A).astype(jnp.float32))
    decay = jnp.exp(-jnp.exp(w_log))

    def step(S, inp):
        r_t, d_t, k_t, v_t, av_t, bv_t = inp
        sa = jnp.einsum('bhvk,bhk->bhv', S, av_t)
        S = S * d_t[:, :, None, :] + sa[..., None] * bv_t[:, :, None, :] + v_t[..., None] * k_t[:, :, None, :]
        y_t = jnp.einsum('bhvk,bhk->bhv', S, r_t)
        return S, y_t

    def tmaj(t):
        return jnp.swapaxes(t, 0, 1)

    S_fin, y = lax.scan(step, wkv0.astype(jnp.float32),
                        (tmaj(r), tmaj(decay), tmaj(k), tmaj(v), tmaj(-kk), tmaj(kk * a)))
    y = tmaj(y)
    mu_y = jnp.mean(y, axis=-1, keepdims=True)
    var_y = jnp.mean(jnp.square(y - mu_y), axis=-1, keepdims=True)
    yn = ((y - mu_y) * lax.rsqrt(var_y + GN_EPS)).reshape(B, T, D) * lnx_g + lnx_b
    bonus = jnp.sum(r * k * r_k.astype(jnp.float32), axis=-1, keepdims=True) * v
    out = (yn + bonus.reshape(B, T, D)) * g
    return out.astype(x.dtype) @ w_o, x[:, -1], S_fin.astype(wkv0.dtype)


def fox_project(x, w_in, b_f):
    B, T, D = x.shape
    proj = x @ w_in
    q, k, v, f = jnp.split(proj, [D_MODEL, 2 * D_MODEL, 3 * D_MODEL], axis=-1)
    shp = (B, T, N_HEADS_B, HEAD_B)
    logf = jax.nn.log_sigmoid((f + b_f).astype(jnp.float32))
    return q.reshape(shp), k.reshape(shp), v.reshape(shp), logf


def fox_attend(q, k, v, c_q, c_k, q_pos, k_pos):
    s = jnp.einsum('bqhd,bkhd->bhqk', q, k, preferred_element_type=jnp.float32) * (HEAD_B ** -0.5)
    s = s + jnp.transpose(c_q, (0, 2, 1))[..., :, None] - jnp.transpose(c_k, (0, 2, 1))[..., None, :]
    s = jnp.where(k_pos[None, :] <= q_pos[:, None], s, -jnp.inf)
    p = jax.nn.softmax(s, axis=-1)
    return jnp.einsum('bhqk,bkhd->bqhd', p.astype(v.dtype), v)


def fox_prompt(x, w_in, b_f, w_o):
    B, T, D = x.shape
    q, k, v, logf = fox_project(x, w_in, b_f)
    c = jnp.cumsum(logf, axis=1)
    nb = T // Q_BLOCK
    qb = jnp.swapaxes(q.reshape(B, nb, Q_BLOCK, N_HEADS_B, HEAD_B), 0, 1)
    cb = jnp.swapaxes(c.reshape(B, nb, Q_BLOCK, N_HEADS_B), 0, 1)
    k_pos = jnp.arange(T)

    def block(args):
        q_i, c_i, i = args
        q_pos = i * Q_BLOCK + jnp.arange(Q_BLOCK)
        return fox_attend(q_i, k, v, c_i, c, q_pos, k_pos)

    o = lax.map(block, (qb, cb, jnp.arange(nb)))
    o = jnp.swapaxes(o, 0, 1).reshape(B, T, D)
    return o @ w_o, k, v, logf


def fox_sample(x, ck, cv, clogf, w_in, b_f, w_o):
    B, T, D = x.shape
    P = ck.shape[1]
    q, k, v, logf = fox_project(x, w_in, b_f)
    k_all = jnp.concatenate([ck.astype(k.dtype), k], axis=1)
    v_all = jnp.concatenate([cv.astype(v.dtype), v], axis=1)
    c = jnp.cumsum(jnp.concatenate([clogf.astype(jnp.float32), logf], axis=1), axis=1)
    o = fox_attend(q, k_all, v_all, c[:, P:], c, P + jnp.arange(T), jnp.arange(P + T))
    return o.reshape(B, T, D) @ w_o, k, v, logf


def setup_inputs(seed: int = 0) -> dict:
    key = jax.random.key(seed)
    ks = iter(jax.random.split(key, 40))

    def nrm(shape, scale):
        return jax.random.normal(next(ks), shape, jnp.float32) * scale

    def unif(shape, lo, hi):
        return jax.random.uniform(next(ks), shape, jnp.float32, lo, hi)

    D = D_MODEL
    fox_col_scale = jnp.concatenate([jnp.ones((2 * D,)), jnp.full((D,), BETA), jnp.ones((N_HEADS_B,))]) * D ** -0.5
    return {
        "x_prompt": nrm((BATCH, SEQ, D), 1.0),
        "x_sample": nrm((DEC_BATCH, DEC_SEQ, D), 1.0),
        "state_wkv": nrm((N_RWKV, DEC_BATCH, N_HEADS_A, HEAD_A, HEAD_A), 0.3),
        "state_shift": nrm((N_RWKV, DEC_BATCH, D), 1.0),
        "cache_k": nrm((N_FOX, DEC_BATCH, PAST_LEN, N_HEADS_B, HEAD_B), 1.0),
        "cache_v": nrm((N_FOX, DEC_BATCH, PAST_LEN, N_HEADS_B, HEAD_B), BETA),
        "cache_logf": jax.nn.log_sigmoid(2.0 + nrm((N_FOX, DEC_BATCH, PAST_LEN, N_HEADS_B), 1.0)),
        "rwkv_mu": unif((N_RWKV, 6, D), 0.0, 1.0),
        "rwkv_w0": unif((N_RWKV, D), -5.0, -1.0),
        "rwkv_w1": nrm((N_RWKV, D, DECAY_LORA), D ** -0.5),
        "rwkv_w2": nrm((N_RWKV, DECAY_LORA, D), 0.1 * DECAY_LORA ** -0.5),
        "rwkv_a0": nrm((N_RWKV, D), 0.1),
        "rwkv_a1": nrm((N_RWKV, D, AAA_LORA), D ** -0.5),
        "rwkv_a2": nrm((N_RWKV, AAA_LORA, D), 0.1 * AAA_LORA ** -0.5),
        "rwkv_g1": nrm((N_RWKV, D, GATE_LORA), D ** -0.5),
        "rwkv_g2": nrm((N_RWKV, GATE_LORA, D), GATE_LORA ** -0.5),
        "rwkv_k_k": 0.85 + nrm((N_RWKV, D), 0.05),
        "rwkv_k_a": 1.0 + nrm((N_RWKV, D), 0.05),
        "rwkv_r_k": nrm((N_RWKV, N_HEADS_A, HEAD_A), 0.1),
        "rwkv_w_r": nrm((N_RWKV, D, D), D ** -0.5),
        "rwkv_w_k": nrm((N_RWKV, D, D), D ** -0.5),
        "rwkv_w_v": nrm((N_RWKV, D, D), BETA * D ** -0.5),
        "rwkv_w_o": nrm((N_RWKV, D, D), BETA * D ** -0.5),
        "rwkv_lnx_g": 1.0 + nrm((N_RWKV, D), 0.05),
        "rwkv_lnx_b": nrm((N_RWKV, D), 0.01),
        "fox_w_in": nrm((N_FOX, D, 3 * D + N_HEADS_B), 1.0) * fox_col_scale,
        "fox_b_f": 2.0 + nrm((N_FOX, N_HEADS_B), 0.5),
        "fox_w_o": nrm((N_FOX, D, D), BETA * D ** -0.5),
        "ffn_w1": nrm((DEPTH, D, D_FF), BETA * D ** -0.5),
        "ffn_w2": nrm((DEPTH, D_FF, D), BETA * D_FF ** -0.5),
        "ln_mix_g": 1.0 + nrm((DEPTH, D), 0.05),
        "ln_mix_b": nrm((DEPTH, D), 0.01),
        "ln_ffn_g": 1.0 + nrm((DEPTH, D), 0.05),
        "ln_ffn_b": nrm((DEPTH, D), 0.01),
    }


def reference(x_prompt, x_sample, state_wkv, state_shift, cache_k, cache_v, cache_logf,
              rwkv_mu, rwkv_w0, rwkv_w1, rwkv_w2, rwkv_a0, rwkv_a1, rwkv_a2, rwkv_g1, rwkv_g2,
              rwkv_k_k, rwkv_k_a, rwkv_r_k, rwkv_w_r, rwkv_w_k, rwkv_w_v, rwkv_w_o,
              rwkv_lnx_g, rwkv_lnx_b, fox_w_in, fox_b_f, fox_w_o,
              ffn_w1, ffn_w2, ln_mix_g, ln_mix_b, ln_ffn_g, ln_ffn_b):
    xp, xs = x_prompt, x_sample
    bp = xp.shape[0]
    wkv_p, shift_p, k_p, v_p, lf_p = [], [], [], [], []
    wkv_s, shift_s, k_s, v_s, lf_s = [], [], [], [], []
    for i in range(DEPTH):
        j = i // N_MIXERS
        if i % N_MIXERS == 0:
            prm = (rwkv_mu[j], rwkv_w0[j], rwkv_w1[j], rwkv_w2[j], rwkv_a0[j], rwkv_a1[j], rwkv_a2[j],
                   rwkv_g1[j], rwkv_g2[j], rwkv_k_k[j], rwkv_k_a[j], rwkv_r_k[j], rwkv_w_r[j],
                   rwkv_w_k[j], rwkv_w_v[j], rwkv_w_o[j], rwkv_lnx_g[j], rwkv_lnx_b[j])
            zero_shift = jnp.zeros((bp, D_MODEL), xp.dtype)
            zero_wkv = jnp.zeros((bp, N_HEADS_A, HEAD_A, HEAD_A), state_wkv.dtype)
            hp, sh_p, S_p = rwkv_time_mix(xp, zero_shift, zero_wkv, *prm)
            hs, sh_s, S_s = rwkv_time_mix(xs, state_shift[j], state_wkv[j], *prm)
            wkv_p.append(S_p); shift_p.append(sh_p)
            wkv_s.append(S_s); shift_s.append(sh_s)
        else:
            hp, kp_, vp_, lfp_ = fox_prompt(xp, fox_w_in[j], fox_b_f[j], fox_w_o[j])
            hs, ks_, vs_, lfs_ = fox_sample(xs, cache_k[j], cache_v[j], cache_logf[j],
                                            fox_w_in[j], fox_b_f[j], fox_w_o[j])
            k_p.append(kp_); v_p.append(vp_); lf_p.append(lfp_)
            k_s.append(ks_); v_s.append(vs_); lf_s.append(lfs_)
        xp = layer_norm(ALPHA * xp + hp, ln_mix_g[i], ln_mix_b[i])
        xs = layer_norm(ALPHA * xs + hs, ln_mix_g[i], ln_mix_b[i])
        xp = layer_norm(ALPHA * xp + sq_relu_mlp(xp, ffn_w1[i], ffn_w2[i]), ln_ffn_g[i], ln_ffn_b[i])
        xs = layer_norm(ALPHA * xs + sq_relu_mlp(xs, ffn_w1[i], ffn_w2[i]), ln_ffn_g[i], ln_ffn_b[i])
    return (xp, xs,
            jnp.stack(wkv_p), jnp.stack(shift_p), jnp.stack(k_p), jnp.stack(v_p), jnp.stack(lf_p),
            jnp.stack(wkv_s), jnp.stack(shift_s), jnp.stack(k_s), jnp.stack(v_s), jnp.stack(lf_s))
```

```python
import functools
import math

import jax
import jax.numpy as jnp
from jax import lax
from jax.experimental import pallas as pl
from jax.experimental.pallas import tpu as pltpu

BF16 = jnp.bfloat16
F32 = jnp.float32

HEAD = 64
HEAD_SHIFT = 6
LANES = 128
SLAB = 256
HEADS_PER_SLAB = SLAB // HEAD
CHUNK = 64
LN_EPS = 1e-5
GN_EPS = 64e-5
EXP_M05 = math.exp(-0.5)
VMEM_LIMIT = 56 * 1024 * 1024


def _dot(a, b):
    return jnp.dot(a, b, preferred_element_type=F32)


def _dot_nt(a, b):
    return lax.dot_general(a, b, (((1,), (1,)), ((), ())), preferred_element_type=F32)


def _dot_tn(a, b):
    return lax.dot_general(a, b, (((0,), (0,)), ((), ())), preferred_element_type=F32)


def _split2(x):
    hi = x.astype(BF16)
    lo = (x - hi.astype(F32)).astype(BF16)
    return hi, lo


def _split3(x):
    p1 = x.astype(BF16)
    r1 = x - p1.astype(F32)
    p2 = r1.astype(BF16)
    p3 = (r1 - p2.astype(F32)).astype(BF16)
    return p1, p2, p3


def _layer_norm(z, g, b):
    mu = jnp.mean(z, axis=-1, keepdims=True)
    zc = z - mu
    var = jnp.mean(zc * zc, axis=-1, keepdims=True)
    return zc * lax.rsqrt(var + LN_EPS) * g + b


def _params(*sem):
    return pltpu.CompilerParams(dimension_semantics=sem, vmem_limit_bytes=VMEM_LIMIT)


def _const_spec(shape):
    nd = len(shape)
    return pl.BlockSpec(shape, lambda *_: (0,) * nd, pipeline_mode=pl.Buffered(1))


def _row_tile(n, want):
    t = min(n, want)
    assert n % t == 0 and t % 8 == 0, (n, t)
    return t


def _rwkv_proj_kernel(x_ref, xp_ref, mu_ref, w0_ref, a0_ref, wr_ref, wk_ref, wv_ref,
                      w1_ref, w2_ref, a1_ref, a2_ref, g1_ref, g2_ref,
                      r_ref, lw_ref, k_ref, v_ref, a_ref, g_ref):
    x = x_ref[...]
    xx = xp_ref[...] - x

    def mix(i):
        return (x + xx * mu_ref[i:i + 1, :]).astype(BF16)

    r_ref[...] = _dot(mix(0), wr_ref[...])
    w_raw = w0_ref[...] + _dot(jnp.tanh(_dot(mix(1), w1_ref[...])).astype(BF16), w2_ref[...])
    lw_ref[...] = -EXP_M05 * jax.nn.sigmoid(w_raw)
    k_ref[...] = _dot(mix(2), wk_ref[...])
    v_ref[...] = _dot(mix(3), wv_ref[...])
    a_ref[...] = jax.nn.sigmoid(a0_ref[...] + _dot(_dot(mix(4), a1_ref[...]).astype(BF16), a2_ref[...]))
    g_ref[...] = _dot(jax.nn.sigmoid(_dot(mix(5), g1_ref[...])).astype(BF16), g2_ref[...])


def _rwkv_proj(x, xp, mu, w0, a0, wr, wk, wv, w1, w2, a1, a2, g1, g2, tm):
    n, d = x.shape
    tm = _row_tile(n, tm)
    row = pl.BlockSpec((tm, d), lambda i: (i, 0))
    consts = [mu, w0, a0, wr, wk, wv, w1, w2, a1, a2, g1, g2]
    return pl.pallas_call(
        _rwkv_proj_kernel,
        grid=(n // tm,),
        in_specs=[row, row] + [_const_spec(c.shape) for c in consts],
        out_specs=[row] * 6,
        out_shape=[jax.ShapeDtypeStruct((n, d), F32)] * 6,
        compiler_params=_params("parallel"),
        name="rwkv_proj",
    )(x, xp, *consts)


def _wkv_kernel(r_ref, lw_ref, k_ref, v_ref, a_ref, kk_ref, ka_ref, rk_ref, lg_ref, lb_ref, h0_ref,
                o_ref, hout_ref, h_scr):
    c = pl.program_id(1)
    nslab = h_scr.shape[0]
    C = CHUNK

    @pl.when(c == 0)
    def _():
        h_scr[...] = h0_ref[0]

    row_s = lax.broadcasted_iota(jnp.int32, (SLAB, SLAB), 0)
    col_s = lax.broadcasted_iota(jnp.int32, (SLAB, SLAB), 1)
    mask_bd = (row_s >> HEAD_SHIFT) == (col_s >> HEAD_SHIFT)
    eye_s = row_s == col_s
    row_c = lax.broadcasted_iota(jnp.int32, (C, SLAB), 0)
    colin_c = lax.broadcasted_iota(jnp.int32, (C, SLAB), 1) & (C - 1)
    strict_lo = colin_c < row_c
    incl_lo = colin_c <= row_c
    eye_c = colin_c == row_c
    ones_bd = jnp.where(mask_bd, 1.0, 0.0).astype(BF16)
    tri = jnp.where(lax.broadcasted_iota(jnp.int32, (C, C), 1) <= lax.broadcasted_iota(jnp.int32, (C, C), 0),
                    1.0, 0.0).astype(BF16)

    def bd(x):
        xb = x.astype(BF16)
        return jnp.where(mask_bd, jnp.concatenate([xb] * HEADS_PER_SLAB, axis=0), jnp.zeros((), BF16))

    def head_sum(x):
        hi, lo = _split2(x)
        return _dot(hi, ones_bd) + _dot(lo, ones_bd)

    lw_all = lw_ref[0]
    p1, p2, p3 = _split3(lw_all)
    L_all = _dot(tri, p1) + _dot(tri, p2) + _dot(tri, p3)

    for s in range(nslab):
        sl = slice(s * SLAB, (s + 1) * SLAB)
        r = r_ref[0, :, sl]
        k0 = k_ref[0, :, sl]
        v = v_ref[0, :, sl]
        asig = a_ref[0, :, sl]
        lw = lw_all[:, sl]
        L = L_all[:, sl]

        kk = k0 * kk_ref[:, sl]
        nrm = jnp.sqrt(head_sum(kk * kk))
        kk = kk / jnp.maximum(nrm, 1e-12)
        k = k0 * (1.0 + (asig - 1.0) * ka_ref[:, sl])
        av = -kk
        bv = kk * asig

        LC = L[C - 1:C, :]
        eL = jnp.exp(L)
        eLx = jnp.exp(L - lw)
        enL = jnp.exp(-L)
        eLc = jnp.exp(LC - L)
        At = av * eLx
        Rt = r * eL
        Bt = bv * enL
        Kt = k * enL
        Bh = bv * eLc
        Kh = k * eLc
        PC = jnp.exp(LC)

        lhs = jnp.concatenate([At, Rt], axis=0).astype(BF16)
        ab = _dot_nt(lhs, bd(Bt))
        ak = _dot_nt(lhs, bd(Kt))
        N = jnp.where(strict_lo, ab[:C], 0.0)
        M = jnp.where(strict_lo, ak[:C], 0.0)
        Arb = jnp.where(incl_lo, ab[C:], 0.0)
        Ark = jnp.where(incl_lo, ak[C:], 0.0)

        X = jnp.where(eye_c, 1.0, 0.0) + N
        Pw = N
        for _ in range(int(math.log2(C)) - 1):
            Pw = _dot(Pw.astype(BF16), bd(Pw))
            X = X + _dot(X.astype(BF16), bd(Pw))

        MV = _dot(M.astype(BF16), bd(v))
        wu = _dot(X.astype(BF16), jnp.concatenate([bd(At), bd(MV)], axis=1))
        WA = wu[:, :SLAB]
        UV = wu[:, SLAB:]
        Yl = _dot(jnp.concatenate([Arb, Ark], axis=1).astype(BF16),
                  jnp.concatenate([bd(UV), bd(v)], axis=0))
        Rp = Rt + _dot(Arb.astype(BF16), bd(WA))
        G = jnp.where(mask_bd, _dot_tn(Bh.astype(BF16), WA.astype(BF16)), 0.0)
        G = G + jnp.where(eye_s, PC, 0.0)
        J = jnp.where(mask_bd,
                      _dot_tn(jnp.concatenate([Bh, Kh], axis=0).astype(BF16),
                              jnp.concatenate([UV, v], axis=0).astype(BF16)), 0.0)

        Hb = h_scr[s].astype(BF16)
        y = _dot(Rp.astype(BF16), Hb) + Yl
        h_scr[s] = _dot(G.astype(BF16), Hb) + J

        mu = head_sum(y) * (1.0 / HEAD)
        yc = y - mu
        var = head_sum(yc * yc) * (1.0 / HEAD)
        yn = yc * lax.rsqrt(var + GN_EPS) * lg_ref[:, sl] + lb_ref[:, sl]
        bonus = head_sum(r * k * rk_ref[:, sl]) * v
        o_ref[0, :, sl] = yn + bonus

    @pl.when(c == pl.num_programs(1) - 1)
    def _():
        hout_ref[0] = h_scr[...]


def _wkv(r, lw, k, v, a, k_k, k_a, r_k, lnx_g, lnx_b, h0):
    b, t, d = r.shape
    nslab = d // SLAB
    seq = pl.BlockSpec((1, CHUNK, d), lambda i, c: (i, c, 0))
    vec = pl.BlockSpec((1, d), lambda i, c: (0, 0))
    hspec = pl.BlockSpec((1, nslab, SLAB, SLAB), lambda i, c: (i, 0, 0, 0))
    return pl.pallas_call(
        _wkv_kernel,
        grid=(b, t // CHUNK),
        in_specs=[seq] * 5 + [vec] * 5 + [hspec],
        out_specs=[seq, hspec],
        out_shape=[jax.ShapeDtypeStruct((b, t, d), F32),
                   jax.ShapeDtypeStruct((b, nslab, SLAB, SLAB), F32)],
        scratch_shapes=[pltpu.VMEM((nslab, SLAB, SLAB), F32)],
        compiler_params=_params("parallel", "arbitrary"),
        name="wkv_chunked",
    )(r, lw, k, v, a, k_k, k_a, r_k, lnx_g, lnx_b, h0)


def _post_kernel(*refs, alpha, gated, ff_chunk):
    if gated:
        pre_ref, gate_ref, x_ref, wo_ref, w1_ref, w2_ref, ln_ref, out_ref = refs
        pre = pre_ref[...] * gate_ref[...]
    else:
        pre_ref, x_ref, wo_ref, w1_ref, w2_ref, ln_ref, out_ref = refs
        pre = pre_ref[...]
    h = _dot(pre.astype(BF16), wo_ref[...])
    x1 = _layer_norm(alpha * x_ref[...] + h, ln_ref[0:1, :], ln_ref[1:2, :])
    x1b = x1.astype(BF16)
    m = jnp.zeros_like(x1)
    for f in range(w1_ref.shape[1] // ff_chunk):
        fs = slice(f * ff_chunk, (f + 1) * ff_chunk)
        hid = jnp.maximum(_dot(x1b, w1_ref[:, fs]), 0.0)
        m = m + _dot((hid * hid).astype(BF16), w2_ref[fs, :])
    out_ref[...] = _layer_norm(alpha * x1 + m, ln_ref[2:3, :], ln_ref[3:4, :])


def _post(pre, gate, x, wo, w1, w2, ln, alpha, tm):
    n, d = x.shape
    tm = _row_tile(n, tm)
    row = pl.BlockSpec((tm, d), lambda i: (i, 0))
    acts = [pre, x] if gate is None else [pre, gate, x]
    consts = [wo, w1, w2, ln]
    return pl.pallas_call(
        functools.partial(_post_kernel, alpha=alpha, gated=gate is not None, ff_chunk=min(1024, w1.shape[1])),
        grid=(n // tm,),
        in_specs=[row] * len(acts) + [_const_spec(c.shape) for c in consts],
        out_specs=row,
        out_shape=jax.ShapeDtypeStruct((n, d), F32),
        compiler_params=_params("parallel"),
        name="post_mlp",
    )(*acts, *consts)


def _fox_proj_kernel(x_ref, wq_ref, wk_ref, wv_ref, wf_ref, bf_ref,
                     q_ref, k_ref, v_ref, kb_ref, vb_ref, lf_ref):
    xb = x_ref[...].astype(BF16)
    q_ref[...] = (_dot(xb, wq_ref[...]) * (HEAD ** -0.5)).astype(BF16)
    k = _dot(xb, wk_ref[...])
    v = _dot(xb, wv_ref[...])
    k_ref[...] = k
    v_ref[...] = v
    kb_ref[...] = k.astype(BF16)
    vb_ref[...] = v.astype(BF16)
    f = _dot(xb, wf_ref[...]) + bf_ref[...]
    lf_ref[...] = jnp.minimum(f, 0.0) - jnp.log1p(jnp.exp(-jnp.abs(f)))


def _fox_proj(x, wq, wk, wv, wf, bf, tm):
    n, d = x.shape
    tm = _row_tile(n, tm)
    row = pl.BlockSpec((tm, d), lambda i: (i, 0))
    rowf = pl.BlockSpec((tm, LANES), lambda i: (i, 0))
    consts = [wq, wk, wv, wf, bf]
    return pl.pallas_call(
        _fox_proj_kernel,
        grid=(n // tm,),
        in_specs=[row] + [_const_spec(c.shape) for c in consts],
        out_specs=[row] * 5 + [rowf],
        out_shape=[jax.ShapeDtypeStruct((n, d), BF16), jax.ShapeDtypeStruct((n, d), F32),
                   jax.ShapeDtypeStruct((n, d), F32), jax.ShapeDtypeStruct((n, d), BF16),
                   jax.ShapeDtypeStruct((n, d), BF16), jax.ShapeDtypeStruct((n, LANES), F32)],
        compiler_params=_params("parallel"),
        name="fox_proj",
    )(x, *consts)


CS_TILE = 256


def _cumsum_kernel(lf_ref, ccol_ref, crow_ref, carry_ref):
    @pl.when(pl.program_id(1) == 0)
    def _():
        carry_ref[...] = jnp.zeros_like(carry_ref)

    n = CS_TILE
    tri = jnp.where(lax.broadcasted_iota(jnp.int32, (n, n), 1) <= lax.broadcasted_iota(jnp.int32, (n, n), 0),
                    1.0, 0.0).astype(BF16)
    eye = jnp.where(lax.broadcasted_iota(jnp.int32, (LANES, LANES), 1)
                    == lax.broadcasted_iota(jnp.int32, (LANES, LANES), 0), 1.0, 0.0).astype(BF16)
    p1, p2, p3 = _split3(lf_ref[0])
    cs = _dot(tri, p1) + _dot(tri, p2) + _dot(tri, p3) + carry_ref[0:1, :]
    ccol_ref[0] = cs
    carry_ref[...] = jnp.broadcast_to(cs[n - 1:n, :], carry_ref.shape)
    q1, q2, q3 = _split3(cs)
    crow_ref[0] = _dot_nt(eye, q1) + _dot_nt(eye, q2) + _dot_nt(eye, q3)


def _cumsum(lf):
    b, l, w = lf.shape
    assert l % CS_TILE == 0 and w == LANES
    return pl.pallas_call(
        _cumsum_kernel,
        grid=(b, l // CS_TILE),
        in_specs=[pl.BlockSpec((1, CS_TILE, w), lambda i, j: (i, j, 0))],
        out_specs=[pl.BlockSpec((1, CS_TILE, w), lambda i, j: (i, j, 0)),
                   pl.BlockSpec((1, w, CS_TILE), lambda i, j: (i, 0, j))],
        out_shape=[jax.ShapeDtypeStruct((b, l, w), F32), jax.ShapeDtypeStruct((b, w, l), F32)],
        scratch_shapes=[pltpu.VMEM((8, w), F32)],
        compiler_params=_params("parallel", "arbitrary"),
        name="logf_cumsum",
    )(lf)


def _flash_kernel(q_ref, k_ref, v_ref, cq_ref, ck_ref, o_ref, *, tq):
    i = pl.program_id(2)
    lane = lax.broadcasted_iota(jnp.int32, (1, LANES), 1)
    q = q_ref[0]
    qpos = i * tq + lax.broadcasted_iota(jnp.int32, (tq, tq), 0)
    kpos0 = lax.broadcasted_iota(jnp.int32, (tq, tq), 1)
    outs = []
    for hh in range(LANES // HEAD):
        in_head = (lane >> HEAD_SHIFT) == hh
        qh = jnp.where(in_head, q, jnp.zeros((), BF16))
        cq = cq_ref[0, 0, :, hh:hh + 1]

        def body(j, carry, qh=qh, cq=cq, hh=hh):
            m, l, acc = carry
            off = pl.multiple_of(j * tq, tq)
            kj = k_ref[0, pl.ds(off, tq), :]
            vj = v_ref[0, pl.ds(off, tq), :]
            ck = ck_ref[0, 0, hh:hh + 1, pl.ds(off, tq)]
            s = _dot_nt(qh, kj) + cq - ck
            s = jnp.where(kpos0 + off <= qpos, s, -jnp.inf)
            m_new = jnp.maximum(m, jnp.max(s, axis=-1, keepdims=True))
            alpha = jnp.exp(m - m_new)
            p = jnp.exp(s - m_new)
            l = alpha * l + jnp.sum(p, axis=-1, keepdims=True)
            acc = alpha * acc + _dot(p.astype(BF16), vj)
            return m_new, l, acc

        init = (jnp.full((tq, 1), -jnp.inf, F32), jnp.zeros((tq, 1), F32), jnp.zeros((tq, LANES), F32))
        m, l, acc = lax.fori_loop(0, i + 1, body, init)
        outs.append(acc / l)
    o_ref[0] = jnp.where((lane >> HEAD_SHIFT) == 0, outs[0], outs[1])


def _flash(q, k, v, cq, ck, tq):
    b, t, d = q.shape
    tq = _row_tile(t, tq)
    g = d // LANES
    return pl.pallas_call(
        functools.partial(_flash_kernel, tq=tq),
        grid=(b, g, t // tq),
        in_specs=[pl.BlockSpec((1, tq, LANES), lambda bi, gi, i: (bi, i, gi)),
                  pl.BlockSpec((1, t, LANES), lambda bi, gi, i: (bi, 0, gi)),
                  pl.BlockSpec((1, t, LANES), lambda bi, gi, i: (bi, 0, gi)),
                  pl.BlockSpec((1, 1, tq, 2), lambda bi, gi, i: (bi, gi, i, 0)),
                  pl.BlockSpec((1, 1, 2, t), lambda bi, gi, i: (bi, gi, 0, 0))],
        out_specs=pl.BlockSpec((1, tq, LANES), lambda bi, gi, i: (bi, i, gi)),
        out_shape=jax.ShapeDtypeStruct((b, t, d), F32),
        compiler_params=_params("parallel", "parallel", "arbitrary"),
        name="fox_flash",
    )(q, k, v, cq, ck)


def _decode_attn_kernel(q_ref, ck_ref, cv_ref, kn_ref, vn_ref, cq_ref, cc_ref, cn_ref, o_ref):
    lane = lax.broadcasted_iota(jnp.int32, (1, LANES), 1)
    q = q_ref[0]
    t = q.shape[0]
    kc = ck_ref[0].astype(BF16)
    vc = cv_ref[0].astype(BF16)
    kn = kn_ref[0]
    vn = vn_ref[0]
    causal = lax.broadcasted_iota(jnp.int32, (t, t), 1) <= lax.broadcasted_iota(jnp.int32, (t, t), 0)
    outs = []
    for hh in range(LANES // HEAD):
        qh = jnp.where((lane >> HEAD_SHIFT) == hh, q, jnp.zeros((), BF16))
        cq = cq_ref[0, 0, :, hh:hh + 1]
        s1 = _dot_nt(qh, kc) + cq - cc_ref[0, 0, hh:hh + 1, :]
        s2 = _dot_nt(qh, kn) + cq - cn_ref[0, 0, hh:hh + 1, :]
        s2 = jnp.where(causal, s2, -jnp.inf)
        m = jnp.maximum(jnp.max(s1, axis=-1, keepdims=True), jnp.max(s2, axis=-1, keepdims=True))
        p1 = jnp.exp(s1 - m)
        p2 = jnp.exp(s2 - m)
        l = jnp.sum(p1, axis=-1, keepdims=True) + jnp.sum(p2, axis=-1, keepdims=True)
        outs.append((_dot(p1.astype(BF16), vc) + _dot(p2.astype(BF16), vn)) / l)
    o_ref[0] = jnp.where((lane >> HEAD_SHIFT) == 0, outs[0], outs[1])


def _decode_attn(q, cache_k, cache_v, kn, vn, cq, cc, cn):
    b, t, d = q.shape
    p = cache_k.shape[1]
    g = d // LANES
    new = pl.BlockSpec((1, t, LANES), lambda bi, gi: (bi, 0, gi))
    old = pl.BlockSpec((1, p, LANES), lambda bi, gi: (bi, 0, gi))
    return pl.pallas_call(
        _decode_attn_kernel,
        grid=(b, g),
        in_specs=[new, old, old, new, new,
                  pl.BlockSpec((1, 1, t, 2), lambda bi, gi: (bi, gi, 0, 0)),
                  pl.BlockSpec((1, 1, 2, p), lambda bi, gi: (bi, gi, 0, 0)),
                  pl.BlockSpec((1, 1, 2, t), lambda bi, gi: (bi, gi, 0, 0))],
        out_specs=new,
        out_shape=jax.ShapeDtypeStruct((b, t, d), F32),
        compiler_params=_params("parallel", "parallel"),
        name="fox_decode_attn",
    )(q, cache_k, cache_v, kn, vn, cq, cc, cn)


def _pad_cols(w, mult):
    pad = (-w.shape[-1]) % mult
    return jnp.pad(w, ((0, 0), (0, pad))) if pad else w


def _pad_rows(w, mult):
    pad = (-w.shape[0]) % mult
    return jnp.pad(w, ((0, pad), (0, 0))) if pad else w


def _pair_layout(c_col, c_row, heads):
    b, l, _ = c_col.shape
    g = heads // 2
    cq = c_col[:, :, :heads].reshape(b, l, g, 2).transpose(0, 2, 1, 3)
    ck = c_row[:, :heads, :].reshape(b, g, 2, l)
    return cq, ck


def _rwkv_layer(x, shift0, wkv0, p, tm):
    b, t, d = x.shape
    h = d // HEAD
    xp = jnp.concatenate([shift0[:, None, :], x[:, :-1]], axis=1)
    r, lw, k, v, a, g = _rwkv_proj(x.reshape(b * t, d), xp.reshape(b * t, d), *p["proj"], tm=tm)
    tpad = (-t) % CHUNK

    def seq(z):
        z = z.reshape(b, t, d)
        return jnp.pad(z, ((0, 0), (0, tpad), (0, 0))) if tpad else z

    ht = jnp.swapaxes(wkv0.astype(F32), -1, -2).reshape(b, d // SLAB, HEADS_PER_SLAB, HEAD, HEAD)
    eye = jnp.eye(HEADS_PER_SLAB, dtype=F32)
    h0 = jnp.einsum("bshkv,hg->bshkgv", ht, eye).reshape(b, d // SLAB, SLAB, SLAB)
    o, hfin = _wkv(seq(r), seq(lw), seq(k), seq(v), seq(a), *p["wkv"], h0)
    hfin = hfin.reshape(b, d // SLAB, HEADS_PER_SLAB, HEAD, HEADS_PER_SLAB, HEAD)
    hfin = jnp.einsum("bshkgv,hg->bshkv", hfin, eye).reshape(b, h, HEAD, HEAD)
    s_fin = jnp.swapaxes(hfin, -1, -2).astype(wkv0.dtype)
    return o[:, :t].reshape(b * t, d), g, x[:, -1], s_fin


def kernel(x_prompt, x_sample, state_wkv, state_shift, cache_k, cache_v, cache_logf, rwkv_mu, rwkv_w0, rwkv_w1, rwkv_w2, rwkv_a0, rwkv_a1, rwkv_a2, rwkv_g1, rwkv_g2, rwkv_k_k, rwkv_k_a, rwkv_r_k, rwkv_w_r, rwkv_w_k, rwkv_w_v, rwkv_w_o, rwkv_lnx_g, rwkv_lnx_b, fox_w_in, fox_b_f, fox_w_o, ffn_w1, ffn_w2, ln_mix_g, ln_mix_b, ln_ffn_g, ln_ffn_b):
    depth = ln_mix_g.shape[0]
    alpha = (2 * depth) ** 0.25
    bp, tp, d = x_prompt.shape
    bs, ts, _ = x_sample.shape
    heads = d // HEAD
    past = cache_k.shape[2]
    tm = 256

    xp = x_prompt
    xs = x_sample
    outs_p = {n: [] for n in ("wkv", "shift", "k", "v", "lf")}
    outs_s = {n: [] for n in ("wkv", "shift", "k", "v", "lf")}
    for i in range(depth):
        j = i // 2
        ln = jnp.stack([ln_mix_g[i], ln_mix_b[i], ln_ffn_g[i], ln_ffn_b[i]])
        w1 = ffn_w1[i].astype(BF16)
        w2 = ffn_w2[i].astype(BF16)
        if i % 2 == 0:
            row = lambda z: z.reshape(1, d)
            params = {
                "proj": (rwkv_mu[j], row(rwkv_w0[j]), row(rwkv_a0[j]),
                         rwkv_w_r[j].astype(BF16), rwkv_w_k[j].astype(BF16), rwkv_w_v[j].astype(BF16),
                         _pad_cols(rwkv_w1[j], LANES).astype(BF16), _pad_rows(rwkv_w2[j], LANES).astype(BF16),
                         _pad_cols(rwkv_a1[j], LANES).astype(BF16), _pad_rows(rwkv_a2[j], LANES).astype(BF16),
                         _pad_cols(rwkv_g1[j], LANES).astype(BF16), _pad_rows(rwkv_g2[j], LANES).astype(BF16)),
                "wkv": (row(rwkv_k_k[j]), row(rwkv_k_a[j]), row(rwkv_r_k[j]), row(rwkv_lnx_g[j]), row(rwkv_lnx_b[j])),
            }
            wo = rwkv_w_o[j].astype(BF16)
            zero_shift = jnp.zeros((bp, d), xp.dtype)
            zero_wkv = jnp.zeros((bp, heads, HEAD, HEAD), state_wkv.dtype)
            for x, shift0, wkv0, outs in ((xp, zero_shift, zero_wkv, outs_p), (xs, state_shift[j], state_wkv[j], outs_s)):
                b, t, _ = x.shape
                o, g, sh, s_fin = _rwkv_layer(x, shift0, wkv0, params, tm)
                x_new = _post(o, g, x.reshape(b * t, d), wo, w1, w2, ln, alpha, tm).reshape(b, t, d)
                outs["wkv"].append(s_fin)
                outs["shift"].append(sh)
                if x is xp:
                    xp = x_new
                else:
                    xs = x_new
        else:
            w_in = fox_w_in[j]
            wq = w_in[:, :d].astype(BF16)
            wk = w_in[:, d:2 * d].astype(BF16)
            wv = w_in[:, 2 * d:3 * d].astype(BF16)
            wf = _pad_cols(w_in[:, 3 * d:], LANES).astype(BF16)
            bf = _pad_cols(fox_b_f[j].reshape(1, heads), LANES)
            wo = fox_w_o[j].astype(BF16)

            q, k, v, kb, vb, lf = _fox_proj(xp.reshape(bp * tp, d), wq, wk, wv, wf, bf, tm)
            c_col, c_row = _cumsum(lf.reshape(bp, tp, LANES))
            cq, ck = _pair_layout(c_col, c_row, heads)
            sh3 = lambda z: z.reshape(bp, tp, d)
            o = _flash(sh3(q), sh3(kb), sh3(vb), cq, ck, tq=256)
            outs_p["k"].append(k.reshape(bp, tp, heads, HEAD))
            outs_p["v"].append(v.reshape(bp, tp, heads, HEAD))
            outs_p["lf"].append(lf.reshape(bp, tp, LANES)[:, :, :heads])
            xp = _post(o.reshape(bp * tp, d), None, xp.reshape(bp * tp, d), wo, w1, w2, ln, alpha, tm).reshape(bp, tp, d)

            q, k, v, kb, vb, lf = _fox_proj(xs.reshape(bs * ts, d), wq, wk, wv, wf, bf, tm)
            lf_new = lf.reshape(bs, ts, LANES)
            lf_all = jnp.concatenate([_pad_cols(cache_logf[j].astype(F32).reshape(bs * past, heads), LANES)
                                      .reshape(bs, past, LANES), lf_new], axis=1)
            lpad = (-(past + ts)) % CS_TILE
            c_col, c_row = _cumsum(jnp.pad(lf_all, ((0, 0), (0, lpad), (0, 0))))
            cq, ck = _pair_layout(c_col, c_row, heads)
            sh3 = lambda z: z.reshape(bs, ts, d)
            o = _decode_attn(sh3(q), cache_k[j].reshape(bs, past, d), cache_v[j].reshape(bs, past, d),
                             sh3(kb), sh3(vb), cq[:, :, past:past + ts], ck[:, :, :, :past], ck[:, :, :, past:past + ts])
            outs_s["k"].append(k.reshape(bs, ts, heads, HEAD))
            outs_s["v"].append(v.reshape(bs, ts, heads, HEAD))
            outs_s["lf"].append(lf_new[:, :, :heads])
            xs = _post(o.reshape(bs * ts, d), None, xs.reshape(bs * ts, d), wo, w1, w2, ln, alpha, tm).reshape(bs, ts, d)

    st = jnp.stack
    return (xp, xs,
            st(outs_p["wkv"]), st(outs_p["shift"]), st(outs_p["k"]), st(outs_p["v"]), st(outs_p["lf"]),
            st(outs_s["wkv"]), st(outs_s["shift"]), st(outs_s["k"]), st(outs_s["v"]), st(outs_s["lf"]))
```

```python
import functools
import math

import jax
import jax.numpy as jnp
from jax import lax
from jax.experimental import pallas as pl
from jax.experimental.pallas import tpu as pltpu

BF16 = jnp.bfloat16
F32 = jnp.float32

HEAD = 64
HEAD_SHIFT = 6
LANES = 128
SLAB = 256
HEADS_PER_SLAB = SLAB // HEAD
CHUNK = 64
LN_EPS = 1e-5
GN_EPS = 64e-5
EXP_M05 = math.exp(-0.5)
LOG2E = math.log2(math.e)
VMEM_LIMIT = 56 * 1024 * 1024


def _dot(a, b):
    return jnp.dot(a, b, preferred_element_type=F32)


def _dot_nt(a, b):
    return lax.dot_general(a, b, (((1,), (1,)), ((), ())), preferred_element_type=F32)


def _dot_tn(a, b):
    return lax.dot_general(a, b, (((0,), (0,)), ((), ())), preferred_element_type=F32)


def _split2(x):
    hi = x.astype(BF16)
    lo = (x - hi.astype(F32)).astype(BF16)
    return hi, lo


def _split3(x):
    p1 = x.astype(BF16)
    r1 = x - p1.astype(F32)
    p2 = r1.astype(BF16)
    p3 = (r1 - p2.astype(F32)).astype(BF16)
    return p1, p2, p3


def _layer_norm(z, g, b):
    mu = jnp.mean(z, axis=-1, keepdims=True)
    zc = z - mu
    var = jnp.mean(zc * zc, axis=-1, keepdims=True)
    return zc * lax.rsqrt(var + LN_EPS) * g + b


def _params(*sem):
    return pltpu.CompilerParams(dimension_semantics=sem, vmem_limit_bytes=VMEM_LIMIT)


def _const_spec(shape):
    nd = len(shape)
    return pl.BlockSpec(shape, lambda *_: (0,) * nd, pipeline_mode=pl.Buffered(1))


def _row_tile(n, want):
    t = min(n, want)
    assert n % t == 0 and t % 8 == 0, (n, t)
    return t


def _rwkv_proj_kernel(x_ref, xp_ref, mu_ref, w0_ref, a0_ref, wr_ref, wk_ref, wv_ref,
                      w1_ref, w2_ref, a1_ref, a2_ref, g1_ref, g2_ref,
                      r_ref, lw_ref, k_ref, v_ref, a_ref, g_ref):
    x = x_ref[...]
    xx = xp_ref[...] - x

    def mix(i):
        return (x + xx * mu_ref[i:i + 1, :]).astype(BF16)

    r_ref[...] = _dot(mix(0), wr_ref[...])
    w_raw = w0_ref[...] + _dot(jnp.tanh(_dot(mix(1), w1_ref[...])).astype(BF16), w2_ref[...])
    lw_ref[...] = -EXP_M05 * jax.nn.sigmoid(w_raw)
    k_ref[...] = _dot(mix(2), wk_ref[...])
    v_ref[...] = _dot(mix(3), wv_ref[...])
    a_ref[...] = jax.nn.sigmoid(a0_ref[...] + _dot(_dot(mix(4), a1_ref[...]).astype(BF16), a2_ref[...]))
    g_ref[...] = _dot(jax.nn.sigmoid(_dot(mix(5), g1_ref[...])).astype(BF16), g2_ref[...])


def _rwkv_proj(x, xp, mu, w0, a0, wr, wk, wv, w1, w2, a1, a2, g1, g2, tm):
    n, d = x.shape
    tm = _row_tile(n, tm)
    row = pl.BlockSpec((tm, d), lambda i: (i, 0))
    consts = [mu, w0, a0, wr, wk, wv, w1, w2, a1, a2, g1, g2]
    return pl.pallas_call(
        _rwkv_proj_kernel,
        grid=(n // tm,),
        in_specs=[row, row] + [_const_spec(c.shape) for c in consts],
        out_specs=[row] * 6,
        out_shape=[jax.ShapeDtypeStruct((n, d), F32)] * 6,
        compiler_params=_params("parallel"),
        name="rwkv_proj",
    )(x, xp, *consts)


def _wkv_kernel(r_ref, lw_ref, k_ref, v_ref, a_ref, kk_ref, ka_ref, rk_ref, lg_ref, lb_ref, h0_ref,
                o_ref, hout_ref, h_scr):
    c = pl.program_id(1)
    nslab = h_scr.shape[0]
    C = CHUNK

    @pl.when(c == 0)
    def _():
        h_scr[...] = h0_ref[0]

    row_s = lax.broadcasted_iota(jnp.int32, (SLAB, SLAB), 0)
    col_s = lax.broadcasted_iota(jnp.int32, (SLAB, SLAB), 1)
    mask_bd = (row_s >> HEAD_SHIFT) == (col_s >> HEAD_SHIFT)
    eye_s = row_s == col_s
    row_c = lax.broadcasted_iota(jnp.int32, (C, SLAB), 0)
    colin_c = lax.broadcasted_iota(jnp.int32, (C, SLAB), 1) & (C - 1)
    strict_lo = colin_c < row_c
    incl_lo = colin_c <= row_c
    eye_c = colin_c == row_c
    ones_bd = jnp.where(mask_bd, 1.0, 0.0).astype(BF16)
    tri = jnp.where(lax.broadcasted_iota(jnp.int32, (C, C), 1) <= lax.broadcasted_iota(jnp.int32, (C, C), 0),
                    1.0, 0.0).astype(BF16)

    def bd(x):
        xb = x.astype(BF16)
        return jnp.where(mask_bd, jnp.concatenate([xb] * HEADS_PER_SLAB, axis=0), jnp.zeros((), BF16))

    def head_sum(x):
        hi, lo = _split2(x)
        return _dot(hi, ones_bd) + _dot(lo, ones_bd)

    lw_all = lw_ref[0]
    p1, p2, p3 = _split3(lw_all)
    L_all = _dot(tri, p1) + _dot(tri, p2) + _dot(tri, p3)

    slabs = [slice(s * SLAB, (s + 1) * SLAB) for s in range(nslab)]
    S = range(nslab)
    r = [r_ref[0, :, sl] for sl in slabs]
    v = [v_ref[0, :, sl] for sl in slabs]
    asig = [a_ref[0, :, sl] for sl in slabs]
    k0 = [k_ref[0, :, sl] for sl in slabs]
    kk = [k0[s] * kk_ref[:, slabs[s]] for s in S]
    nrm = [jnp.sqrt(head_sum(kk[s] * kk[s])) for s in S]
    kk = [kk[s] / jnp.maximum(nrm[s], 1e-12) for s in S]
    k = [k0[s] * (1.0 + (asig[s] - 1.0) * ka_ref[:, slabs[s]]) for s in S]
    bv = [kk[s] * asig[s] for s in S]
    L = [L_all[:, sl] for sl in slabs]
    LC = [L[s][C - 1:C, :] for s in S]
    enL = [jnp.exp(-L[s]) for s in S]
    eLc = [jnp.exp(LC[s] - L[s]) for s in S]
    At = [-kk[s] * jnp.exp(L[s] - lw_all[:, slabs[s]]) for s in S]
    Rt = [r[s] * jnp.exp(L[s]) for s in S]
    Bt = [bv[s] * enL[s] for s in S]
    Kt = [k[s] * enL[s] for s in S]
    Bh = [bv[s] * eLc[s] for s in S]
    Kh = [k[s] * eLc[s] for s in S]

    lhs = [jnp.concatenate([At[s], Rt[s]], axis=0).astype(BF16) for s in S]
    ab = [_dot_nt(lhs[s], bd(Bt[s])) for s in S]
    ak = [_dot_nt(lhs[s], bd(Kt[s])) for s in S]
    N = [jnp.where(strict_lo, ab[s][:C], 0.0) for s in S]
    M = [jnp.where(strict_lo, ak[s][:C], 0.0) for s in S]
    Arb = [jnp.where(incl_lo, ab[s][C:], 0.0) for s in S]
    Ark = [jnp.where(incl_lo, ak[s][C:], 0.0) for s in S]
    MV = [_dot(M[s].astype(BF16), bd(v[s])) for s in S]

    X = [jnp.where(eye_c, 1.0, 0.0) + N[s] for s in S]
    Pw = N
    for _ in range(int(math.log2(C)) - 1):
        Pw = [_dot(Pw[s].astype(BF16), bd(Pw[s])) for s in S]
        X = [X[s] + _dot(X[s].astype(BF16), bd(Pw[s])) for s in S]

    wu = [_dot(X[s].astype(BF16), jnp.concatenate([bd(At[s]), bd(MV[s])], axis=1)) for s in S]
    WA = [wu[s][:, :SLAB] for s in S]
    UV = [wu[s][:, SLAB:] for s in S]
    Yl = [_dot(jnp.concatenate([Arb[s], Ark[s]], axis=1).astype(BF16),
               jnp.concatenate([bd(UV[s]), bd(v[s])], axis=0)) for s in S]
    Rp = [Rt[s] + _dot(Arb[s].astype(BF16), bd(WA[s])) for s in S]
    G = [jnp.where(mask_bd, _dot_tn(Bh[s].astype(BF16), WA[s].astype(BF16)), 0.0)
         + jnp.where(eye_s, jnp.exp(LC[s]), 0.0) for s in S]
    J = [jnp.where(mask_bd,
                   _dot_tn(jnp.concatenate([Bh[s], Kh[s]], axis=0).astype(BF16),
                           jnp.concatenate([UV[s], v[s]], axis=0).astype(BF16)), 0.0) for s in S]
    Hb = [h_scr[s].astype(BF16) for s in S]
    y = [_dot(Rp[s].astype(BF16), Hb[s]) + Yl[s] for s in S]
    for s in S:
        h_scr[s] = _dot(G[s].astype(BF16), Hb[s]) + J[s]

    mu = [head_sum(y[s]) * (1.0 / HEAD) for s in S]
    yc = [y[s] - mu[s] for s in S]
    var = [head_sum(yc[s] * yc[s]) * (1.0 / HEAD) for s in S]
    bonus = [head_sum(r[s] * k[s] * rk_ref[:, slabs[s]]) * v[s] for s in S]
    for s in S:
        sl = slabs[s]
        o_ref[0, :, sl] = yc[s] * lax.rsqrt(var[s] + GN_EPS) * lg_ref[:, sl] + lb_ref[:, sl] + bonus[s]

    @pl.when(c == pl.num_programs(1) - 1)
    def _():
        hout_ref[0] = h_scr[...]


def _wkv(r, lw, k, v, a, k_k, k_a, r_k, lnx_g, lnx_b, h0):
    b, t, d = r.shape
    nslab = d // SLAB
    seq = pl.BlockSpec((1, CHUNK, d), lambda i, c: (i, c, 0))
    vec = pl.BlockSpec((1, d), lambda i, c: (0, 0))
    hspec = pl.BlockSpec((1, nslab, SLAB, SLAB), lambda i, c: (i, 0, 0, 0))
    return pl.pallas_call(
        _wkv_kernel,
        grid=(b, t // CHUNK),
        in_specs=[seq] * 5 + [vec] * 5 + [hspec],
        out_specs=[seq, hspec],
        out_shape=[jax.ShapeDtypeStruct((b, t, d), F32),
                   jax.ShapeDtypeStruct((b, nslab, SLAB, SLAB), F32)],
        scratch_shapes=[pltpu.VMEM((nslab, SLAB, SLAB), F32)],
        compiler_params=_params("parallel", "arbitrary"),
        name="wkv_chunked",
    )(r, lw, k, v, a, k_k, k_a, r_k, lnx_g, lnx_b, h0)


def _post_kernel(*refs, alpha, gated, ff_chunk):
    if gated:
        pre_ref, gate_ref, x_ref, wo_ref, w1_ref, w2_ref, ln_ref, out_ref = refs
        pre = pre_ref[...] * gate_ref[...]
    else:
        pre_ref, x_ref, wo_ref, w1_ref, w2_ref, ln_ref, out_ref = refs
        pre = pre_ref[...]
    h = _dot(pre.astype(BF16), wo_ref[...])
    x1 = _layer_norm(alpha * x_ref[...] + h, ln_ref[0:1, :], ln_ref[1:2, :])
    x1b = x1.astype(BF16)
    m = jnp.zeros_like(x1)
    for f in range(w1_ref.shape[1] // ff_chunk):
        fs = slice(f * ff_chunk, (f + 1) * ff_chunk)
        hid = jnp.maximum(_dot(x1b, w1_ref[:, fs]), 0.0)
        m = m + _dot((hid * hid).astype(BF16), w2_ref[fs, :])
    out_ref[...] = _layer_norm(alpha * x1 + m, ln_ref[2:3, :], ln_ref[3:4, :])


def _post(pre, gate, x, wo, w1, w2, ln, alpha, tm):
    n, d = x.shape
    tm = _row_tile(n, tm)
    row = pl.BlockSpec((tm, d), lambda i: (i, 0))
    acts = [pre, x] if gate is None else [pre, gate, x]
    consts = [wo, w1, w2, ln]
    return pl.pallas_call(
        functools.partial(_post_kernel, alpha=alpha, gated=gate is not None, ff_chunk=min(1024, w1.shape[1])),
        grid=(n // tm,),
        in_specs=[row] * len(acts) + [_const_spec(c.shape) for c in consts],
        out_specs=row,
        out_shape=jax.ShapeDtypeStruct((n, d), F32),
        compiler_params=_params("parallel"),
        name="post_mlp",
    )(*acts, *consts)


def _fox_proj_kernel(x_ref, wq_ref, wk_ref, wv_ref, wf_ref, bf_ref,
                     q_ref, k_ref, v_ref, kb_ref, vb_ref, lf_ref):
    xb = x_ref[...].astype(BF16)
    q_ref[...] = (_dot(xb, wq_ref[...]) * (HEAD ** -0.5 * LOG2E)).astype(BF16)
    k = _dot(xb, wk_ref[...])
    v = _dot(xb, wv_ref[...])
    k_ref[...] = k
    v_ref[...] = v
    kb_ref[...] = k.astype(BF16)
    vb_ref[...] = v.astype(BF16)
    f = _dot(xb, wf_ref[...]) + bf_ref[...]
    lf_ref[...] = jnp.minimum(f, 0.0) - jnp.log1p(jnp.exp(-jnp.abs(f)))


def _fox_proj(x, wq, wk, wv, wf, bf, tm):
    n, d = x.shape
    tm = _row_tile(n, tm)
    row = pl.BlockSpec((tm, d), lambda i: (i, 0))
    rowf = pl.BlockSpec((tm, LANES), lambda i: (i, 0))
    consts = [wq, wk, wv, wf, bf]
    return pl.pallas_call(
        _fox_proj_kernel,
        grid=(n // tm,),
        in_specs=[row] + [_const_spec(c.shape) for c in consts],
        out_specs=[row] * 5 + [rowf],
        out_shape=[jax.ShapeDtypeStruct((n, d), BF16), jax.ShapeDtypeStruct((n, d), F32),
                   jax.ShapeDtypeStruct((n, d), F32), jax.ShapeDtypeStruct((n, d), BF16),
                   jax.ShapeDtypeStruct((n, d), BF16), jax.ShapeDtypeStruct((n, LANES), F32)],
        compiler_params=_params("parallel"),
        name="fox_proj",
    )(x, *consts)


CS_TILE = 256


def _cumsum_kernel(lf_ref, ccol_ref, crow_ref, carry_ref):
    @pl.when(pl.program_id(1) == 0)
    def _():
        carry_ref[...] = jnp.zeros_like(carry_ref)

    n = CS_TILE
    tri = jnp.where(lax.broadcasted_iota(jnp.int32, (n, n), 1) <= lax.broadcasted_iota(jnp.int32, (n, n), 0),
                    1.0, 0.0).astype(BF16)
    eye = jnp.where(lax.broadcasted_iota(jnp.int32, (LANES, LANES), 1)
                    == lax.broadcasted_iota(jnp.int32, (LANES, LANES), 0), 1.0, 0.0).astype(BF16)
    p1, p2, p3 = _split3(lf_ref[0])
    cs = _dot(tri, p1) + _dot(tri, p2) + _dot(tri, p3) + carry_ref[0:1, :]
    ccol_ref[0] = cs
    carry_ref[...] = jnp.broadcast_to(cs[n - 1:n, :], carry_ref.shape)
    q1, q2, q3 = _split3(cs)
    crow_ref[0] = _dot_nt(eye, q1) + _dot_nt(eye, q2) + _dot_nt(eye, q3)


def _cumsum(lf):
    b, l, w = lf.shape
    assert l % CS_TILE == 0 and w == LANES
    return pl.pallas_call(
        _cumsum_kernel,
        grid=(b, l // CS_TILE),
        in_specs=[pl.BlockSpec((1, CS_TILE, w), lambda i, j: (i, j, 0))],
        out_specs=[pl.BlockSpec((1, CS_TILE, w), lambda i, j: (i, j, 0)),
                   pl.BlockSpec((1, w, CS_TILE), lambda i, j: (i, 0, j))],
        out_shape=[jax.ShapeDtypeStruct((b, l, w), F32), jax.ShapeDtypeStruct((b, w, l), F32)],
        scratch_shapes=[pltpu.VMEM((8, w), F32)],
        compiler_params=_params("parallel", "arbitrary"),
        name="logf_cumsum",
    )(lf)


def _flash_kernel(q_ref, k_ref, v_ref, c_ref, o_ref, kaug_scr, *, tq):
    i = pl.program_id(2)
    t = k_ref.shape[1]
    nh = LANES // HEAD
    lane = lax.broadcasted_iota(jnp.int32, (1, LANES), 1)
    in_head = [(lane >> HEAD_SHIFT) == hh for hh in range(nh)]
    rel = [lane - ((hh + 1) % nh) * HEAD for hh in range(nh)]

    @pl.when(i == 0)
    def _():
        for blk in range(t // tq):
            rows = slice(blk * tq, (blk + 1) * tq)
            kb = k_ref[0, rows, :]
            for hh in range(nh):
                bias = _bias_lanes(c_ref[0, 0, rows, hh:hh + 1] * LOG2E, rel[hh], key_side=True)
                kaug_scr[hh, rows, :] = jnp.where(in_head[hh], kb, bias.astype(BF16))

    row0 = pl.multiple_of(i * tq, tq)
    q = q_ref[0]
    qa = [jnp.where(in_head[hh], q,
                    _bias_lanes(c_ref[0, 0, pl.ds(row0, tq), hh:hh + 1] * LOG2E, rel[hh], key_side=False).astype(BF16))
          for hh in range(nh)]

    def step(off, carry, mask):
        vj = v_ref[0, pl.ds(off, tq), :]
        new = []
        for hh in range(nh):
            m, l, acc = carry[hh]
            s = _dot_nt(qa[hh], kaug_scr[hh, pl.ds(off, tq), :])
            if mask is not None:
                s = jnp.where(mask, s, -jnp.inf)
            m_new = jnp.maximum(m, jnp.max(s, axis=-1, keepdims=True))
            alpha = jnp.exp2(m - m_new)
            p = jnp.exp2(s - m_new)
            l = alpha * l + jnp.sum(p, axis=-1, keepdims=True)
            acc = alpha * acc + _dot(p.astype(BF16), vj)
            new.append((m_new, l, acc))
        return tuple(new)

    init = tuple((jnp.full((tq, 1), -jnp.inf, F32), jnp.zeros((tq, 1), F32), jnp.zeros((tq, LANES), F32))
                 for _ in range(nh))
    carry = lax.fori_loop(0, i, lambda j, c: step(pl.multiple_of(j * tq, tq), c, None), init)
    causal = lax.broadcasted_iota(jnp.int32, (tq, tq), 1) <= lax.broadcasted_iota(jnp.int32, (tq, tq), 0)
    carry = step(row0, carry, causal)
    o_ref[0] = jnp.where(in_head[0], carry[0][2] / carry[0][1], carry[1][2] / carry[1][1])


def _bias_lanes(c, rel, key_side):
    p1 = c.astype(BF16).astype(F32)
    r1 = c - p1
    p2 = r1.astype(BF16).astype(F32)
    p3 = (r1 - p2).astype(BF16).astype(F32)
    lo, sign = (3, -1.0) if key_side else (0, 1.0)
    ones = (rel >= 3 - lo) & (rel < 6 - lo)
    out = jnp.where(ones, 1.0, 0.0)
    for n, p in enumerate((p1, p2, p3)):
        out = jnp.where(rel == lo + n, sign * p, out)
    return out


def _flash(q, k, v, c, tq):
    b, t, d = q.shape
    tq = _row_tile(t, tq)
    g = d // LANES
    return pl.pallas_call(
        functools.partial(_flash_kernel, tq=tq),
        grid=(b, g, t // tq),
        in_specs=[pl.BlockSpec((1, tq, LANES), lambda bi, gi, i: (bi, i, gi)),
                  pl.BlockSpec((1, t, LANES), lambda bi, gi, i: (bi, 0, gi)),
                  pl.BlockSpec((1, t, LANES), lambda bi, gi, i: (bi, 0, gi)),
                  pl.BlockSpec((1, 1, t, 2), lambda bi, gi, i: (bi, gi, 0, 0))],
        out_specs=pl.BlockSpec((1, tq, LANES), lambda bi, gi, i: (bi, i, gi)),
        out_shape=jax.ShapeDtypeStruct((b, t, d), F32),
        scratch_shapes=[pltpu.VMEM((LANES // HEAD, t, LANES), BF16)],
        compiler_params=_params("parallel", "parallel", "arbitrary"),
        name="fox_flash",
    )(q, k, v, c)


def _decode_attn_kernel(q_ref, ck_ref, cv_ref, kn_ref, vn_ref, cq_ref, cc_ref, cn_ref, o_ref):
    lane = lax.broadcasted_iota(jnp.int32, (1, LANES), 1)
    q = q_ref[0]
    t = q.shape[0]
    kc = ck_ref[0].astype(BF16)
    vc = cv_ref[0].astype(BF16)
    kn = kn_ref[0]
    vn = vn_ref[0]
    causal = lax.broadcasted_iota(jnp.int32, (t, t), 1) <= lax.broadcasted_iota(jnp.int32, (t, t), 0)
    outs = []
    for hh in range(LANES // HEAD):
        qh = jnp.where((lane >> HEAD_SHIFT) == hh, q, jnp.zeros((), BF16))
        cq = cq_ref[0, 0, :, hh:hh + 1]
        s1 = _dot_nt(qh, kc) + (cq - cc_ref[0, 0, hh:hh + 1, :]) * LOG2E
        s2 = _dot_nt(qh, kn) + (cq - cn_ref[0, 0, hh:hh + 1, :]) * LOG2E
        s2 = jnp.where(causal, s2, -jnp.inf)
        m = jnp.maximum(jnp.max(s1, axis=-1, keepdims=True), jnp.max(s2, axis=-1, keepdims=True))
        p1 = jnp.exp2(s1 - m)
        p2 = jnp.exp2(s2 - m)
        l = jnp.sum(p1, axis=-1, keepdims=True) + jnp.sum(p2, axis=-1, keepdims=True)
        outs.append((_dot(p1.astype(BF16), vc) + _dot(p2.astype(BF16), vn)) / l)
    o_ref[0] = jnp.where((lane >> HEAD_SHIFT) == 0, outs[0], outs[1])


def _decode_attn(q, cache_k, cache_v, kn, vn, cq, cc, cn):
    b, t, d = q.shape
    p = cache_k.shape[1]
    g = d // LANES
    new = pl.BlockSpec((1, t, LANES), lambda bi, gi: (bi, 0, gi))
    old = pl.BlockSpec((1, p, LANES), lambda bi, gi: (bi, 0, gi))
    return pl.pallas_call(
        _decode_attn_kernel,
        grid=(b, g),
        in_specs=[new, old, old, new, new,
                  pl.BlockSpec((1, 1, t, 2), lambda bi, gi: (bi, gi, 0, 0)),
                  pl.BlockSpec((1, 1, 2, p), lambda bi, gi: (bi, gi, 0, 0)),
                  pl.BlockSpec((1, 1, 2, t), lambda bi, gi: (bi, gi, 0, 0))],
        out_specs=new,
        out_shape=jax.ShapeDtypeStruct((b, t, d), F32),
        compiler_params=_params("parallel", "parallel"),
        name="fox_decode_attn",
    )(q, cache_k, cache_v, kn, vn, cq, cc, cn)


def _pad_cols(w, mult):
    pad = (-w.shape[-1]) % mult
    return jnp.pad(w, ((0, 0), (0, pad))) if pad else w


def _pad_rows(w, mult):
    pad = (-w.shape[0]) % mult
    return jnp.pad(w, ((0, pad), (0, 0))) if pad else w


def _pair_layout(c_col, c_row, heads):
    b, l, _ = c_col.shape
    g = heads // 2
    cq = c_col[:, :, :heads].reshape(b, l, g, 2).transpose(0, 2, 1, 3)
    ck = c_row[:, :heads, :].reshape(b, g, 2, l)
    return cq, ck


def _rwkv_layer(x, shift0, wkv0, p, tm):
    b, t, d = x.shape
    h = d // HEAD
    xp = jnp.concatenate([shift0[:, None, :], x[:, :-1]], axis=1)
    r, lw, k, v, a, g = _rwkv_proj(x.reshape(b * t, d), xp.reshape(b * t, d), *p["proj"], tm=tm)
    tpad = (-t) % CHUNK

    def seq(z):
        z = z.reshape(b, t, d)
        return jnp.pad(z, ((0, 0), (0, tpad), (0, 0))) if tpad else z

    ht = jnp.swapaxes(wkv0.astype(F32), -1, -2).reshape(b, d // SLAB, HEADS_PER_SLAB, HEAD, HEAD)
    eye = jnp.eye(HEADS_PER_SLAB, dtype=F32)
    h0 = jnp.einsum("bshkv,hg->bshkgv", ht, eye).reshape(b, d // SLAB, SLAB, SLAB)
    o, hfin = _wkv(seq(r), seq(lw), seq(k), seq(v), seq(a), *p["wkv"], h0)
    hfin = hfin.reshape(b, d // SLAB, HEADS_PER_SLAB, HEAD, HEADS_PER_SLAB, HEAD)
    hfin = jnp.einsum("bshkgv,hg->bshkv", hfin, eye).reshape(b, h, HEAD, HEAD)
    s_fin = jnp.swapaxes(hfin, -1, -2).astype(wkv0.dtype)
    return o[:, :t].reshape(b * t, d), g, x[:, -1], s_fin


def kernel(x_prompt, x_sample, state_wkv, state_shift, cache_k, cache_v, cache_logf, rwkv_mu, rwkv_w0, rwkv_w1, rwkv_w2, rwkv_a0, rwkv_a1, rwkv_a2, rwkv_g1, rwkv_g2, rwkv_k_k, rwkv_k_a, rwkv_r_k, rwkv_w_r, rwkv_w_k, rwkv_w_v, rwkv_w_o, rwkv_lnx_g, rwkv_lnx_b, fox_w_in, fox_b_f, fox_w_o, ffn_w1, ffn_w2, ln_mix_g, ln_mix_b, ln_ffn_g, ln_ffn_b):
    depth = ln_mix_g.shape[0]
    alpha = (2 * depth) ** 0.25
    bp, tp, d = x_prompt.shape
    bs, ts, _ = x_sample.shape
    heads = d // HEAD
    past = cache_k.shape[2]
    tm = 256

    xp = x_prompt
    xs = x_sample
    outs_p = {n: [] for n in ("wkv", "shift", "k", "v", "lf")}
    outs_s = {n: [] for n in ("wkv", "shift", "k", "v", "lf")}
    for i in range(depth):
        j = i // 2
        ln = jnp.stack([ln_mix_g[i], ln_mix_b[i], ln_ffn_g[i], ln_ffn_b[i]])
        w1 = ffn_w1[i].astype(BF16)
        w2 = ffn_w2[i].astype(BF16)
        if i % 2 == 0:
            row = lambda z: z.reshape(1, d)
            params = {
                "proj": (rwkv_mu[j], row(rwkv_w0[j]), row(rwkv_a0[j]),
                         rwkv_w_r[j].astype(BF16), rwkv_w_k[j].astype(BF16), rwkv_w_v[j].astype(BF16),
                         _pad_cols(rwkv_w1[j], LANES).astype(BF16), _pad_rows(rwkv_w2[j], LANES).astype(BF16),
                         _pad_cols(rwkv_a1[j], LANES).astype(BF16), _pad_rows(rwkv_a2[j], LANES).astype(BF16),
                         _pad_cols(rwkv_g1[j], LANES).astype(BF16), _pad_rows(rwkv_g2[j], LANES).astype(BF16)),
                "wkv": (row(rwkv_k_k[j]), row(rwkv_k_a[j]), row(rwkv_r_k[j]), row(rwkv_lnx_g[j]), row(rwkv_lnx_b[j])),
            }
            wo = rwkv_w_o[j].astype(BF16)
            zero_shift = jnp.zeros((bp, d), xp.dtype)
            zero_wkv = jnp.zeros((bp, heads, HEAD, HEAD), state_wkv.dtype)
            for x, shift0, wkv0, outs in ((xp, zero_shift, zero_wkv, outs_p), (xs, state_shift[j], state_wkv[j], outs_s)):
                b, t, _ = x.shape
                o, g, sh, s_fin = _rwkv_layer(x, shift0, wkv0, params, tm)
                x_new = _post(o, g, x.reshape(b * t, d), wo, w1, w2, ln, alpha, tm).reshape(b, t, d)
                outs["wkv"].append(s_fin)
                outs["shift"].append(sh)
                if x is xp:
                    xp = x_new
                else:
                    xs = x_new
        else:
            w_in = fox_w_in[j]
            wq = w_in[:, :d].astype(BF16)
            wk = w_in[:, d:2 * d].astype(BF16)
            wv = w_in[:, 2 * d:3 * d].astype(BF16)
            wf = _pad_cols(w_in[:, 3 * d:], LANES).astype(BF16)
            bf = _pad_cols(fox_b_f[j].reshape(1, heads), LANES)
            wo = fox_w_o[j].astype(BF16)

            q, k, v, kb, vb, lf = _fox_proj(xp.reshape(bp * tp, d), wq, wk, wv, wf, bf, tm)
            c_col, c_row = _cumsum(lf.reshape(bp, tp, LANES))
            cq, ck = _pair_layout(c_col, c_row, heads)
            sh3 = lambda z: z.reshape(bp, tp, d)
            o = _flash(sh3(q), sh3(kb), sh3(vb), cq, tq=512)
            outs_p["k"].append(k.reshape(bp, tp, heads, HEAD))
            outs_p["v"].append(v.reshape(bp, tp, heads, HEAD))
            outs_p["lf"].append(lf.reshape(bp, tp, LANES)[:, :, :heads])
            xp = _post(o.reshape(bp * tp, d), None, xp.reshape(bp * tp, d), wo, w1, w2, ln, alpha, tm).reshape(bp, tp, d)

            q, k, v, kb, vb, lf = _fox_proj(xs.reshape(bs * ts, d), wq, wk, wv, wf, bf, tm)
            lf_new = lf.reshape(bs, ts, LANES)
            lf_all = jnp.concatenate([_pad_cols(cache_logf[j].astype(F32).reshape(bs * past, heads), LANES)
                                      .reshape(bs, past, LANES), lf_new], axis=1)
            lpad = (-(past + ts)) % CS_TILE
            c_col, c_row = _cumsum(jnp.pad(lf_all, ((0, 0), (0, lpad), (0, 0))))
            cq, ck = _pair_layout(c_col, c_row, heads)
            sh3 = lambda z: z.reshape(bs, ts, d)
            o = _decode_attn(sh3(q), cache_k[j].reshape(bs, past, d), cache_v[j].reshape(bs, past, d),
                             sh3(kb), sh3(vb), cq[:, :, past:past + ts], ck[:, :, :, :past], ck[:, :, :, past:past + ts])
            outs_s["k"].append(k.reshape(bs, ts, heads, HEAD))
            outs_s["v"].append(v.reshape(bs, ts, heads, HEAD))
            outs_s["lf"].append(lf_new[:, :, :heads])
            xs = _post(o.reshape(bs * ts, d), None, xs.reshape(bs * ts, d), wo, w1, w2, ln, alpha, tm).reshape(bs, ts, d)

    st = jnp.stack
    return (xp, xs,
            st(outs_p["wkv"]), st(outs_p["shift"]), st(outs_p["k"]), st(outs_p["v"]), st(outs_p["lf"]),
            st(outs_s["wkv"]), st(outs_s["shift"]), st(outs_s["k"]), st(outs_s["v"]), st(outs_s["lf"]))
```

```python
import functools
import math

import jax
import jax.numpy as jnp
from jax import lax
from jax.experimental import pallas as pl
from jax.experimental.pallas import tpu as pltpu

BF16 = jnp.bfloat16
F32 = jnp.float32

HEAD = 64
HEAD_SHIFT = 6
LANES = 128
SUBLANES = 8
SLAB = 256
HEADS_PER_SLAB = SLAB // HEAD
CHUNK = 64
LN_EPS = 1e-5
GN_EPS = 64e-5
EXP_M05 = math.exp(-0.5)
LOG2E = math.log2(math.e)
VMEM_LIMIT = 56 * 1024 * 1024


def _dot(a, b):
    return jnp.dot(a, b, preferred_element_type=F32)


def _dot_nt(a, b):
    return lax.dot_general(a, b, (((1,), (1,)), ((), ())), preferred_element_type=F32)


def _dot_tn(a, b):
    return lax.dot_general(a, b, (((0,), (0,)), ((), ())), preferred_element_type=F32)


def _split2(x):
    hi = x.astype(BF16)
    lo = (x - hi.astype(F32)).astype(BF16)
    return hi, lo


def _split3(x):
    p1 = x.astype(BF16)
    r1 = x - p1.astype(F32)
    p2 = r1.astype(BF16)
    p3 = (r1 - p2.astype(F32)).astype(BF16)
    return p1, p2, p3


def _dot3(a, b, dot=_dot):
    p1, p2, p3 = _split3(a)
    return dot(p1, b) + dot(p2, b) + dot(p3, b)


def _tri(n, upper):
    r = lax.broadcasted_iota(jnp.int32, (n, n), 0)
    c = lax.broadcasted_iota(jnp.int32, (n, n), 1)
    return jnp.where((r <= c) if upper else (c <= r), 1.0, 0.0).astype(BF16)


def _layer_norm(z, g, b):
    mu = jnp.mean(z, axis=-1, keepdims=True)
    zc = z - mu
    var = jnp.mean(zc * zc, axis=-1, keepdims=True)
    return zc * lax.rsqrt(var + LN_EPS) * g + b


def _params(*sem):
    return pltpu.CompilerParams(dimension_semantics=sem, vmem_limit_bytes=VMEM_LIMIT)


def _const_spec(shape):
    nd = len(shape)
    return pl.BlockSpec(shape, lambda *_: (0,) * nd, pipeline_mode=pl.Buffered(1))


def _row_tile(n, want):
    t = min(n, want)
    assert n % t == 0 and t % SUBLANES == 0, (n, t)
    return t


def _rwkv_proj_kernel(x_ref, prev_ref, first_ref, mu_ref, w0_ref, a0_ref, wr_ref, wk_ref, wv_ref,
                      w1_ref, w2_ref, a1_ref, a2_ref, g1_ref, g2_ref,
                      r_ref, lw_ref, k_ref, v_ref, a_ref, g_ref):
    x = x_ref[0]
    row0 = jnp.where(pl.program_id(1) == 0, first_ref[0], prev_ref[0, SUBLANES - 1:SUBLANES, :])
    is_row0 = lax.broadcasted_iota(jnp.int32, x.shape, 0) == 0
    xx = jnp.where(is_row0, row0, pltpu.roll(x, 1, 0)) - x

    def mix(i):
        return (x + xx * mu_ref[i:i + 1, :]).astype(BF16)

    r_ref[0] = _dot(mix(0), wr_ref[...])
    w_raw = w0_ref[...] + _dot(jnp.tanh(_dot(mix(1), w1_ref[...])).astype(BF16), w2_ref[...])
    lw_ref[0] = -EXP_M05 * jax.nn.sigmoid(w_raw)
    k_ref[0] = _dot(mix(2), wk_ref[...])
    v_ref[0] = _dot(mix(3), wv_ref[...])
    a_ref[0] = jax.nn.sigmoid(a0_ref[...] + _dot(_dot(mix(4), a1_ref[...]).astype(BF16), a2_ref[...]))
    g_ref[0] = _dot(jax.nn.sigmoid(_dot(mix(5), g1_ref[...])).astype(BF16), g2_ref[...])


def _rwkv_proj(x, shift0, mu, w0, a0, wr, wk, wv, w1, w2, a1, a2, g1, g2, tm):
    b, t, d = x.shape
    tm = _row_tile(t, tm)
    per_tile = tm // SUBLANES
    row = pl.BlockSpec((1, tm, d), lambda bi, i: (bi, i, 0))
    prev = pl.BlockSpec((1, SUBLANES, d), lambda bi, i: (bi, jnp.maximum(i * per_tile - 1, 0), 0))
    first = pl.BlockSpec((1, 1, d), lambda bi, i: (bi, 0, 0))
    consts = [mu, w0, a0, wr, wk, wv, w1, w2, a1, a2, g1, g2]
    return pl.pallas_call(
        _rwkv_proj_kernel,
        grid=(b, t // tm),
        in_specs=[row, prev, first] + [_const_spec(c.shape) for c in consts],
        out_specs=[row] * 6,
        out_shape=[jax.ShapeDtypeStruct((b, t, d), F32)] * 6,
        compiler_params=_params("parallel", "parallel"),
        name="rwkv_proj",
    )(x, x, shift0, *consts)


def _wkv_kernel(r_ref, lw_ref, k_ref, v_ref, a_ref, kk_ref, ka_ref, rk_ref, lg_ref, lb_ref, h0_ref,
                o_ref, hout_ref, h_scr, *, last_chunk):
    c = pl.program_id(1)
    nslab = h_scr.shape[0]
    C = CHUNK
    rows = r_ref.shape[1]

    @pl.when(c == 0)
    def _():
        h_scr[...] = jnp.zeros_like(h_scr)
        for s in range(nslab):
            for h in range(HEADS_PER_SLAB):
                hs = slice(h * HEAD, (h + 1) * HEAD)
                h_scr[s, hs, hs] = h0_ref[0, s * HEADS_PER_SLAB + h]

    row_s = lax.broadcasted_iota(jnp.int32, (SLAB, SLAB), 0)
    col_s = lax.broadcasted_iota(jnp.int32, (SLAB, SLAB), 1)
    mask_bd = (row_s >> HEAD_SHIFT) == (col_s >> HEAD_SHIFT)
    eye_s = row_s == col_s
    row_c = lax.broadcasted_iota(jnp.int32, (C, SLAB), 0)
    colin_c = lax.broadcasted_iota(jnp.int32, (C, SLAB), 1) & (C - 1)
    strict_lo = colin_c < row_c
    incl_lo = colin_c <= row_c
    eye_c = colin_c == row_c
    ones_bd = jnp.where(mask_bd, 1.0, 0.0).astype(BF16)

    def bd(x):
        xb = x.astype(BF16)
        return jnp.where(mask_bd, jnp.concatenate([xb] * HEADS_PER_SLAB, axis=0), jnp.zeros((), BF16))

    def head_sum(x):
        hi, lo = _split2(x)
        return _dot(hi, ones_bd) + _dot(lo, ones_bd)

    def chunk_rows(ref, sl):
        x = ref[0, :, sl]
        if rows < C:
            x = jnp.concatenate([x, jnp.zeros((C - rows, x.shape[1]), x.dtype)], axis=0)
        return x

    lw_all = chunk_rows(lw_ref, slice(None))
    L_all = _dot3(lw_all, _tri(C, upper=False), dot=lambda p, t: _dot(t, p))

    slabs = [slice(s * SLAB, (s + 1) * SLAB) for s in range(nslab)]
    S = range(nslab)
    r = [chunk_rows(r_ref, sl) for sl in slabs]
    v = [chunk_rows(v_ref, sl) for sl in slabs]
    asig = [chunk_rows(a_ref, sl) for sl in slabs]
    k0 = [chunk_rows(k_ref, sl) for sl in slabs]
    kk = [k0[s] * kk_ref[:, slabs[s]] for s in S]
    nrm = [jnp.sqrt(head_sum(kk[s] * kk[s])) for s in S]
    kk = [kk[s] / jnp.maximum(nrm[s], 1e-12) for s in S]
    k = [k0[s] * (1.0 + (asig[s] - 1.0) * ka_ref[:, slabs[s]]) for s in S]
    bv = [kk[s] * asig[s] for s in S]
    L = [L_all[:, sl] for sl in slabs]
    LC = [L[s][C - 1:C, :] for s in S]
    enL = [jnp.exp(-L[s]) for s in S]
    eLc = [jnp.exp(LC[s] - L[s]) for s in S]
    At = [-kk[s] * jnp.exp(L[s] - lw_all[:, slabs[s]]) for s in S]
    Rt = [r[s] * jnp.exp(L[s]) for s in S]
    Bt = [bv[s] * enL[s] for s in S]
    Kt = [k[s] * enL[s] for s in S]
    Bh = [bv[s] * eLc[s] for s in S]
    Kh = [k[s] * eLc[s] for s in S]

    lhs = [jnp.concatenate([At[s], Rt[s]], axis=0).astype(BF16) for s in S]
    ab = [_dot_nt(lhs[s], bd(Bt[s])) for s in S]
    ak = [_dot_nt(lhs[s], bd(Kt[s])) for s in S]
    N = [jnp.where(strict_lo, ab[s][:C], 0.0) for s in S]
    M = [jnp.where(strict_lo, ak[s][:C], 0.0) for s in S]
    Arb = [jnp.where(incl_lo, ab[s][C:], 0.0) for s in S]
    Ark = [jnp.where(incl_lo, ak[s][C:], 0.0) for s in S]
    MV = [_dot(M[s].astype(BF16), bd(v[s])) for s in S]

    X = [jnp.where(eye_c, 1.0, 0.0) + N[s] for s in S]
    Pw = N
    for _ in range(int(math.log2(C)) - 1):
        Pw = [_dot(Pw[s].astype(BF16), bd(Pw[s])) for s in S]
        X = [X[s] + _dot(X[s].astype(BF16), bd(Pw[s])) for s in S]

    wu = [_dot(X[s].astype(BF16), jnp.concatenate([bd(At[s]), bd(MV[s])], axis=1)) for s in S]
    WA = [wu[s][:, :SLAB] for s in S]
    UV = [wu[s][:, SLAB:] for s in S]
    Yl = [_dot(jnp.concatenate([Arb[s], Ark[s]], axis=1).astype(BF16),
               jnp.concatenate([bd(UV[s]), bd(v[s])], axis=0)) for s in S]
    Rp = [Rt[s] + _dot(Arb[s].astype(BF16), bd(WA[s])) for s in S]
    G = [jnp.where(mask_bd, _dot_tn(Bh[s].astype(BF16), WA[s].astype(BF16)), 0.0)
         + jnp.where(eye_s, jnp.exp(LC[s]), 0.0) for s in S]
    J = [jnp.where(mask_bd,
                   _dot_tn(jnp.concatenate([Bh[s], Kh[s]], axis=0).astype(BF16),
                           jnp.concatenate([UV[s], v[s]], axis=0).astype(BF16)), 0.0) for s in S]
    Hb = [h_scr[s].astype(BF16) for s in S]
    y = [_dot(Rp[s].astype(BF16), Hb[s]) + Yl[s] for s in S]
    for s in S:
        h_scr[s] = _dot(G[s].astype(BF16), Hb[s]) + J[s]

    mu = [head_sum(y[s]) * (1.0 / HEAD) for s in S]
    yc = [y[s] - mu[s] for s in S]
    var = [head_sum(yc[s] * yc[s]) * (1.0 / HEAD) for s in S]
    bonus = [head_sum(r[s] * k[s] * rk_ref[:, slabs[s]]) * v[s] for s in S]
    for s in S:
        sl = slabs[s]
        out = yc[s] * lax.rsqrt(var[s] + GN_EPS) * lg_ref[:, sl] + lb_ref[:, sl] + bonus[s]
        o_ref[0, :, sl] = out[:rows]

    @pl.when(c == last_chunk)
    def _():
        for s in range(nslab):
            for h in range(HEADS_PER_SLAB):
                hs = slice(h * HEAD, (h + 1) * HEAD)
                hout_ref[0, s * HEADS_PER_SLAB + h] = h_scr[s, hs, hs]


def _wkv(r, lw, k, v, a, k_k, k_a, r_k, lnx_g, lnx_b, h0):
    b, t, d = r.shape
    rows = min(t, CHUNK)
    assert t % rows == 0 and rows % SUBLANES == 0
    nslab = d // SLAB
    heads = d // HEAD
    seq = pl.BlockSpec((1, rows, d), lambda i, c: (i, c, 0))
    vec = pl.BlockSpec((1, d), lambda i, c: (0, 0))
    hspec = pl.BlockSpec((1, heads, HEAD, HEAD), lambda i, c: (i, 0, 0, 0))
    return pl.pallas_call(
        functools.partial(_wkv_kernel, last_chunk=t // rows - 1),
        grid=(b, t // rows),
        in_specs=[seq] * 5 + [vec] * 5 + [hspec],
        out_specs=[seq, hspec],
        out_shape=[jax.ShapeDtypeStruct((b, t, d), F32),
                   jax.ShapeDtypeStruct((b, heads, HEAD, HEAD), F32)],
        scratch_shapes=[pltpu.VMEM((nslab, SLAB, SLAB), F32)],
        compiler_params=_params("parallel", "arbitrary"),
        name="wkv_chunked",
    )(r, lw, k, v, a, k_k, k_a, r_k, lnx_g, lnx_b, h0)


def _post_kernel(*refs, alpha, gated, ff_chunk):
    if gated:
        pre_ref, gate_ref, x_ref, wo_ref, w1_ref, w2_ref, ln_ref, out_ref = refs
        pre = pre_ref[...] * gate_ref[...]
    else:
        pre_ref, x_ref, wo_ref, w1_ref, w2_ref, ln_ref, out_ref = refs
        pre = pre_ref[...]
    h = _dot(pre.astype(BF16), wo_ref[...])
    x1 = _layer_norm(alpha * x_ref[...] + h, ln_ref[0:1, :], ln_ref[1:2, :])
    x1b = x1.astype(BF16)
    m = jnp.zeros_like(x1)
    for f in range(w1_ref.shape[1] // ff_chunk):
        fs = slice(f * ff_chunk, (f + 1) * ff_chunk)
        hid = jnp.maximum(_dot(x1b, w1_ref[:, fs]), 0.0)
        m = m + _dot((hid * hid).astype(BF16), w2_ref[fs, :])
    out_ref[...] = _layer_norm(alpha * x1 + m, ln_ref[2:3, :], ln_ref[3:4, :])


def _post(pre, gate, x, wo, w1, w2, ln, alpha, tm):
    n, d = x.shape
    tm = _row_tile(n, tm)
    row = pl.BlockSpec((tm, d), lambda i: (i, 0))
    acts = [pre, x] if gate is None else [pre, gate, x]
    consts = [wo, w1, w2, ln]
    return pl.pallas_call(
        functools.partial(_post_kernel, alpha=alpha, gated=gate is not None, ff_chunk=min(1024, w1.shape[1])),
        grid=(n // tm,),
        in_specs=[row] * len(acts) + [_const_spec(c.shape) for c in consts],
        out_specs=row,
        out_shape=jax.ShapeDtypeStruct((n, d), F32),
        compiler_params=_params("parallel"),
        name="post_mlp",
    )(*acts, *consts)


def _fox_proj_kernel(x_ref, wq_ref, wk_ref, wv_ref, wf_ref, bf_ref,
                     q_ref, k_ref, v_ref, kb_ref, vb_ref, lf_ref, *, channel_major):
    xb = x_ref[0].astype(BF16)
    q_ref[0] = (_dot(xb, wq_ref[...]) * (HEAD ** -0.5 * LOG2E)).astype(BF16)
    k = _dot(xb, wk_ref[...])
    v = _dot(xb, wv_ref[...])
    k_ref[0] = k.T if channel_major else k
    v_ref[0] = v.T if channel_major else v
    kb_ref[0] = k.astype(BF16)
    vb_ref[0] = v.astype(BF16)
    f = _dot(xb, wf_ref[...]) + bf_ref[...]
    lf_ref[0] = jnp.minimum(f, 0.0) - jnp.log1p(jnp.exp(-jnp.abs(f)))


def _fox_proj(x, wq, wk, wv, wf, bf, tm, channel_major):
    b, t, d = x.shape
    tm = _row_tile(t, tm)
    row = pl.BlockSpec((1, tm, d), lambda bi, i: (bi, i, 0))
    rowf = pl.BlockSpec((1, tm, LANES), lambda bi, i: (bi, i, 0))
    kv_spec = pl.BlockSpec((1, d, tm), lambda bi, i: (bi, 0, i)) if channel_major else row
    kv_shape = jax.ShapeDtypeStruct((b, d, t) if channel_major else (b, t, d), F32)
    act_bf = jax.ShapeDtypeStruct((b, t, d), BF16)
    consts = [wq, wk, wv, wf, bf]
    return pl.pallas_call(
        functools.partial(_fox_proj_kernel, channel_major=channel_major),
        grid=(b, t // tm),
        in_specs=[row] + [_const_spec(c.shape) for c in consts],
        out_specs=[row, kv_spec, kv_spec, row, row, rowf],
        out_shape=[act_bf, kv_shape, kv_shape, act_bf, act_bf, jax.ShapeDtypeStruct((b, t, LANES), F32)],
        compiler_params=_params("parallel", "parallel"),
        name="fox_proj",
    )(x, *consts)


CS_TILE = 512


def _cumsum_kernel(lf_ref, c_ref, carry_ref):
    @pl.when(pl.program_id(1) == 0)
    def _():
        carry_ref[...] = jnp.zeros_like(carry_ref)

    n = lf_ref.shape[1]
    cs = _dot3(lf_ref[0], _tri(n, upper=False), dot=lambda p, t: _dot(t, p)) + carry_ref[0:1, :]
    c_ref[0] = cs
    carry_ref[...] = jnp.broadcast_to(cs[n - 1:n, :], carry_ref.shape)


def _cumsum(lf):
    b, l, w = lf.shape
    tile = _row_tile(l, CS_TILE)
    spec = pl.BlockSpec((1, tile, w), lambda i, j: (i, j, 0))
    return pl.pallas_call(
        _cumsum_kernel,
        grid=(b, l // tile),
        in_specs=[spec],
        out_specs=spec,
        out_shape=jax.ShapeDtypeStruct((b, l, w), F32),
        scratch_shapes=[pltpu.VMEM((SUBLANES, w), F32)],
        compiler_params=_params("parallel", "arbitrary"),
        name="logf_cumsum",
    )(lf)


def _bias_lanes(c, head0, key_side):
    nh = LANES // HEAD
    src = lax.broadcasted_iota(jnp.int32, (LANES, LANES), 0)
    dst = lax.broadcasted_iota(jnp.int32, (LANES, LANES), 1)
    lane = lax.broadcasted_iota(jnp.int32, (1, LANES), 1)
    out = jnp.zeros(c.shape, F32)
    pieces = _split3(c * LOG2E)
    lo, sign = (3, -1.0) if key_side else (0, 1.0)
    for hh in range(nh):
        base = ((hh + 1) % nh) * HEAD
        for n, p in enumerate(pieces):
            place = jnp.where((src == head0 + hh) & (dst == base + lo + n), sign, 0.0).astype(BF16)
            out = out + _dot(p, place)
        rel = lane - base
        out = out + jnp.where((rel >= 3 - lo) & (rel < 6 - lo), 1.0, 0.0)
    return out


def _flash_kernel(q_ref, k_ref, v_ref, c_ref, o_ref, kaug_scr, *, tq):
    g = pl.program_id(1)
    i = pl.program_id(2)
    t = k_ref.shape[1]
    nh = LANES // HEAD
    lane = lax.broadcasted_iota(jnp.int32, (1, LANES), 1)
    in_head = [(lane >> HEAD_SHIFT) == hh for hh in range(nh)]
    bias_lanes = [(lane >= ((hh + 1) % nh) * HEAD) & (lane < ((hh + 1) % nh) * HEAD + 6) for hh in range(nh)]

    def augment(x, bias):
        bias = bias.astype(BF16)
        return [jnp.where(in_head[hh], x, jnp.where(bias_lanes[hh], bias, jnp.zeros((), BF16)))
                for hh in range(nh)]

    @pl.when(i == 0)
    def _():
        for blk in range(t // tq):
            rows = slice(blk * tq, (blk + 1) * tq)
            ka = augment(k_ref[0, rows, :], _bias_lanes(c_ref[0, rows, :], nh * g, key_side=True))
            for hh in range(nh):
                kaug_scr[hh, rows, :] = ka[hh]

    row0 = pl.multiple_of(i * tq, tq)
    qa = augment(q_ref[0], _bias_lanes(c_ref[0, pl.ds(row0, tq), :], nh * g, key_side=False))

    def step(off, carry, mask):
        vj = v_ref[0, pl.ds(off, tq), :]
        new = []
        for hh in range(nh):
            m, l, acc = carry[hh]
            s = _dot_nt(qa[hh], kaug_scr[hh, pl.ds(off, tq), :])
            if mask is not None:
                s = jnp.where(mask, s, -jnp.inf)
            m_new = jnp.maximum(m, jnp.max(s, axis=-1, keepdims=True))
            alpha = jnp.exp2(m - m_new)
            p = jnp.exp2(s - m_new)
            l = alpha * l + jnp.sum(p, axis=-1, keepdims=True)
            acc = alpha * acc + _dot(p.astype(BF16), vj)
            new.append((m_new, l, acc))
        return tuple(new)

    init = tuple((jnp.full((tq, 1), -jnp.inf, F32), jnp.zeros((tq, 1), F32), jnp.zeros((tq, LANES), F32))
                 for _ in range(nh))
    carry = lax.fori_loop(0, i, lambda j, c: step(pl.multiple_of(j * tq, tq), c, None), init)
    causal = lax.broadcasted_iota(jnp.int32, (tq, tq), 1) <= lax.broadcasted_iota(jnp.int32, (tq, tq), 0)
    carry = step(row0, carry, causal)
    o_ref[0] = jnp.where(in_head[0], carry[0][2] / carry[0][1], carry[1][2] / carry[1][1])


def _flash(q, k, v, c, tq):
    b, t, d = q.shape
    tq = _row_tile(t, tq)
    g = d // LANES
    return pl.pallas_call(
        functools.partial(_flash_kernel, tq=tq),
        grid=(b, g, t // tq),
        in_specs=[pl.BlockSpec((1, tq, LANES), lambda bi, gi, i: (bi, i, gi)),
                  pl.BlockSpec((1, t, LANES), lambda bi, gi, i: (bi, 0, gi)),
                  pl.BlockSpec((1, t, LANES), lambda bi, gi, i: (bi, 0, gi)),
                  pl.BlockSpec((1, t, LANES), lambda bi, gi, i: (bi, 0, 0))],
        out_specs=pl.BlockSpec((1, tq, LANES), lambda bi, gi, i: (bi, i, gi)),
        out_shape=jax.ShapeDtypeStruct((b, t, d), F32),
        scratch_shapes=[pltpu.VMEM((LANES // HEAD, t, LANES), BF16)],
        compiler_params=_params("parallel", "parallel", "arbitrary"),
        name="fox_flash",
    )(q, k, v, c)


DEC_TILE = 512


def _decode_attn_kernel(q_ref, ckt_ref, cvt_ref, kn_ref, vn_ref, clf_ref, lfn_ref, o_ref, c_scr):
    g = pl.program_id(1)
    nh = LANES // HEAD
    past = ckt_ref.shape[2]
    t = q_ref.shape[1]

    @pl.when(g == 0)
    def _():
        tri = _tri(DEC_TILE, upper=True)
        carry = jnp.zeros((clf_ref.shape[1], 1), F32)
        for blk in range(past // DEC_TILE):
            cols = slice(blk * DEC_TILE, (blk + 1) * DEC_TILE)
            cs = _dot3(clf_ref[0, :, cols], tri) + carry
            c_scr[:, cols] = cs
            carry = cs[:, DEC_TILE - 1:DEC_TILE]

    lane = lax.broadcasted_iota(jnp.int32, (1, LANES), 1)
    q = q_ref[0]
    kct = ckt_ref[0].astype(BF16)
    vct = cvt_ref[0].astype(BF16)
    kn = kn_ref[0]
    vn = vn_ref[0]
    causal = lax.broadcasted_iota(jnp.int32, (t, t), 1) <= lax.broadcasted_iota(jnp.int32, (t, t), 0)
    lf_new = _dot3(lfn_ref[0], _tri(t, upper=False), dot=lambda p, tr: _dot(tr, p))
    outs = []
    for hh in range(nh):
        head = nh * g + hh
        qh = jnp.where((lane >> HEAD_SHIFT) == hh, q, jnp.zeros((), BF16))
        c_cache = c_scr[pl.ds(head, 1), :]
        total = c_cache[:, past - 1:past]
        onehot = jnp.where(lane == head, 1.0, 0.0)
        cn_col = jnp.sum(lf_new * onehot, axis=-1, keepdims=True) + total
        sel = jnp.broadcast_to(onehot, (SUBLANES, LANES)).astype(BF16)
        cn_row = _dot3(lf_new, sel, dot=lambda p, e: _dot_nt(e, p))[0:1] + total
        s1 = _dot(qh, kct) + (cn_col - c_cache) * LOG2E
        s2 = _dot_nt(qh, kn) + (cn_col - cn_row) * LOG2E
        s2 = jnp.where(causal, s2, -jnp.inf)
        m = jnp.maximum(jnp.max(s1, axis=-1, keepdims=True), jnp.max(s2, axis=-1, keepdims=True))
        e1 = jnp.exp2(s1 - m)
        e2 = jnp.exp2(s2 - m)
        l = jnp.sum(e1, axis=-1, keepdims=True) + jnp.sum(e2, axis=-1, keepdims=True)
        outs.append((_dot_nt(e1.astype(BF16), vct) + _dot(e2.astype(BF16), vn)) / l)
    o_ref[0] = jnp.where((lane >> HEAD_SHIFT) == 0, outs[0], outs[1])


def _decode_attn(q, cache_kt, cache_vt, kn, vn, cache_lf, lf_new):
    b, t, d = q.shape
    p = cache_kt.shape[2]
    h = cache_lf.shape[1]
    assert p % DEC_TILE == 0
    g = d // LANES
    new = pl.BlockSpec((1, t, LANES), lambda bi, gi: (bi, 0, gi))
    old = pl.BlockSpec((1, LANES, p), lambda bi, gi: (bi, gi, 0))
    return pl.pallas_call(
        _decode_attn_kernel,
        grid=(b, g),
        in_specs=[new, old, old, new, new,
                  pl.BlockSpec((1, h, p), lambda bi, gi: (bi, 0, 0)),
                  pl.BlockSpec((1, t, LANES), lambda bi, gi: (bi, 0, 0))],
        out_specs=new,
        out_shape=jax.ShapeDtypeStruct((b, t, d), F32),
        scratch_shapes=[pltpu.VMEM((h, p), F32)],
        compiler_params=_params("parallel", "arbitrary"),
        name="fox_decode_attn",
    )(q, cache_kt, cache_vt, kn, vn, cache_lf, lf_new)


def _pad_cols(w, mult):
    pad = (-w.shape[-1]) % mult
    return jnp.pad(w, ((0, 0), (0, pad))) if pad else w


def _pad_rows(w, mult):
    pad = (-w.shape[0]) % mult
    return jnp.pad(w, ((0, pad), (0, 0))) if pad else w


def kernel(x_prompt, x_sample, state_wkv, state_shift, cache_k, cache_v, cache_logf, rwkv_mu, rwkv_w0, rwkv_w1, rwkv_w2, rwkv_a0, rwkv_a1, rwkv_a2, rwkv_g1, rwkv_g2, rwkv_k_k, rwkv_k_a, rwkv_r_k, rwkv_w_r, rwkv_w_k, rwkv_w_v, rwkv_w_o, rwkv_lnx_g, rwkv_lnx_b, fox_w_in, fox_b_f, fox_w_o, ffn_w1, ffn_w2, ln_mix_g, ln_mix_b, ln_ffn_g, ln_ffn_b):
    depth = ln_mix_g.shape[0]
    alpha = (2 * depth) ** 0.25
    bp, tp, d = x_prompt.shape
    bs, ts, _ = x_sample.shape
    heads = d // HEAD
    past = cache_k.shape[2]
    tm = 256

    xs_by_group = {"p": x_prompt, "s": x_sample}
    outs = {grp: {n: [] for n in ("wkv", "shift", "k", "v", "lf")} for grp in ("p", "s")}
    for i in range(depth):
        j = i // 2
        ln = jnp.stack([ln_mix_g[i], ln_mix_b[i], ln_ffn_g[i], ln_ffn_b[i]])
        w1 = ffn_w1[i].astype(BF16)
        w2 = ffn_w2[i].astype(BF16)
        if i % 2 == 0:
            row = lambda z: z.reshape(1, d)
            proj = (rwkv_mu[j], row(rwkv_w0[j]), row(rwkv_a0[j]),
                    rwkv_w_r[j].astype(BF16), rwkv_w_k[j].astype(BF16), rwkv_w_v[j].astype(BF16),
                    _pad_cols(rwkv_w1[j], LANES).astype(BF16), _pad_rows(rwkv_w2[j], LANES).astype(BF16),
                    _pad_cols(rwkv_a1[j], LANES).astype(BF16), _pad_rows(rwkv_a2[j], LANES).astype(BF16),
                    _pad_cols(rwkv_g1[j], LANES).astype(BF16), _pad_rows(rwkv_g2[j], LANES).astype(BF16))
            wkv_vecs = (row(rwkv_k_k[j]), row(rwkv_k_a[j]), row(rwkv_r_k[j]), row(rwkv_lnx_g[j]), row(rwkv_lnx_b[j]))
            wo = rwkv_w_o[j].astype(BF16)
            starts = {"p": (jnp.zeros((bp, d), x_prompt.dtype), jnp.zeros((bp, heads, HEAD, HEAD), state_wkv.dtype)),
                      "s": (state_shift[j], state_wkv[j])}
            for grp in ("p", "s"):
                x = xs_by_group[grp]
                b, t, _ = x.shape
                shift0, wkv0 = starts[grp]
                r, lw, k, v, a, gate = _rwkv_proj(x, shift0.reshape(b, 1, d), *proj, tm=tm)
                h0 = jnp.swapaxes(wkv0.astype(F32), -1, -2)
                o, hfin = _wkv(r, lw, k, v, a, *wkv_vecs, h0)
                outs[grp]["wkv"].append(jnp.swapaxes(hfin, -1, -2).astype(wkv0.dtype))
                outs[grp]["shift"].append(x[:, -1])
                xs_by_group[grp] = _post(o.reshape(b * t, d), gate.reshape(b * t, d), x.reshape(b * t, d),
                                         wo, w1, w2, ln, alpha, tm).reshape(b, t, d)
        else:
            w_in = fox_w_in[j]
            wq = w_in[:, :d].astype(BF16)
            wk = w_in[:, d:2 * d].astype(BF16)
            wv = w_in[:, 2 * d:3 * d].astype(BF16)
            wf = _pad_cols(w_in[:, 3 * d:], LANES).astype(BF16)
            bf = _pad_cols(fox_b_f[j].reshape(1, heads), LANES)
            wo = fox_w_o[j].astype(BF16)

            xp = xs_by_group["p"]
            q, kt, vt, kb, vb, lf = _fox_proj(xp, wq, wk, wv, wf, bf, tm, channel_major=True)
            o = _flash(q, kb, vb, _cumsum(lf), tq=512)
            to_heads = lambda z: jnp.transpose(z.reshape(bp, heads, HEAD, tp), (0, 3, 1, 2))
            outs["p"]["k"].append(to_heads(kt))
            outs["p"]["v"].append(to_heads(vt))
            outs["p"]["lf"].append(lf[:, :, :heads])
            xs_by_group["p"] = _post(o.reshape(bp * tp, d), None, xp.reshape(bp * tp, d),
                                     wo, w1, w2, ln, alpha, tm).reshape(bp, tp, d)

            xs = xs_by_group["s"]
            q, k, v, kb, vb, lf = _fox_proj(xs, wq, wk, wv, wf, bf, tm, channel_major=False)
            channel_major = lambda z: jnp.transpose(z, (0, 2, 3, 1)).reshape(bs, d, past)
            o = _decode_attn(q, channel_major(cache_k[j]), channel_major(cache_v[j]), kb, vb,
                             jnp.transpose(cache_logf[j].astype(F32), (0, 2, 1)), lf)
            outs["s"]["k"].append(k.reshape(bs, ts, heads, HEAD))
            outs["s"]["v"].append(v.reshape(bs, ts, heads, HEAD))
            outs["s"]["lf"].append(lf[:, :, :heads])
            xs_by_group["s"] = _post(o.reshape(bs * ts, d), None, xs.reshape(bs * ts, d),
                                     wo, w1, w2, ln, alpha, tm).reshape(bs, ts, d)

    st = jnp.stack
    op, os_ = outs["p"], outs["s"]
    return (xs_by_group["p"], xs_by_group["s"],
            st(op["wkv"]), st(op["shift"]), st(op["k"]), st(op["v"]), st(op["lf"]),
            st(os_["wkv"]), st(os_["shift"]), st(os_["k"]), st(os_["v"]), st(os_["lf"]))
```

```python
import functools
import math

import jax
import jax.numpy as jnp
from jax import lax
from jax.experimental import pallas as pl
from jax.experimental.pallas import tpu as pltpu

BF16 = jnp.bfloat16
F32 = jnp.float32

HEAD = 64
HEAD_SHIFT = 6
LANES = 128
SUBLANES = 8
SLAB = 256
HEADS_PER_SLAB = SLAB // HEAD
CHUNK = 64
WKV_CHUNKS_PER_STEP = 2
LN_EPS = 1e-5
GN_EPS = 64e-5
EXP_M05 = math.exp(-0.5)
LOG2E = math.log2(math.e)
VMEM_LIMIT = 56 * 1024 * 1024


def _dot(a, b):
    return jnp.dot(a, b, preferred_element_type=F32)


def _dot_nt(a, b):
    return lax.dot_general(a, b, (((1,), (1,)), ((), ())), preferred_element_type=F32)


def _dot_tn(a, b):
    return lax.dot_general(a, b, (((0,), (0,)), ((), ())), preferred_element_type=F32)


def _split3(x):
    p1 = x.astype(BF16)
    r1 = x - p1.astype(F32)
    p2 = r1.astype(BF16)
    p3 = (r1 - p2.astype(F32)).astype(BF16)
    return p1, p2, p3


def _dot3(a, b, dot=_dot):
    p1, p2, p3 = _split3(a)
    return dot(p1, b) + dot(p2, b) + dot(p3, b)


def _tri(n, upper):
    r = lax.broadcasted_iota(jnp.int32, (n, n), 0)
    c = lax.broadcasted_iota(jnp.int32, (n, n), 1)
    return jnp.where((r <= c) if upper else (c <= r), 1.0, 0.0).astype(BF16)


def _layer_norm(z, g, b):
    mu = jnp.mean(z, axis=-1, keepdims=True)
    zc = z - mu
    var = jnp.mean(zc * zc, axis=-1, keepdims=True)
    return zc * lax.rsqrt(var + LN_EPS) * g + b


def _params(*sem):
    return pltpu.CompilerParams(dimension_semantics=sem, vmem_limit_bytes=VMEM_LIMIT)


def _const_spec(shape):
    nd = len(shape)
    return pl.BlockSpec(shape, lambda *_: (0,) * nd, pipeline_mode=pl.Buffered(1))


def _row_tile(n, want):
    t = min(n, want)
    assert n % t == 0 and t % SUBLANES == 0, (n, t)
    return t


def _rwkv_proj_kernel(x_ref, prev_ref, first_ref, mu_ref, w0_ref, a0_ref, wr_ref, wk_ref, wv_ref,
                      w1_ref, w2_ref, a1_ref, a2_ref, g1_ref, g2_ref,
                      r_ref, lw_ref, k_ref, v_ref, a_ref, g_ref):
    x = x_ref[0]
    row0 = jnp.where(pl.program_id(1) == 0, first_ref[0], prev_ref[0, SUBLANES - 1:SUBLANES, :])
    is_row0 = lax.broadcasted_iota(jnp.int32, x.shape, 0) == 0
    xx = jnp.where(is_row0, row0, pltpu.roll(x, 1, 0)) - x

    def mix(i):
        return (x + xx * mu_ref[i:i + 1, :]).astype(BF16)

    r_ref[0] = _dot(mix(0), wr_ref[...])
    w_raw = w0_ref[...] + _dot(jnp.tanh(_dot(mix(1), w1_ref[...])).astype(BF16), w2_ref[...])
    lw_ref[0] = -EXP_M05 * jax.nn.sigmoid(w_raw)
    k_ref[0] = _dot(mix(2), wk_ref[...])
    v_ref[0] = _dot(mix(3), wv_ref[...])
    a_ref[0] = jax.nn.sigmoid(a0_ref[...] + _dot(_dot(mix(4), a1_ref[...]).astype(BF16), a2_ref[...]))
    g_ref[0] = _dot(jax.nn.sigmoid(_dot(mix(5), g1_ref[...])).astype(BF16), g2_ref[...])


def _rwkv_proj(x, shift0, mu, w0, a0, wr, wk, wv, w1, w2, a1, a2, g1, g2, tm):
    b, t, d = x.shape
    tm = _row_tile(t, tm)
    per_tile = tm // SUBLANES
    row = pl.BlockSpec((1, tm, d), lambda bi, i: (bi, i, 0))
    prev = pl.BlockSpec((1, SUBLANES, d), lambda bi, i: (bi, jnp.maximum(i * per_tile - 1, 0), 0))
    first = pl.BlockSpec((1, 1, d), lambda bi, i: (bi, 0, 0))
    consts = [mu, w0, a0, wr, wk, wv, w1, w2, a1, a2, g1, g2]
    return pl.pallas_call(
        _rwkv_proj_kernel,
        grid=(b, t // tm),
        in_specs=[row, prev, first] + [_const_spec(c.shape) for c in consts],
        out_specs=[row] * 6,
        out_shape=[jax.ShapeDtypeStruct((b, t, d), F32)] * 6,
        compiler_params=_params("parallel", "parallel"),
        name="rwkv_proj",
    )(x, x, shift0, *consts)


def _wkv_kernel(r_ref, lw_ref, k_ref, v_ref, a_ref, kk_ref, ka_ref, rk_ref, lg_ref, lb_ref, h0_ref,
                o_ref, hout_ref, h_scr, *, last_chunk):
    c = pl.program_id(1)
    nslab = h_scr.shape[0]
    C = CHUNK
    rows = r_ref.shape[1]

    @pl.when(c == 0)
    def _():
        h_scr[...] = jnp.zeros_like(h_scr)
        for s in range(nslab):
            for h in range(HEADS_PER_SLAB):
                hs = slice(h * HEAD, (h + 1) * HEAD)
                h_scr[s, hs, hs] = h0_ref[0, s * HEADS_PER_SLAB + h]

    row_s = lax.broadcasted_iota(jnp.int32, (SLAB, SLAB), 0)
    col_s = lax.broadcasted_iota(jnp.int32, (SLAB, SLAB), 1)
    mask_bd = (row_s >> HEAD_SHIFT) == (col_s >> HEAD_SHIFT)
    eye_s = row_s == col_s
    row_c = lax.broadcasted_iota(jnp.int32, (C, SLAB), 0)
    colin_c = lax.broadcasted_iota(jnp.int32, (C, SLAB), 1) & (C - 1)
    strict_lo = colin_c < row_c
    incl_lo = colin_c <= row_c
    eye_c = colin_c == row_c
    ones_bd = jnp.where(mask_bd, 1.0, 0.0).astype(BF16)

    def bd(x):
        xb = x.astype(BF16)
        return jnp.where(mask_bd, jnp.concatenate([xb] * HEADS_PER_SLAB, axis=0), jnp.zeros((), BF16))

    def head_sum(x):
        return _dot(x.astype(BF16), ones_bd)

    nck = max(rows // C, 1)
    live = min(rows, C)

    def chunk_rows(ref, ci, sl):
        x = ref[0, ci * C:ci * C + live, sl]
        if live < C:
            x = jnp.concatenate([x, jnp.zeros((C - live, x.shape[1]), x.dtype)], axis=0)
        return x

    lw_c = [chunk_rows(lw_ref, ci, slice(None)) for ci in range(nck)]
    L_c = [_dot3(lw_c[ci], _tri(C, upper=False), dot=lambda p, t: _dot(t, p)) for ci in range(nck)]

    units = [(ci, s) for ci in range(nck) for s in range(nslab)]
    S = range(len(units))
    slabs = [slice(s * SLAB, (s + 1) * SLAB) for _, s in units]
    r = [chunk_rows(r_ref, ci, slabs[u]) for u, (ci, _) in enumerate(units)]
    v = [chunk_rows(v_ref, ci, slabs[u]) for u, (ci, _) in enumerate(units)]
    asig = [chunk_rows(a_ref, ci, slabs[u]) for u, (ci, _) in enumerate(units)]
    k0 = [chunk_rows(k_ref, ci, slabs[u]) for u, (ci, _) in enumerate(units)]
    lw = [lw_c[ci][:, slabs[u]] for u, (ci, _) in enumerate(units)]
    kk = [k0[s] * kk_ref[:, slabs[s]] for s in S]
    k = [k0[s] * (1.0 + (asig[s] - 1.0) * ka_ref[:, slabs[s]]) for s in S]
    sums = [head_sum(jnp.concatenate([kk[s] * kk[s], r[s] * k[s] * rk_ref[:, slabs[s]]], axis=0)) for s in S]
    kk = [kk[s] / jnp.maximum(jnp.sqrt(sums[s][:C]), 1e-12) for s in S]
    bonus = [sums[s][C:] * v[s] for s in S]
    bv = [kk[s] * asig[s] for s in S]
    L = [L_c[ci][:, slabs[u]] for u, (ci, _) in enumerate(units)]
    LC = [L[s][C - 1:C, :] for s in S]
    enL = [jnp.exp(-L[s]) for s in S]
    eLc = [jnp.exp(LC[s] - L[s]) for s in S]
    At = [-kk[s] * jnp.exp(L[s] - lw[s]) for s in S]
    Rt = [r[s] * jnp.exp(L[s]) for s in S]
    Bt = [bv[s] * enL[s] for s in S]
    Kt = [k[s] * enL[s] for s in S]
    Bh = [bv[s] * eLc[s] for s in S]
    Kh = [k[s] * eLc[s] for s in S]

    lhs = [jnp.concatenate([At[s], Rt[s]], axis=0).astype(BF16) for s in S]
    ab = [_dot_nt(lhs[s], bd(Bt[s])) for s in S]
    ak = [_dot_nt(lhs[s], bd(Kt[s])) for s in S]
    N = [jnp.where(strict_lo, ab[s][:C], 0.0) for s in S]
    M = [jnp.where(strict_lo, ak[s][:C], 0.0) for s in S]
    Arb = [jnp.where(incl_lo, ab[s][C:], 0.0) for s in S]
    Ark = [jnp.where(incl_lo, ak[s][C:], 0.0) for s in S]
    MV = [_dot(M[s].astype(BF16), bd(v[s])) for s in S]

    X = [jnp.where(eye_c, 1.0, 0.0) + N[s] for s in S]
    Pw = N
    for _ in range(int(math.log2(C)) - 1):
        Pw = [_dot(Pw[s].astype(BF16), bd(Pw[s])) for s in S]
        X = [X[s] + _dot(X[s].astype(BF16), bd(Pw[s])) for s in S]

    wu = [_dot(X[s].astype(BF16), jnp.concatenate([bd(At[s]), bd(MV[s])], axis=1)) for s in S]
    WA = [wu[s][:, :SLAB] for s in S]
    UV = [wu[s][:, SLAB:] for s in S]
    Yl = [_dot(jnp.concatenate([Arb[s], Ark[s]], axis=1).astype(BF16),
               jnp.concatenate([bd(UV[s]), bd(v[s])], axis=0)) for s in S]
    Rp = [Rt[s] + _dot(Arb[s].astype(BF16), bd(WA[s])) for s in S]
    rw = [jnp.concatenate([Rp[s], WA[s]], axis=0).astype(BF16) for s in S]
    bk = [jnp.concatenate([Bh[s], Kh[s]], axis=0).astype(BF16) for s in S]
    pc_col = [jnp.sum(jnp.where(eye_s, jnp.exp(LC[s]), 0.0), axis=1, keepdims=True) for s in S]

    h = [h_scr[s] for s in range(nslab)]
    y = [None] * len(units)
    for ci in range(nck):
        us = [ci * nslab + s for s in range(nslab)]
        yu = [_dot(rw[u], h[s].astype(BF16)) for s, u in enumerate(us)]
        for s, u in enumerate(us):
            y[u] = yu[s][:C] + Yl[u]
        uv = [jnp.concatenate([yu[s][C:] + UV[u], v[u]], axis=0).astype(BF16) for s, u in enumerate(us)]
        h = [h[s] * pc_col[u] + jnp.where(mask_bd, _dot_tn(bk[u], uv[s]), 0.0) for s, u in enumerate(us)]
    for s in range(nslab):
        h_scr[s] = h[s]

    mu = [head_sum(y[s]) * (1.0 / HEAD) for s in S]
    yc = [y[s] - mu[s] for s in S]
    var = [head_sum(yc[s] * yc[s]) * (1.0 / HEAD) for s in S]
    for u, (ci, _) in enumerate(units):
        sl = slabs[u]
        out = yc[u] * lax.rsqrt(var[u] + GN_EPS) * lg_ref[:, sl] + lb_ref[:, sl] + bonus[u]
        o_ref[0, ci * C:ci * C + live, sl] = out[:live]

    @pl.when(c == last_chunk)
    def _():
        for s in range(nslab):
            for h in range(HEADS_PER_SLAB):
                hs = slice(h * HEAD, (h + 1) * HEAD)
                hout_ref[0, s * HEADS_PER_SLAB + h] = h_scr[s, hs, hs]


def _wkv(r, lw, k, v, a, k_k, k_a, r_k, lnx_g, lnx_b, h0):
    b, t, d = r.shape
    rows = WKV_CHUNKS_PER_STEP * CHUNK if t % (WKV_CHUNKS_PER_STEP * CHUNK) == 0 else min(t, CHUNK)
    assert t % rows == 0 and rows % SUBLANES == 0
    nslab = d // SLAB
    heads = d // HEAD
    seq = pl.BlockSpec((1, rows, d), lambda i, c: (i, c, 0))
    vec = pl.BlockSpec((1, d), lambda i, c: (0, 0))
    hspec = pl.BlockSpec((1, heads, HEAD, HEAD), lambda i, c: (i, 0, 0, 0))
    return pl.pallas_call(
        functools.partial(_wkv_kernel, last_chunk=t // rows - 1),
        grid=(b, t // rows),
        in_specs=[seq] * 5 + [vec] * 5 + [hspec],
        out_specs=[seq, hspec],
        out_shape=[jax.ShapeDtypeStruct((b, t, d), F32),
                   jax.ShapeDtypeStruct((b, heads, HEAD, HEAD), F32)],
        scratch_shapes=[pltpu.VMEM((nslab, SLAB, SLAB), F32)],
        compiler_params=_params("parallel", "arbitrary"),
        name="wkv_chunked",
    )(r, lw, k, v, a, k_k, k_a, r_k, lnx_g, lnx_b, h0)


def _post_kernel(*refs, alpha, gated, ff_chunk):
    if gated:
        pre_ref, gate_ref, x_ref, wo_ref, w1_ref, w2_ref, ln_ref, out_ref = refs
        pre = pre_ref[...] * gate_ref[...]
    else:
        pre_ref, x_ref, wo_ref, w1_ref, w2_ref, ln_ref, out_ref = refs
        pre = pre_ref[...]
    h = _dot(pre.astype(BF16), wo_ref[...])
    x1 = _layer_norm(alpha * x_ref[...] + h, ln_ref[0:1, :], ln_ref[1:2, :])
    x1b = x1.astype(BF16)
    m = jnp.zeros_like(x1)
    for f in range(w1_ref.shape[1] // ff_chunk):
        fs = slice(f * ff_chunk, (f + 1) * ff_chunk)
        hid = jnp.maximum(_dot(x1b, w1_ref[:, fs]), 0.0)
        m = m + _dot((hid * hid).astype(BF16), w2_ref[fs, :])
    out_ref[...] = _layer_norm(alpha * x1 + m, ln_ref[2:3, :], ln_ref[3:4, :])


def _post(pre, gate, x, wo, w1, w2, ln, alpha, tm):
    n, d = x.shape
    tm = _row_tile(n, tm)
    row = pl.BlockSpec((tm, d), lambda i: (i, 0))
    acts = [pre, x] if gate is None else [pre, gate, x]
    consts = [wo, w1, w2, ln]
    return pl.pallas_call(
        functools.partial(_post_kernel, alpha=alpha, gated=gate is not None, ff_chunk=min(1024, w1.shape[1])),
        grid=(n // tm,),
        in_specs=[row] * len(acts) + [_const_spec(c.shape) for c in consts],
        out_specs=row,
        out_shape=jax.ShapeDtypeStruct((n, d), F32),
        compiler_params=_params("parallel"),
        name="post_mlp",
    )(*acts, *consts)


def _fox_proj_kernel(x_ref, wq_ref, wk_ref, wv_ref, wf_ref, bf_ref,
                     q_ref, k_ref, v_ref, kb_ref, vb_ref, lf_ref, *, channel_major):
    xb = x_ref[0].astype(BF16)
    q_ref[0] = (_dot(xb, wq_ref[...]) * (HEAD ** -0.5 * LOG2E)).astype(BF16)
    k = _dot(xb, wk_ref[...])
    kb_ref[0] = k.astype(BF16)
    if channel_major:
        k_ref[0] = k.T
        v = _dot_nt(wv_ref[...], xb)
    else:
        k_ref[0] = k
        v = _dot(xb, wv_ref[...])
    v_ref[0] = v
    vb_ref[0] = v.astype(BF16)
    f = _dot(xb, wf_ref[...]) + bf_ref[...]
    lf_ref[0] = jnp.minimum(f, 0.0) - jnp.log1p(jnp.exp(-jnp.abs(f)))


def _fox_proj(x, wq, wk, wv, wf, bf, tm, channel_major):
    b, t, d = x.shape
    tm = _row_tile(t, tm)
    row = pl.BlockSpec((1, tm, d), lambda bi, i: (bi, i, 0))
    rowf = pl.BlockSpec((1, tm, LANES), lambda bi, i: (bi, i, 0))
    kv_spec = pl.BlockSpec((1, d, tm), lambda bi, i: (bi, 0, i)) if channel_major else row
    kv_shape = jax.ShapeDtypeStruct((b, d, t) if channel_major else (b, t, d), F32)
    act_bf = jax.ShapeDtypeStruct((b, t, d), BF16)
    vb_shape = jax.ShapeDtypeStruct(kv_shape.shape, BF16)
    consts = [wq, wk, wv, wf, bf]
    return pl.pallas_call(
        functools.partial(_fox_proj_kernel, channel_major=channel_major),
        grid=(b, t // tm),
        in_specs=[row] + [_const_spec(c.shape) for c in consts],
        out_specs=[row, kv_spec, kv_spec, row, kv_spec, rowf],
        out_shape=[act_bf, kv_shape, kv_shape, act_bf, vb_shape, jax.ShapeDtypeStruct((b, t, LANES), F32)],
        compiler_params=_params("parallel", "parallel"),
        name="fox_proj",
    )(x, *consts)


CS_TILE = 512


def _cumsum_kernel(lf_ref, c_ref, carry_ref):
    @pl.when(pl.program_id(1) == 0)
    def _():
        carry_ref[...] = jnp.zeros_like(carry_ref)

    n = lf_ref.shape[1]
    cs = _dot3(lf_ref[0], _tri(n, upper=False), dot=lambda p, t: _dot(t, p)) + carry_ref[0:1, :]
    c_ref[0] = cs
    carry_ref[...] = jnp.broadcast_to(cs[n - 1:n, :], carry_ref.shape)


def _cumsum(lf):
    b, l, w = lf.shape
    tile = _row_tile(l, CS_TILE)
    spec = pl.BlockSpec((1, tile, w), lambda i, j: (i, j, 0))
    return pl.pallas_call(
        _cumsum_kernel,
        grid=(b, l // tile),
        in_specs=[spec],
        out_specs=spec,
        out_shape=jax.ShapeDtypeStruct((b, l, w), F32),
        scratch_shapes=[pltpu.VMEM((SUBLANES, w), F32)],
        compiler_params=_params("parallel", "arbitrary"),
        name="logf_cumsum",
    )(lf)


def _bias_lanes(c, head0, key_side):
    nh = LANES // HEAD
    src = lax.broadcasted_iota(jnp.int32, (LANES, LANES), 0)
    dst = lax.broadcasted_iota(jnp.int32, (LANES, LANES), 1)
    lane = lax.broadcasted_iota(jnp.int32, (1, LANES), 1)
    out = jnp.zeros(c.shape, F32)
    pieces = _split3(c * LOG2E)
    lo, sign = (3, -1.0) if key_side else (0, 1.0)
    for hh in range(nh):
        base = ((hh + 1) % nh) * HEAD
        for n, p in enumerate(pieces):
            place = jnp.where((src == head0 + hh) & (dst == base + lo + n), sign, 0.0).astype(BF16)
            out = out + _dot(p, place)
        rel = lane - base
        out = out + jnp.where((rel >= 3 - lo) & (rel < 6 - lo), 1.0, 0.0)
    return out


def _flash_kernel(q_ref, k_ref, vt_ref, c_ref, o_ref, kaug_scr, *, tq):
    g = pl.program_id(1)
    i = pl.program_id(2)
    t = k_ref.shape[1]
    nh = LANES // HEAD
    lane = lax.broadcasted_iota(jnp.int32, (1, LANES), 1)
    in_head = [(lane >> HEAD_SHIFT) == hh for hh in range(nh)]
    bias_lanes = [(lane >= ((hh + 1) % nh) * HEAD) & (lane < ((hh + 1) % nh) * HEAD + 6) for hh in range(nh)]

    def augment(x, bias):
        bias = bias.astype(BF16)
        return [jnp.where(in_head[hh], x, jnp.where(bias_lanes[hh], bias, jnp.zeros((), BF16)))
                for hh in range(nh)]

    @pl.when(i == 0)
    def _():
        for blk in range(t // tq):
            rows = slice(blk * tq, (blk + 1) * tq)
            ka = augment(k_ref[0, rows, :], _bias_lanes(c_ref[0, rows, :], nh * g, key_side=True))
            for hh in range(nh):
                kaug_scr[hh, rows, :] = ka[hh]

    row0 = pl.multiple_of(i * tq, tq)
    qa = augment(q_ref[0], _bias_lanes(c_ref[0, pl.ds(row0, tq), :], nh * g, key_side=False))

    qg = min(tq, SLAB)
    chains = [(hh, qs) for hh in range(nh) for qs in range(0, tq, qg)]
    qa_c = [qa[hh][qs:qs + qg] for hh, qs in chains]

    def step(off, carry, mask):
        ks = [kaug_scr[hh, pl.ds(off, tq), :] for hh in range(nh)]
        vts = [vt_ref[0, hh * HEAD:(hh + 1) * HEAD, pl.ds(off, tq)] for hh in range(nh)]
        st = [_dot_nt(ks[hh], qa_c[n]) for n, (hh, _) in enumerate(chains)]
        if mask is not None:
            st = [jnp.where(mask[:, qs:qs + qg], st[n], -jnp.inf) for n, (_, qs) in enumerate(chains)]
        m_new = [jnp.maximum(carry[n][0], jnp.max(st[n], axis=0, keepdims=True)) for n in range(len(chains))]
        alpha = [jnp.exp2(carry[n][0] - m_new[n]) for n in range(len(chains))]
        p = [jnp.exp2(st[n] - m_new[n]) for n in range(len(chains))]
        l = [alpha[n] * carry[n][1] + jnp.sum(p[n], axis=0, keepdims=True) for n in range(len(chains))]
        acc = [alpha[n] * carry[n][2] + _dot(vts[hh], p[n].astype(BF16)) for n, (hh, _) in enumerate(chains)]
        return tuple(zip(m_new, l, acc))

    init = tuple((jnp.full((1, qg), -jnp.inf, F32), jnp.zeros((1, qg), F32), jnp.zeros((HEAD, qg), F32))
                 for _ in chains)
    carry = lax.fori_loop(0, i, lambda j, c: step(pl.multiple_of(j * tq, tq), c, None), init)
    causal = lax.broadcasted_iota(jnp.int32, (tq, tq), 0) <= lax.broadcasted_iota(jnp.int32, (tq, tq), 1)
    carry = step(row0, carry, causal)
    per_head = [jnp.concatenate([carry[n][2] / carry[n][1] for n, (h2, _) in enumerate(chains) if h2 == hh], axis=1)
                for hh in range(nh)]
    o_ref[0] = jnp.concatenate(per_head, axis=0).T


def _flash(q, k, vt, c, tq):
    b, t, d = q.shape
    tq = _row_tile(t, tq)
    g = d // LANES
    return pl.pallas_call(
        functools.partial(_flash_kernel, tq=tq),
        grid=(b, g, t // tq),
        in_specs=[pl.BlockSpec((1, tq, LANES), lambda bi, gi, i: (bi, i, gi)),
                  pl.BlockSpec((1, t, LANES), lambda bi, gi, i: (bi, 0, gi)),
                  pl.BlockSpec((1, LANES, t), lambda bi, gi, i: (bi, gi, 0)),
                  pl.BlockSpec((1, t, LANES), lambda bi, gi, i: (bi, 0, 0))],
        out_specs=pl.BlockSpec((1, tq, LANES), lambda bi, gi, i: (bi, i, gi)),
        out_shape=jax.ShapeDtypeStruct((b, t, d), F32),
        scratch_shapes=[pltpu.VMEM((LANES // HEAD, t, LANES), BF16)],
        compiler_params=_params("parallel", "parallel", "arbitrary"),
        name="fox_flash",
    )(q, k, vt, c)


DEC_TILE = 512


def _decode_attn_kernel(q_ref, ckt_ref, cvt_ref, kn_ref, vn_ref, clf_ref, lfn_ref, o_ref, c_scr):
    g = pl.program_id(1)
    nh = LANES // HEAD
    past = ckt_ref.shape[2]
    t = q_ref.shape[1]

    @pl.when(g == 0)
    def _():
        tri = _tri(DEC_TILE, upper=True)
        carry = jnp.zeros((clf_ref.shape[1], 1), F32)
        for blk in range(past // DEC_TILE):
            cols = slice(blk * DEC_TILE, (blk + 1) * DEC_TILE)
            cs = _dot3(clf_ref[0, :, cols], tri) + carry
            c_scr[:, cols] = cs
            carry = cs[:, DEC_TILE - 1:DEC_TILE]

    lane = lax.broadcasted_iota(jnp.int32, (1, LANES), 1)
    q = q_ref[0]
    kct = ckt_ref[0].astype(BF16)
    vct = cvt_ref[0].astype(BF16)
    kn = kn_ref[0]
    vn = vn_ref[0]
    causal = lax.broadcasted_iota(jnp.int32, (t, t), 1) <= lax.broadcasted_iota(jnp.int32, (t, t), 0)
    lf_new = _dot3(lfn_ref[0], _tri(t, upper=False), dot=lambda p, tr: _dot(tr, p))
    outs = []
    for hh in range(nh):
        head = nh * g + hh
        qh = jnp.where((lane >> HEAD_SHIFT) == hh, q, jnp.zeros((), BF16))
        c_cache = c_scr[pl.ds(head, 1), :]
        total = c_cache[:, past - 1:past]
        onehot = jnp.where(lane == head, 1.0, 0.0)
        cn_col = jnp.sum(lf_new * onehot, axis=-1, keepdims=True) + total
        sel = jnp.broadcast_to(onehot, (SUBLANES, LANES)).astype(BF16)
        cn_row = _dot3(lf_new, sel, dot=lambda p, e: _dot_nt(e, p))[0:1] + total
        s1 = _dot(qh, kct) + (cn_col - c_cache) * LOG2E
        s2 = _dot_nt(qh, kn) + (cn_col - cn_row) * LOG2E
        s2 = jnp.where(causal, s2, -jnp.inf)
        m = jnp.maximum(jnp.max(s1, axis=-1, keepdims=True), jnp.max(s2, axis=-1, keepdims=True))
        e1 = jnp.exp2(s1 - m)
        e2 = jnp.exp2(s2 - m)
        l = jnp.sum(e1, axis=-1, keepdims=True) + jnp.sum(e2, axis=-1, keepdims=True)
        outs.append((_dot_nt(e1.astype(BF16), vct) + _dot(e2.astype(BF16), vn)) / l)
    o_ref[0] = jnp.where((lane >> HEAD_SHIFT) == 0, outs[0], outs[1])


def _decode_attn(q, cache_kt, cache_vt, kn, vn, cache_lf, lf_new):
    b, t, d = q.shape
    p = cache_kt.shape[2]
    h = cache_lf.shape[1]
    assert p % DEC_TILE == 0
    g = d // LANES
    new = pl.BlockSpec((1, t, LANES), lambda bi, gi: (bi, 0, gi))
    old = pl.BlockSpec((1, LANES, p), lambda bi, gi: (bi, gi, 0))
    return pl.pallas_call(
        _decode_attn_kernel,
        grid=(b, g),
        in_specs=[new, old, old, new, new,
                  pl.BlockSpec((1, h, p), lambda bi, gi: (bi, 0, 0)),
                  pl.BlockSpec((1, t, LANES), lambda bi, gi: (bi, 0, 0))],
        out_specs=new,
        out_shape=jax.ShapeDtypeStruct((b, t, d), F32),
        scratch_shapes=[pltpu.VMEM((h, p), F32)],
        compiler_params=_params("parallel", "arbitrary"),
        name="fox_decode_attn",
    )(q, cache_kt, cache_vt, kn, vn, cache_lf, lf_new)


def _pad_cols(w, mult):
    pad = (-w.shape[-1]) % mult
    return jnp.pad(w, ((0, 0), (0, pad))) if pad else w


def _pad_rows(w, mult):
    pad = (-w.shape[0]) % mult
    return jnp.pad(w, ((0, pad), (0, 0))) if pad else w


def kernel(x_prompt, x_sample, state_wkv, state_shift, cache_k, cache_v, cache_logf, rwkv_mu, rwkv_w0, rwkv_w1, rwkv_w2, rwkv_a0, rwkv_a1, rwkv_a2, rwkv_g1, rwkv_g2, rwkv_k_k, rwkv_k_a, rwkv_r_k, rwkv_w_r, rwkv_w_k, rwkv_w_v, rwkv_w_o, rwkv_lnx_g, rwkv_lnx_b, fox_w_in, fox_b_f, fox_w_o, ffn_w1, ffn_w2, ln_mix_g, ln_mix_b, ln_ffn_g, ln_ffn_b):
    depth = ln_mix_g.shape[0]
    alpha = (2 * depth) ** 0.25
    bp, tp, d = x_prompt.shape
    bs, ts, _ = x_sample.shape
    heads = d // HEAD
    past = cache_k.shape[2]
    tm = 256

    xs_by_group = {"p": x_prompt, "s": x_sample}
    outs = {grp: {n: [] for n in ("wkv", "shift", "k", "v", "lf")} for grp in ("p", "s")}
    for i in range(depth):
        j = i // 2
        ln = jnp.stack([ln_mix_g[i], ln_mix_b[i], ln_ffn_g[i], ln_ffn_b[i]])
        w1 = ffn_w1[i].astype(BF16)
        w2 = ffn_w2[i].astype(BF16)
        if i % 2 == 0:
            row = lambda z: z.reshape(1, d)
            proj = (rwkv_mu[j], row(rwkv_w0[j]), row(rwkv_a0[j]),
                    rwkv_w_r[j].astype(BF16), rwkv_w_k[j].astype(BF16), rwkv_w_v[j].astype(BF16),
                    _pad_cols(rwkv_w1[j], LANES).astype(BF16), _pad_rows(rwkv_w2[j], LANES).astype(BF16),
                    _pad_cols(rwkv_a1[j], LANES).astype(BF16), _pad_rows(rwkv_a2[j], LANES).astype(BF16),
                    _pad_cols(rwkv_g1[j], LANES).astype(BF16), _pad_rows(rwkv_g2[j], LANES).astype(BF16))
            wkv_vecs = (row(rwkv_k_k[j]), row(rwkv_k_a[j]), row(rwkv_r_k[j]), row(rwkv_lnx_g[j]), row(rwkv_lnx_b[j]))
            wo = rwkv_w_o[j].astype(BF16)
            starts = {"p": (jnp.zeros((bp, d), x_prompt.dtype), jnp.zeros((bp, heads, HEAD, HEAD), state_wkv.dtype)),
                      "s": (state_shift[j], state_wkv[j])}
            for grp in ("p", "s"):
                x = xs_by_group[grp]
                b, t, _ = x.shape
                shift0, wkv0 = starts[grp]
                r, lw, k, v, a, gate = _rwkv_proj(x, shift0.reshape(b, 1, d), *proj, tm=tm)
                h0 = jnp.swapaxes(wkv0.astype(F32), -1, -2)
                o, hfin = _wkv(r, lw, k, v, a, *wkv_vecs, h0)
                outs[grp]["wkv"].append(jnp.swapaxes(hfin, -1, -2).astype(wkv0.dtype))
                outs[grp]["shift"].append(x[:, -1])
                xs_by_group[grp] = _post(o.reshape(b * t, d), gate.reshape(b * t, d), x.reshape(b * t, d),
                                         wo, w1, w2, ln, alpha, tm).reshape(b, t, d)
        else:
            w_in = fox_w_in[j]
            wq = w_in[:, :d].astype(BF16)
            wk = w_in[:, d:2 * d].astype(BF16)
            wv = w_in[:, 2 * d:3 * d].astype(BF16)
            wf = _pad_cols(w_in[:, 3 * d:], LANES).astype(BF16)
            bf = _pad_cols(fox_b_f[j].reshape(1, heads), LANES)
            wo = fox_w_o[j].astype(BF16)

            xp = xs_by_group["p"]
            q, kt, vt, kb, vtb, lf = _fox_proj(xp, wq, wk, wv.T, wf, bf, tm, channel_major=True)
            o = _flash(q, kb, vtb, _cumsum(lf), tq=512)
            to_heads = lambda z: jnp.transpose(z.reshape(bp, heads, HEAD, tp), (0, 3, 1, 2))
            outs["p"]["k"].append(to_heads(kt))
            outs["p"]["v"].append(to_heads(vt))
            outs["p"]["lf"].append(lf[:, :, :heads])
            xs_by_group["p"] = _post(o.reshape(bp * tp, d), None, xp.reshape(bp * tp, d),
                                     wo, w1, w2, ln, alpha, tm).reshape(bp, tp, d)

            xs = xs_by_group["s"]
            q, k, v, kb, vb, lf = _fox_proj(xs, wq, wk, wv, wf, bf, tm, channel_major=False)
            channel_major = lambda z: jnp.transpose(z, (0, 2, 3, 1)).reshape(bs, d, past)
            o = _decode_attn(q, channel_major(cache_k[j]), channel_major(cache_v[j]), kb, vb,
                             jnp.transpose(cache_logf[j].astype(F32), (0, 2, 1)), lf)
            outs["s"]["k"].append(k.reshape(bs, ts, heads, HEAD))
            outs["s"]["v"].append(v.reshape(bs, ts, heads, HEAD))
            outs["s"]["lf"].append(lf[:, :, :heads])
            xs_by_group["s"] = _post(o.reshape(bs * ts, d), None, xs.reshape(bs * ts, d),
                                     wo, w1, w2, ln, alpha, tm).reshape(bs, ts, d)

    st = jnp.stack
    op, os_ = outs["p"], outs["s"]
    return (xs_by_group["p"], xs_by_group["s"],
            st(op["wkv"]), st(op["shift"]), st(op["k"]), st(op["v"]), st(op["lf"]),
            st(os_["wkv"]), st(os_["shift"]), st(os_["k"]), st(os_["v"]), st(os_["lf"]))
```

```python
import functools
import math

import jax
import jax.numpy as jnp
from jax import lax
from jax.experimental import pallas as pl
from jax.experimental.pallas import tpu as pltpu

BF16 = jnp.bfloat16
F32 = jnp.float32

HEAD = 64
HEAD_SHIFT = 6
LANES = 128
SUBLANES = 8
SLAB = 256
HEADS_PER_SLAB = SLAB // HEAD
CHUNK = 64
WKV_CHUNKS_PER_STEP = 2
LN_EPS = 1e-5
GN_EPS = 64e-5
EXP_M05 = math.exp(-0.5)
LOG2E = math.log2(math.e)
VMEM_LIMIT = 56 * 1024 * 1024


def _dot(a, b):
    return jnp.dot(a, b, preferred_element_type=F32)


def _dot_nt(a, b):
    return lax.dot_general(a, b, (((1,), (1,)), ((), ())), preferred_element_type=F32)


def _dot_tn(a, b):
    return lax.dot_general(a, b, (((0,), (0,)), ((), ())), preferred_element_type=F32)


def _split3(x):
    p1 = x.astype(BF16)
    r1 = x - p1.astype(F32)
    p2 = r1.astype(BF16)
    p3 = (r1 - p2.astype(F32)).astype(BF16)
    return p1, p2, p3


def _dot3(a, b, dot=_dot):
    p1, p2, p3 = _split3(a)
    return dot(p1, b) + dot(p2, b) + dot(p3, b)


def _tri(n, upper):
    r = lax.broadcasted_iota(jnp.int32, (n, n), 0)
    c = lax.broadcasted_iota(jnp.int32, (n, n), 1)
    return jnp.where((r <= c) if upper else (c <= r), 1.0, 0.0).astype(BF16)


def _layer_norm(z, g, b):
    mu = jnp.mean(z, axis=-1, keepdims=True)
    zc = z - mu
    var = jnp.mean(zc * zc, axis=-1, keepdims=True)
    return zc * lax.rsqrt(var + LN_EPS) * g + b


def _params(*sem):
    return pltpu.CompilerParams(dimension_semantics=sem, vmem_limit_bytes=VMEM_LIMIT)


def _const_spec(shape):
    nd = len(shape)
    return pl.BlockSpec(shape, lambda *_: (0,) * nd, pipeline_mode=pl.Buffered(1))


def _row_tile(n, want):
    t = min(n, want)
    assert n % t == 0 and t % SUBLANES == 0, (n, t)
    return t


def _rwkv_proj_kernel(x_ref, prev_ref, first_ref, mu_ref, w0_ref, a0_ref, wr_ref, wk_ref, wv_ref,
                      w1_ref, w2_ref, a1_ref, a2_ref, g1_ref, g2_ref,
                      r_ref, lw_ref, k_ref, v_ref, a_ref, g_ref):
    x = x_ref[0]
    row0 = jnp.where(pl.program_id(1) == 0, first_ref[0], prev_ref[0, SUBLANES - 1:SUBLANES, :])
    is_row0 = lax.broadcasted_iota(jnp.int32, x.shape, 0) == 0
    xx = jnp.where(is_row0, row0, pltpu.roll(x, 1, 0)) - x

    def mix(i):
        return (x + xx * mu_ref[i:i + 1, :]).astype(BF16)

    r_ref[0] = _dot(mix(0), wr_ref[...])
    w_raw = w0_ref[...] + _dot(jnp.tanh(_dot(mix(1), w1_ref[...])).astype(BF16), w2_ref[...])
    lw_ref[0] = -EXP_M05 * jax.nn.sigmoid(w_raw)
    k_ref[0] = _dot(mix(2), wk_ref[...])
    v_ref[0] = _dot(mix(3), wv_ref[...])
    a_ref[0] = jax.nn.sigmoid(a0_ref[...] + _dot(_dot(mix(4), a1_ref[...]).astype(BF16), a2_ref[...]))
    g_ref[0] = _dot(jax.nn.sigmoid(_dot(mix(5), g1_ref[...])).astype(BF16), g2_ref[...])


def _rwkv_proj(x, shift0, mu, w0, a0, wr, wk, wv, w1, w2, a1, a2, g1, g2, tm):
    b, t, d = x.shape
    tm = _row_tile(t, tm)
    per_tile = tm // SUBLANES
    row = pl.BlockSpec((1, tm, d), lambda bi, i: (bi, i, 0))
    prev = pl.BlockSpec((1, SUBLANES, d), lambda bi, i: (bi, jnp.maximum(i * per_tile - 1, 0), 0))
    first = pl.BlockSpec((1, 1, d), lambda bi, i: (bi, 0, 0))
    consts = [mu, w0, a0, wr, wk, wv, w1, w2, a1, a2, g1, g2]
    return pl.pallas_call(
        _rwkv_proj_kernel,
        grid=(b, t // tm),
        in_specs=[row, prev, first] + [_const_spec(c.shape) for c in consts],
        out_specs=[row] * 6,
        out_shape=[jax.ShapeDtypeStruct((b, t, d), F32)] * 6,
        compiler_params=_params("parallel", "parallel"),
        name="rwkv_proj",
    )(x, x, shift0, *consts)


def _wkv_kernel(r_ref, lw_ref, k_ref, v_ref, a_ref, kk_ref, ka_ref, rk_ref, lg_ref, lb_ref, h0_ref,
                o_ref, hout_ref, h_scr, *, last_chunk):
    c = pl.program_id(1)
    nslab = h_scr.shape[0]
    C = CHUNK
    rows = r_ref.shape[1]

    @pl.when(c == 0)
    def _():
        h_scr[...] = jnp.zeros_like(h_scr)
        for s in range(nslab):
            for h in range(HEADS_PER_SLAB):
                hs = slice(h * HEAD, (h + 1) * HEAD)
                h_scr[s, hs, hs] = h0_ref[0, s * HEADS_PER_SLAB + h]

    row_s = lax.broadcasted_iota(jnp.int32, (SLAB, SLAB), 0)
    col_s = lax.broadcasted_iota(jnp.int32, (SLAB, SLAB), 1)
    mask_bd = (row_s >> HEAD_SHIFT) == (col_s >> HEAD_SHIFT)
    eye_s = row_s == col_s
    row_c = lax.broadcasted_iota(jnp.int32, (C, SLAB), 0)
    colin_c = lax.broadcasted_iota(jnp.int32, (C, SLAB), 1) & (C - 1)
    strict_lo = colin_c < row_c
    incl_lo = colin_c <= row_c
    eye_c = colin_c == row_c
    ones_bd = jnp.where(mask_bd, 1.0, 0.0).astype(BF16)

    def bd(x):
        xb = x.astype(BF16)
        return jnp.where(mask_bd, jnp.concatenate([xb] * HEADS_PER_SLAB, axis=0), jnp.zeros((), BF16))

    def head_sum(x):
        return _dot(x.astype(BF16), ones_bd)

    nck = max(rows // C, 1)
    live = min(rows, C)

    def chunk_rows(ref, ci, sl):
        x = ref[0, ci * C:ci * C + live, sl]
        if live < C:
            x = jnp.concatenate([x, jnp.zeros((C - live, x.shape[1]), x.dtype)], axis=0)
        return x

    lw_c = [chunk_rows(lw_ref, ci, slice(None)) for ci in range(nck)]
    L_c = [_dot3(lw_c[ci], _tri(C, upper=False), dot=lambda p, t: _dot(t, p)) for ci in range(nck)]

    units = [(ci, s) for ci in range(nck) for s in range(nslab)]
    S = range(len(units))
    slabs = [slice(s * SLAB, (s + 1) * SLAB) for _, s in units]
    r = [chunk_rows(r_ref, ci, slabs[u]) for u, (ci, _) in enumerate(units)]
    v = [chunk_rows(v_ref, ci, slabs[u]) for u, (ci, _) in enumerate(units)]
    asig = [chunk_rows(a_ref, ci, slabs[u]) for u, (ci, _) in enumerate(units)]
    k0 = [chunk_rows(k_ref, ci, slabs[u]) for u, (ci, _) in enumerate(units)]
    lw = [lw_c[ci][:, slabs[u]] for u, (ci, _) in enumerate(units)]
    kk = [k0[s] * kk_ref[:, slabs[s]] for s in S]
    k = [k0[s] * (1.0 + (asig[s] - 1.0) * ka_ref[:, slabs[s]]) for s in S]
    sums = [head_sum(jnp.concatenate([kk[s] * kk[s], r[s] * k[s] * rk_ref[:, slabs[s]]], axis=0)) for s in S]
    kk = [kk[s] / jnp.maximum(jnp.sqrt(sums[s][:C]), 1e-12) for s in S]
    bonus = [sums[s][C:] * v[s] for s in S]
    bv = [kk[s] * asig[s] for s in S]
    L = [L_c[ci][:, slabs[u]] for u, (ci, _) in enumerate(units)]
    LC = [L[s][C - 1:C, :] for s in S]
    enL = [jnp.exp(-L[s]) for s in S]
    eLc = [jnp.exp(LC[s] - L[s]) for s in S]
    At = [-kk[s] * jnp.exp(L[s] - lw[s]) for s in S]
    Rt = [r[s] * jnp.exp(L[s]) for s in S]
    Bt = [bv[s] * enL[s] for s in S]
    Kt = [k[s] * enL[s] for s in S]
    Bh = [bv[s] * eLc[s] for s in S]
    Kh = [k[s] * eLc[s] for s in S]

    lhs = [jnp.concatenate([At[s], Rt[s]], axis=0).astype(BF16) for s in S]
    ab = [_dot_nt(lhs[s], bd(Bt[s])) for s in S]
    ak = [_dot_nt(lhs[s], bd(Kt[s])) for s in S]
    N = [jnp.where(strict_lo, ab[s][:C], 0.0) for s in S]
    M = [jnp.where(strict_lo, ak[s][:C], 0.0) for s in S]
    Arb = [jnp.where(incl_lo, ab[s][C:], 0.0) for s in S]
    Ark = [jnp.where(incl_lo, ak[s][C:], 0.0) for s in S]
    MV = [_dot(M[s].astype(BF16), bd(v[s])) for s in S]

    X = [jnp.where(eye_c, 1.0, 0.0) + N[s] for s in S]
    Pw = N
    for _ in range(int(math.log2(C)) - 1):
        Pw = [_dot(Pw[s].astype(BF16), bd(Pw[s])) for s in S]
        X = [X[s] + _dot(X[s].astype(BF16), bd(Pw[s])) for s in S]

    wu = [_dot(X[s].astype(BF16), jnp.concatenate([bd(At[s]), bd(MV[s])], axis=1)) for s in S]
    WA = [wu[s][:, :SLAB] for s in S]
    UV = [wu[s][:, SLAB:] for s in S]
    Yl = [_dot(jnp.concatenate([Arb[s], Ark[s]], axis=1).astype(BF16),
               jnp.concatenate([bd(UV[s]), bd(v[s])], axis=0)) for s in S]
    Rp = [Rt[s] + _dot(Arb[s].astype(BF16), bd(WA[s])) for s in S]
    rw = [jnp.concatenate([Rp[s], WA[s]], axis=0).astype(BF16) for s in S]
    bk = [jnp.concatenate([Bh[s], Kh[s]], axis=0).astype(BF16) for s in S]
    pc_col = [jnp.sum(jnp.where(eye_s, jnp.exp(LC[s]), 0.0), axis=1, keepdims=True) for s in S]

    h = [h_scr[s] for s in range(nslab)]
    y = [None] * len(units)
    for ci in range(nck):
        us = [ci * nslab + s for s in range(nslab)]
        yu = [_dot(rw[u], h[s].astype(BF16)) for s, u in enumerate(us)]
        for s, u in enumerate(us):
            y[u] = yu[s][:C] + Yl[u]
        uv = [jnp.concatenate([yu[s][C:] + UV[u], v[u]], axis=0).astype(BF16) for s, u in enumerate(us)]
        h = [h[s] * pc_col[u] + jnp.where(mask_bd, _dot_tn(bk[u], uv[s]), 0.0) for s, u in enumerate(us)]
    for s in range(nslab):
        h_scr[s] = h[s]

    mu = [head_sum(y[s]) * (1.0 / HEAD) for s in S]
    yc = [y[s] - mu[s] for s in S]
    var = [head_sum(yc[s] * yc[s]) * (1.0 / HEAD) for s in S]
    for u, (ci, _) in enumerate(units):
        sl = slabs[u]
        out = yc[u] * lax.rsqrt(var[u] + GN_EPS) * lg_ref[:, sl] + lb_ref[:, sl] + bonus[u]
        o_ref[0, ci * C:ci * C + live, sl] = out[:live]

    @pl.when(c == last_chunk)
    def _():
        for s in range(nslab):
            for h in range(HEADS_PER_SLAB):
                hs = slice(h * HEAD, (h + 1) * HEAD)
                hout_ref[0, s * HEADS_PER_SLAB + h] = h_scr[s, hs, hs]


def _wkv(r, lw, k, v, a, k_k, k_a, r_k, lnx_g, lnx_b, h0):
    b, t, d = r.shape
    rows = WKV_CHUNKS_PER_STEP * CHUNK if t % (WKV_CHUNKS_PER_STEP * CHUNK) == 0 else min(t, CHUNK)
    assert t % rows == 0 and rows % SUBLANES == 0
    nslab = d // SLAB
    heads = d // HEAD
    seq = pl.BlockSpec((1, rows, d), lambda i, c: (i, c, 0))
    vec = pl.BlockSpec((1, d), lambda i, c: (0, 0))
    hspec = pl.BlockSpec((1, heads, HEAD, HEAD), lambda i, c: (i, 0, 0, 0))
    return pl.pallas_call(
        functools.partial(_wkv_kernel, last_chunk=t // rows - 1),
        grid=(b, t // rows),
        in_specs=[seq] * 5 + [vec] * 5 + [hspec],
        out_specs=[seq, hspec],
        out_shape=[jax.ShapeDtypeStruct((b, t, d), F32),
                   jax.ShapeDtypeStruct((b, heads, HEAD, HEAD), F32)],
        scratch_shapes=[pltpu.VMEM((nslab, SLAB, SLAB), F32)],
        compiler_params=_params("parallel", "arbitrary"),
        name="wkv_chunked",
    )(r, lw, k, v, a, k_k, k_a, r_k, lnx_g, lnx_b, h0)


def _post_kernel(*refs, alpha, gated, ff_chunk):
    if gated:
        pre_ref, gate_ref, x_ref, wo_ref, w1_ref, w2_ref, ln_ref, out_ref = refs
        pre = pre_ref[...] * gate_ref[...]
    else:
        pre_ref, x_ref, wo_ref, w1_ref, w2_ref, ln_ref, out_ref = refs
        pre = pre_ref[...]
    h = _dot(pre.astype(BF16), wo_ref[...])
    x1 = _layer_norm(alpha * x_ref[...] + h, ln_ref[0:1, :], ln_ref[1:2, :])
    x1b = x1.astype(BF16)
    m = jnp.zeros_like(x1)
    for f in range(w1_ref.shape[1] // ff_chunk):
        fs = slice(f * ff_chunk, (f + 1) * ff_chunk)
        hid = jnp.maximum(_dot(x1b, w1_ref[:, fs]), 0.0)
        m = m + _dot((hid * hid).astype(BF16), w2_ref[fs, :])
    out_ref[...] = _layer_norm(alpha * x1 + m, ln_ref[2:3, :], ln_ref[3:4, :])


def _post(pre, gate, x, wo, w1, w2, ln, alpha, tm):
    n, d = x.shape
    tm = _row_tile(n, tm)
    row = pl.BlockSpec((tm, d), lambda i: (i, 0))
    acts = [pre, x] if gate is None else [pre, gate, x]
    consts = [wo, w1, w2, ln]
    return pl.pallas_call(
        functools.partial(_post_kernel, alpha=alpha, gated=gate is not None, ff_chunk=min(1024, w1.shape[1])),
        grid=(n // tm,),
        in_specs=[row] * len(acts) + [_const_spec(c.shape) for c in consts],
        out_specs=row,
        out_shape=jax.ShapeDtypeStruct((n, d), F32),
        compiler_params=_params("parallel"),
        name="post_mlp",
    )(*acts, *consts)


def _fox_proj_kernel(x_ref, wq_ref, wk_ref, wv_ref, wf_ref, bf_ref,
                     q_ref, k_ref, v_ref, kb_ref, vb_ref, lf_ref, *, channel_major):
    xb = x_ref[0].astype(BF16)
    q_ref[0] = (_dot(xb, wq_ref[...]) * (HEAD ** -0.5 * LOG2E)).astype(BF16)
    k = _dot(xb, wk_ref[...])
    kb_ref[0] = k.astype(BF16)
    if channel_major:
        k_ref[0] = k.T
        v = _dot_nt(wv_ref[...], xb)
    else:
        k_ref[0] = k
        v = _dot(xb, wv_ref[...])
    v_ref[0] = v
    vb_ref[0] = v.astype(BF16)
    f = _dot(xb, wf_ref[...]) + bf_ref[...]
    lf_ref[0] = jnp.minimum(f, 0.0) - jnp.log1p(jnp.exp(-jnp.abs(f)))


def _fox_proj(x, wq, wk, wv, wf, bf, tm, channel_major):
    b, t, d = x.shape
    tm = _row_tile(t, tm)
    row = pl.BlockSpec((1, tm, d), lambda bi, i: (bi, i, 0))
    rowf = pl.BlockSpec((1, tm, LANES), lambda bi, i: (bi, i, 0))
    kv_spec = pl.BlockSpec((1, d, tm), lambda bi, i: (bi, 0, i)) if channel_major else row
    kv_shape = jax.ShapeDtypeStruct((b, d, t) if channel_major else (b, t, d), F32)
    act_bf = jax.ShapeDtypeStruct((b, t, d), BF16)
    vb_shape = jax.ShapeDtypeStruct(kv_shape.shape, BF16)
    consts = [wq, wk, wv, wf, bf]
    return pl.pallas_call(
        functools.partial(_fox_proj_kernel, channel_major=channel_major),
        grid=(b, t // tm),
        in_specs=[row] + [_const_spec(c.shape) for c in consts],
        out_specs=[row, kv_spec, kv_spec, row, kv_spec, rowf],
        out_shape=[act_bf, kv_shape, kv_shape, act_bf, vb_shape, jax.ShapeDtypeStruct((b, t, LANES), F32)],
        compiler_params=_params("parallel", "parallel"),
        name="fox_proj",
    )(x, *consts)


CS_TILE = 512


def _cumsum_kernel(lf_ref, c_ref, carry_ref):
    @pl.when(pl.program_id(1) == 0)
    def _():
        carry_ref[...] = jnp.zeros_like(carry_ref)

    n = lf_ref.shape[1]
    cs = _dot3(lf_ref[0], _tri(n, upper=False), dot=lambda p, t: _dot(t, p)) + carry_ref[0:1, :]
    c_ref[0] = cs
    carry_ref[...] = jnp.broadcast_to(cs[n - 1:n, :], carry_ref.shape)


def _cumsum(lf):
    b, l, w = lf.shape
    tile = _row_tile(l, CS_TILE)
    spec = pl.BlockSpec((1, tile, w), lambda i, j: (i, j, 0))
    return pl.pallas_call(
        _cumsum_kernel,
        grid=(b, l // tile),
        in_specs=[spec],
        out_specs=spec,
        out_shape=jax.ShapeDtypeStruct((b, l, w), F32),
        scratch_shapes=[pltpu.VMEM((SUBLANES, w), F32)],
        compiler_params=_params("parallel", "arbitrary"),
        name="logf_cumsum",
    )(lf)


def _bias_lanes(c, head0, width, key_side):
    nh = width // HEAD
    src = lax.broadcasted_iota(jnp.int32, (LANES, width), 0)
    dst = lax.broadcasted_iota(jnp.int32, (LANES, width), 1)
    lane = lax.broadcasted_iota(jnp.int32, (1, width), 1)
    lo, sign = (3, -1.0) if key_side else (0, 1.0)
    base = [((hh + 1) % nh) * HEAD for hh in range(nh)]
    ones = functools.reduce(jnp.logical_or, [(lane >= bs + 3 - lo) & (lane < bs + 6 - lo) for bs in base])
    out = jnp.where(ones, 1.0, 0.0)
    for n, p in enumerate(_split3(c * LOG2E)):
        hit = functools.reduce(jnp.logical_or,
                               [(src == head0 + hh) & (dst == base[hh] + lo + n) for hh in range(nh)])
        out = out + _dot(p, jnp.where(hit, sign, 0.0).astype(BF16))
    return out


def _flash_kernel(q_ref, k_ref, vt_ref, c_ref, o_ref, kaug_scr, *, tq):
    g = pl.program_id(1)
    i = pl.program_id(2)
    t = k_ref.shape[1]
    width = k_ref.shape[2]
    nh = width // HEAD
    lane = lax.broadcasted_iota(jnp.int32, (1, width), 1)
    in_head = [(lane >> HEAD_SHIFT) == hh for hh in range(nh)]
    bias_lanes = [(lane >= ((hh + 1) % nh) * HEAD) & (lane < ((hh + 1) % nh) * HEAD + 6) for hh in range(nh)]

    def augment(x, bias):
        bias = bias.astype(BF16)
        return [jnp.where(in_head[hh], x, jnp.where(bias_lanes[hh], bias, jnp.zeros((), BF16)))
                for hh in range(nh)]

    @pl.when(i == 0)
    def _():
        for blk in range(t // tq):
            rows = slice(blk * tq, (blk + 1) * tq)
            ka = augment(k_ref[0, rows, :], _bias_lanes(c_ref[0, rows, :], nh * g, width, key_side=True))
            for hh in range(nh):
                kaug_scr[hh, rows, :] = ka[hh]

    row0 = pl.multiple_of(i * tq, tq)
    qa = augment(q_ref[0], _bias_lanes(c_ref[0, pl.ds(row0, tq), :], nh * g, width, key_side=False))

    qg = min(tq, SLAB)
    chains = [(hh, qs) for hh in range(nh) for qs in range(0, tq, qg)]
    nc = len(chains)
    qa_c = [qa[hh][qs:qs + qg] for hh, qs in chains]

    def step(off, carry, mask):
        ks = [kaug_scr[hh, pl.ds(off, tq), :] for hh in range(nh)]
        vts = [vt_ref[0, hh * HEAD:(hh + 1) * HEAD, pl.ds(off, tq)] for hh in range(nh)]
        kext = [tq if mask is None else qs + qg for _, qs in chains]
        st = [_dot_nt(ks[hh][:kext[n]], qa_c[n]) for n, (hh, _) in enumerate(chains)]
        if mask is not None:
            st = [jnp.where(mask[:kext[n], qs:qs + qg], st[n], -jnp.inf) for n, (_, qs) in enumerate(chains)]
        m_new = [jnp.maximum(carry[n][0], jnp.max(st[n], axis=0, keepdims=True)) for n in range(nc)]
        alpha = [jnp.exp2(carry[n][0] - m_new[n]) for n in range(nc)]
        p = [jnp.exp2(st[n] - m_new[n]) for n in range(nc)]
        l = [alpha[n] * carry[n][1] + jnp.sum(p[n], axis=0, keepdims=True) for n in range(nc)]
        acc = [alpha[n] * carry[n][2] + _dot(vts[hh][:, :kext[n]], p[n].astype(BF16))
               for n, (hh, _) in enumerate(chains)]
        return tuple(zip(m_new, l, acc))

    init = tuple((jnp.full((1, qg), -jnp.inf, F32), jnp.zeros((1, qg), F32), jnp.zeros((HEAD, qg), F32))
                 for _ in chains)
    carry = lax.fori_loop(0, i, lambda j, c: step(pl.multiple_of(j * tq, tq), c, None), init)
    causal = lax.broadcasted_iota(jnp.int32, (tq, tq), 0) <= lax.broadcasted_iota(jnp.int32, (tq, tq), 1)
    carry = step(row0, carry, causal)
    per_head = [jnp.concatenate([carry[n][2] / carry[n][1] for n, (h2, _) in enumerate(chains) if h2 == hh], axis=1)
                for hh in range(nh)]
    o_ref[0] = jnp.concatenate(per_head, axis=0).T


def _flash(q, k, vt, c, tq, width):
    b, t, d = q.shape
    tq = _row_tile(t, tq)
    return pl.pallas_call(
        functools.partial(_flash_kernel, tq=tq),
        grid=(b, d // width, t // tq),
        in_specs=[pl.BlockSpec((1, tq, width), lambda bi, gi, i: (bi, i, gi)),
                  pl.BlockSpec((1, t, width), lambda bi, gi, i: (bi, 0, gi)),
                  pl.BlockSpec((1, width, t), lambda bi, gi, i: (bi, gi, 0)),
                  pl.BlockSpec((1, t, LANES), lambda bi, gi, i: (bi, 0, 0))],
        out_specs=pl.BlockSpec((1, tq, width), lambda bi, gi, i: (bi, i, gi)),
        out_shape=jax.ShapeDtypeStruct((b, t, d), F32),
        scratch_shapes=[pltpu.VMEM((width // HEAD, t, width), BF16)],
        compiler_params=_params("parallel", "parallel", "arbitrary"),
        name="fox_flash",
    )(q, k, vt, c)


DEC_TILE = 512


def _decode_attn_kernel(q_ref, ckt_ref, cvt_ref, kn_ref, vn_ref, clf_ref, lfn_ref, o_ref, c_scr):
    g = pl.program_id(1)
    past = ckt_ref.shape[2]
    t = q_ref.shape[1]

    @pl.when(g == 0)
    def _():
        tri = _tri(DEC_TILE, upper=True)
        carry = jnp.zeros((clf_ref.shape[1], 1), F32)
        for blk in range(past // DEC_TILE):
            cols = slice(blk * DEC_TILE, (blk + 1) * DEC_TILE)
            cs = _dot3(clf_ref[0, :, cols], tri) + carry
            c_scr[:, cols] = cs
            carry = cs[:, DEC_TILE - 1:DEC_TILE]

    width = q_ref.shape[2]
    nh = width // HEAD
    lane = lax.broadcasted_iota(jnp.int32, (1, width), 1)
    hlane = lax.broadcasted_iota(jnp.int32, (1, LANES), 1)
    q = q_ref[0]
    kct = ckt_ref[0].astype(BF16)
    vct = cvt_ref[0].astype(BF16)
    kn = kn_ref[0]
    vn = vn_ref[0]
    causal = lax.broadcasted_iota(jnp.int32, (t, t), 1) <= lax.broadcasted_iota(jnp.int32, (t, t), 0)
    lf_new = _dot3(lfn_ref[0], _tri(t, upper=False), dot=lambda p, tr: _dot(tr, p))

    H = range(nh)
    head = [nh * g + hh for hh in H]
    in_head = [(lane >> HEAD_SHIFT) == hh for hh in H]
    rows = lambda parts: jnp.concatenate(parts, axis=0)
    qs = rows([jnp.where(in_head[hh], q, jnp.zeros((), BF16)) for hh in H])
    c_cache = [c_scr[pl.ds(head[hh], 1), :] for hh in H]
    total = [c_cache[hh][:, past - 1:past] for hh in H]
    onehot = [jnp.where(hlane == head[hh], 1.0, 0.0) for hh in H]
    cn_col = [jnp.sum(lf_new * onehot[hh], axis=-1, keepdims=True) + total[hh] for hh in H]
    cn_row = [_dot3(lf_new, jnp.broadcast_to(onehot[hh], (SUBLANES, LANES)).astype(BF16),
                    dot=lambda p, e: _dot_nt(e, p))[0:1] + total[hh] for hh in H]
    s1 = _dot(qs, kct) + rows([cn_col[hh] - c_cache[hh] for hh in H]) * LOG2E
    s2 = _dot_nt(qs, kn) + rows([cn_col[hh] - cn_row[hh] for hh in H]) * LOG2E
    s2 = jnp.where(rows([causal] * nh), s2, -jnp.inf)
    m = jnp.maximum(jnp.max(s1, axis=-1, keepdims=True), jnp.max(s2, axis=-1, keepdims=True))
    e1 = jnp.exp2(s1 - m)
    e2 = jnp.exp2(s2 - m)
    l = jnp.sum(e1, axis=-1, keepdims=True) + jnp.sum(e2, axis=-1, keepdims=True)
    o = (_dot_nt(e1.astype(BF16), vct) + _dot(e2.astype(BF16), vn)) / l
    out = o[(nh - 1) * t:]
    for hh in range(nh - 2, -1, -1):
        out = jnp.where(in_head[hh], o[hh * t:(hh + 1) * t], out)
    o_ref[0] = out


def _decode_attn(q, cache_kt, cache_vt, kn, vn, cache_lf, lf_new):
    b, t, d = q.shape
    p = cache_kt.shape[2]
    h = cache_lf.shape[1]
    assert p % DEC_TILE == 0
    new = pl.BlockSpec((1, t, SLAB), lambda bi, gi: (bi, 0, gi))
    old = pl.BlockSpec((1, SLAB, p), lambda bi, gi: (bi, gi, 0))
    return pl.pallas_call(
        _decode_attn_kernel,
        grid=(b, d // SLAB),
        in_specs=[new, old, old, new, new,
                  pl.BlockSpec((1, h, p), lambda bi, gi: (bi, 0, 0)),
                  pl.BlockSpec((1, t, LANES), lambda bi, gi: (bi, 0, 0))],
        out_specs=new,
        out_shape=jax.ShapeDtypeStruct((b, t, d), F32),
        scratch_shapes=[pltpu.VMEM((h, p), F32)],
        compiler_params=_params("parallel", "arbitrary"),
        name="fox_decode_attn",
    )(q, cache_kt, cache_vt, kn, vn, cache_lf, lf_new)


def _pad_cols(w, mult):
    pad = (-w.shape[-1]) % mult
    return jnp.pad(w, ((0, 0), (0, pad))) if pad else w


def _pad_rows(w, mult):
    pad = (-w.shape[0]) % mult
    return jnp.pad(w, ((0, pad), (0, 0))) if pad else w


def kernel(x_prompt, x_sample, state_wkv, state_shift, cache_k, cache_v, cache_logf, rwkv_mu, rwkv_w0, rwkv_w1, rwkv_w2, rwkv_a0, rwkv_a1, rwkv_a2, rwkv_g1, rwkv_g2, rwkv_k_k, rwkv_k_a, rwkv_r_k, rwkv_w_r, rwkv_w_k, rwkv_w_v, rwkv_w_o, rwkv_lnx_g, rwkv_lnx_b, fox_w_in, fox_b_f, fox_w_o, ffn_w1, ffn_w2, ln_mix_g, ln_mix_b, ln_ffn_g, ln_ffn_b):
    depth = ln_mix_g.shape[0]
    alpha = (2 * depth) ** 0.25
    bp, tp, d = x_prompt.shape
    bs, ts, _ = x_sample.shape
    heads = d // HEAD
    past = cache_k.shape[2]
    tm = 512

    xs_by_group = {"p": x_prompt, "s": x_sample}
    outs = {grp: {n: [] for n in ("wkv", "shift", "k", "v", "lf")} for grp in ("p", "s")}
    for i in range(depth):
        j = i // 2
        ln = jnp.stack([ln_mix_g[i], ln_mix_b[i], ln_ffn_g[i], ln_ffn_b[i]])
        w1 = ffn_w1[i].astype(BF16)
        w2 = ffn_w2[i].astype(BF16)
        if i % 2 == 0:
            row = lambda z: z.reshape(1, d)
            proj = (rwkv_mu[j], row(rwkv_w0[j]), row(rwkv_a0[j]),
                    rwkv_w_r[j].astype(BF16), rwkv_w_k[j].astype(BF16), rwkv_w_v[j].astype(BF16),
                    _pad_cols(rwkv_w1[j], LANES).astype(BF16), _pad_rows(rwkv_w2[j], LANES).astype(BF16),
                    _pad_cols(rwkv_a1[j], LANES).astype(BF16), _pad_rows(rwkv_a2[j], LANES).astype(BF16),
                    _pad_cols(rwkv_g1[j], LANES).astype(BF16), _pad_rows(rwkv_g2[j], LANES).astype(BF16))
            wkv_vecs = (row(rwkv_k_k[j]), row(rwkv_k_a[j]), row(rwkv_r_k[j]), row(rwkv_lnx_g[j]), row(rwkv_lnx_b[j]))
            wo = rwkv_w_o[j].astype(BF16)
            starts = {"p": (jnp.zeros((bp, d), x_prompt.dtype), jnp.zeros((bp, heads, HEAD, HEAD), state_wkv.dtype)),
                      "s": (state_shift[j], state_wkv[j])}
            for grp in ("p", "s"):
                x = xs_by_group[grp]
                b, t, _ = x.shape
                shift0, wkv0 = starts[grp]
                r, lw, k, v, a, gate = _rwkv_proj(x, shift0.reshape(b, 1, d), *proj, tm=tm)
                h0 = jnp.swapaxes(wkv0.astype(F32), -1, -2)
                o, hfin = _wkv(r, lw, k, v, a, *wkv_vecs, h0)
                outs[grp]["wkv"].append(jnp.swapaxes(hfin, -1, -2).astype(wkv0.dtype))
                outs[grp]["shift"].append(x[:, -1])
                xs_by_group[grp] = _post(o.reshape(b * t, d), gate.reshape(b * t, d), x.reshape(b * t, d),
                                         wo, w1, w2, ln, alpha, tm).reshape(b, t, d)
        else:
            w_in = fox_w_in[j]
            wq = w_in[:, :d].astype(BF16)
            wk = w_in[:, d:2 * d].astype(BF16)
            wv = w_in[:, 2 * d:3 * d].astype(BF16)
            wf = _pad_cols(w_in[:, 3 * d:], LANES).astype(BF16)
            bf = _pad_cols(fox_b_f[j].reshape(1, heads), LANES)
            wo = fox_w_o[j].astype(BF16)

            xp = xs_by_group["p"]
            q, kt, vt, kb, vtb, lf = _fox_proj(xp, wq, wk, wv.T, wf, bf, tm, channel_major=True)
            o = _flash(q, kb, vtb, _cumsum(lf), tq=512, width=SLAB)
            to_heads = lambda z: jnp.transpose(z.reshape(bp, heads, HEAD, tp), (0, 3, 1, 2))
            outs["p"]["k"].append(to_heads(kt))
            outs["p"]["v"].append(to_heads(vt))
            outs["p"]["lf"].append(lf[:, :, :heads])
            xs_by_group["p"] = _post(o.reshape(bp * tp, d), None, xp.reshape(bp * tp, d),
                                     wo, w1, w2, ln, alpha, tm).reshape(bp, tp, d)

            xs = xs_by_group["s"]
            q, k, v, kb, vb, lf = _fox_proj(xs, wq, wk, wv, wf, bf, tm, channel_major=False)
            channel_major = lambda z: jnp.transpose(z, (0, 2, 3, 1)).reshape(bs, d, past)
            o = _decode_attn(q, channel_major(cache_k[j]), channel_major(cache_v[j]), kb, vb,
                             jnp.transpose(cache_logf[j].astype(F32), (0, 2, 1)), lf)
            outs["s"]["k"].append(k.reshape(bs, ts, heads, HEAD))
            outs["s"]["v"].append(v.reshape(bs, ts, heads, HEAD))
            outs["s"]["lf"].append(lf[:, :, :heads])
            xs_by_group["s"] = _post(o.reshape(bs * ts, d), None, xs.reshape(bs * ts, d),
                                     wo, w1, w2, ln, alpha, tm).reshape(bs, ts, d)

    st = jnp.stack
    op, os_ = outs["p"], outs["s"]
    return (xs_by_group["p"], xs_by_group["s"],
            st(op["wkv"]), st(op["shift"]), st(op["k"]), st(op["v"]), st(op["lf"]),
            st(os_["wkv"]), st(os_["shift"]), st(os_["k"]), st(os_["v"]), st(os_["lf"]))
```

```python
import functools
import math

import jax
import jax.numpy as jnp
from jax import lax
from jax.experimental import pallas as pl
from jax.experimental.pallas import tpu as pltpu

BF16 = jnp.bfloat16
F32 = jnp.float32

HEAD = 64
HEAD_SHIFT = 6
LANES = 128
SUBLANES = 8
SLAB = 256
HEADS_PER_SLAB = SLAB // HEAD
CHUNK = 64
WKV_CHUNKS_PER_STEP = 2
LN_EPS = 1e-5
GN_EPS = 64e-5
EXP_M05 = math.exp(-0.5)
LOG2E = math.log2(math.e)
VMEM_LIMIT = 56 * 1024 * 1024


def _dot(a, b):
    return jnp.dot(a, b, preferred_element_type=F32)


def _dot_nt(a, b):
    return lax.dot_general(a, b, (((1,), (1,)), ((), ())), preferred_element_type=F32)


def _dot_tn(a, b):
    return lax.dot_general(a, b, (((0,), (0,)), ((), ())), preferred_element_type=F32)


def _split3(x):
    p1 = x.astype(BF16)
    r1 = x - p1.astype(F32)
    p2 = r1.astype(BF16)
    p3 = (r1 - p2.astype(F32)).astype(BF16)
    return p1, p2, p3


def _dot3(a, b, dot=_dot):
    p1, p2, p3 = _split3(a)
    return dot(p1, b) + dot(p2, b) + dot(p3, b)


def _tri(n, upper):
    r = lax.broadcasted_iota(jnp.int32, (n, n), 0)
    c = lax.broadcasted_iota(jnp.int32, (n, n), 1)
    return jnp.where((r <= c) if upper else (c <= r), 1.0, 0.0).astype(BF16)


def _layer_norm(z, g, b):
    mu = jnp.mean(z, axis=-1, keepdims=True)
    zc = z - mu
    var = jnp.mean(zc * zc, axis=-1, keepdims=True)
    return zc * lax.rsqrt(var + LN_EPS) * g + b


def _params(*sem):
    return pltpu.CompilerParams(dimension_semantics=sem, vmem_limit_bytes=VMEM_LIMIT)


def _const_spec(shape):
    nd = len(shape)
    return pl.BlockSpec(shape, lambda *_: (0,) * nd, pipeline_mode=pl.Buffered(1))


def _row_tile(n, want):
    t = min(n, want)
    assert n % t == 0 and t % SUBLANES == 0, (n, t)
    return t


def _rwkv_proj_kernel(x_ref, prev_ref, first_ref, mu_ref, w0_ref, a0_ref, wr_ref, wk_ref, wv_ref,
                      w1_ref, w2_ref, a1_ref, a2_ref, g1_ref, g2_ref,
                      r_ref, lw_ref, k_ref, v_ref, a_ref, g_ref):
    x = x_ref[0]
    row0 = jnp.where(pl.program_id(1) == 0, first_ref[0], prev_ref[0, SUBLANES - 1:SUBLANES, :])
    is_row0 = lax.broadcasted_iota(jnp.int32, x.shape, 0) == 0
    xx = jnp.where(is_row0, row0, pltpu.roll(x, 1, 0)) - x

    def mix(i):
        return (x + xx * mu_ref[i:i + 1, :]).astype(BF16)

    r_ref[0] = _dot(mix(0), wr_ref[...])
    w_raw = w0_ref[...] + _dot(jnp.tanh(_dot(mix(1), w1_ref[...])).astype(BF16), w2_ref[...])
    lw_ref[0] = -EXP_M05 * jax.nn.sigmoid(w_raw)
    k_ref[0] = _dot(mix(2), wk_ref[...])
    v_ref[0] = _dot(mix(3), wv_ref[...])
    a_ref[0] = jax.nn.sigmoid(a0_ref[...] + _dot(_dot(mix(4), a1_ref[...]).astype(BF16), a2_ref[...]))
    g_ref[0] = _dot(jax.nn.sigmoid(_dot(mix(5), g1_ref[...])).astype(BF16), g2_ref[...])


def _rwkv_proj(x, shift0, mu, w0, a0, wr, wk, wv, w1, w2, a1, a2, g1, g2, tm):
    b, t, d = x.shape
    tm = _row_tile(t, tm)
    per_tile = tm // SUBLANES
    row = pl.BlockSpec((1, tm, d), lambda bi, i: (bi, i, 0))
    prev = pl.BlockSpec((1, SUBLANES, d), lambda bi, i: (bi, jnp.maximum(i * per_tile - 1, 0), 0))
    first = pl.BlockSpec((1, 1, d), lambda bi, i: (bi, 0, 0))
    consts = [mu, w0, a0, wr, wk, wv, w1, w2, a1, a2, g1, g2]
    return pl.pallas_call(
        _rwkv_proj_kernel,
        grid=(b, t // tm),
        in_specs=[row, prev, first] + [_const_spec(c.shape) for c in consts],
        out_specs=[row] * 6,
        out_shape=[jax.ShapeDtypeStruct((b, t, d), F32)] * 6,
        compiler_params=_params("parallel", "parallel"),
        name="rwkv_proj",
    )(x, x, shift0, *consts)


def _wkv_kernel(r_ref, lw_ref, k_ref, v_ref, a_ref, kk_ref, ka_ref, rk_ref, lg_ref, lb_ref, h0_ref,
                o_ref, hout_ref, h_scr, *, last_chunk):
    c = pl.program_id(1)
    nslab = h_scr.shape[0]
    C = CHUNK
    rows = r_ref.shape[1]

    @pl.when(c == 0)
    def _():
        h_scr[...] = jnp.zeros_like(h_scr)
        for s in range(nslab):
            for h in range(HEADS_PER_SLAB):
                hs = slice(h * HEAD, (h + 1) * HEAD)
                h_scr[s, hs, hs] = h0_ref[0, s * HEADS_PER_SLAB + h]

    row_s = lax.broadcasted_iota(jnp.int32, (SLAB, SLAB), 0)
    col_s = lax.broadcasted_iota(jnp.int32, (SLAB, SLAB), 1)
    mask_bd = (row_s >> HEAD_SHIFT) == (col_s >> HEAD_SHIFT)
    eye_s = row_s == col_s
    row_c = lax.broadcasted_iota(jnp.int32, (C, SLAB), 0)
    colin_c = lax.broadcasted_iota(jnp.int32, (C, SLAB), 1) & (C - 1)
    strict_lo = colin_c < row_c
    incl_lo = colin_c <= row_c
    eye_c = colin_c == row_c
    ones_bd = jnp.where(mask_bd, 1.0, 0.0).astype(BF16)

    def bd(x):
        xb = x.astype(BF16)
        return jnp.where(mask_bd, jnp.concatenate([xb] * HEADS_PER_SLAB, axis=0), jnp.zeros((), BF16))

    def head_sum(x):
        return _dot(x.astype(BF16), ones_bd)

    nck = max(rows // C, 1)
    live = min(rows, C)

    def chunk_rows(ref, ci, sl):
        x = ref[0, ci * C:ci * C + live, sl]
        if live < C:
            x = jnp.concatenate([x, jnp.zeros((C - live, x.shape[1]), x.dtype)], axis=0)
        return x

    lw_c = [chunk_rows(lw_ref, ci, slice(None)) for ci in range(nck)]
    L_c = [_dot3(lw_c[ci], _tri(C, upper=False), dot=lambda p, t: _dot(t, p)) for ci in range(nck)]

    units = [(ci, s) for ci in range(nck) for s in range(nslab)]
    S = range(len(units))
    slabs = [slice(s * SLAB, (s + 1) * SLAB) for _, s in units]
    r = [chunk_rows(r_ref, ci, slabs[u]) for u, (ci, _) in enumerate(units)]
    v = [chunk_rows(v_ref, ci, slabs[u]) for u, (ci, _) in enumerate(units)]
    asig = [chunk_rows(a_ref, ci, slabs[u]) for u, (ci, _) in enumerate(units)]
    k0 = [chunk_rows(k_ref, ci, slabs[u]) for u, (ci, _) in enumerate(units)]
    lw = [lw_c[ci][:, slabs[u]] for u, (ci, _) in enumerate(units)]
    kk = [k0[s] * kk_ref[:, slabs[s]] for s in S]
    k = [k0[s] * (1.0 + (asig[s] - 1.0) * ka_ref[:, slabs[s]]) for s in S]
    sums = [head_sum(jnp.concatenate([kk[s] * kk[s], r[s] * k[s] * rk_ref[:, slabs[s]]], axis=0)) for s in S]
    kk = [kk[s] / jnp.maximum(jnp.sqrt(sums[s][:C]), 1e-12) for s in S]
    bonus = [sums[s][C:] * v[s] for s in S]
    bv = [kk[s] * asig[s] for s in S]
    L = [L_c[ci][:, slabs[u]] for u, (ci, _) in enumerate(units)]
    LC = [L[s][C - 1:C, :] for s in S]
    enL = [jnp.exp(-L[s]) for s in S]
    eLc = [jnp.exp(LC[s] - L[s]) for s in S]
    At = [-kk[s] * jnp.exp(L[s] - lw[s]) for s in S]
    Rt = [r[s] * jnp.exp(L[s]) for s in S]
    Bt = [bv[s] * enL[s] for s in S]
    Kt = [k[s] * enL[s] for s in S]
    Bh = [bv[s] * eLc[s] for s in S]
    Kh = [k[s] * eLc[s] for s in S]

    lhs = [jnp.concatenate([At[s], Rt[s]], axis=0).astype(BF16) for s in S]
    ab = [_dot_nt(lhs[s], bd(Bt[s])) for s in S]
    ak = [_dot_nt(lhs[s], bd(Kt[s])) for s in S]
    N = [jnp.where(strict_lo, ab[s][:C], 0.0) for s in S]
    M = [jnp.where(strict_lo, ak[s][:C], 0.0) for s in S]
    Arb = [jnp.where(incl_lo, ab[s][C:], 0.0) for s in S]
    Ark = [jnp.where(incl_lo, ak[s][C:], 0.0) for s in S]
    MV = [_dot(M[s].astype(BF16), bd(v[s])) for s in S]

    X = [jnp.where(eye_c, 1.0, 0.0) + N[s] for s in S]
    Pw = N
    for _ in range(int(math.log2(C)) - 1):
        Pw = [_dot(Pw[s].astype(BF16), bd(Pw[s])) for s in S]
        X = [X[s] + _dot(X[s].astype(BF16), bd(Pw[s])) for s in S]

    wu = [_dot(X[s].astype(BF16), jnp.concatenate([bd(At[s]), bd(MV[s])], axis=1)) for s in S]
    WA = [wu[s][:, :SLAB] for s in S]
    UV = [wu[s][:, SLAB:] for s in S]
    Yl = [_dot(jnp.concatenate([Arb[s], Ark[s]], axis=1).astype(BF16),
               jnp.concatenate([bd(UV[s]), bd(v[s])], axis=0)) for s in S]
    Rp = [Rt[s] + _dot(Arb[s].astype(BF16), bd(WA[s])) for s in S]
    rw = [jnp.concatenate([Rp[s], WA[s]], axis=0).astype(BF16) for s in S]
    bk = [jnp.concatenate([Bh[s], Kh[s]], axis=0).astype(BF16) for s in S]
    pc_col = [jnp.sum(jnp.where(eye_s, jnp.exp(LC[s]), 0.0), axis=1, keepdims=True) for s in S]

    h = [h_scr[s] for s in range(nslab)]
    y = [None] * len(units)
    for ci in range(nck):
        us = [ci * nslab + s for s in range(nslab)]
        yu = [_dot(rw[u], h[s].astype(BF16)) for s, u in enumerate(us)]
        for s, u in enumerate(us):
            y[u] = yu[s][:C] + Yl[u]
        uv = [jnp.concatenate([yu[s][C:] + UV[u], v[u]], axis=0).astype(BF16) for s, u in enumerate(us)]
        h = [h[s] * pc_col[u] + jnp.where(mask_bd, _dot_tn(bk[u], uv[s]), 0.0) for s, u in enumerate(us)]
    for s in range(nslab):
        h_scr[s] = h[s]

    mu = [head_sum(y[s]) * (1.0 / HEAD) for s in S]
    yc = [y[s] - mu[s] for s in S]
    var = [head_sum(yc[s] * yc[s]) * (1.0 / HEAD) for s in S]
    for u, (ci, _) in enumerate(units):
        sl = slabs[u]
        out = yc[u] * lax.rsqrt(var[u] + GN_EPS) * lg_ref[:, sl] + lb_ref[:, sl] + bonus[u]
        o_ref[0, ci * C:ci * C + live, sl] = out[:live]

    @pl.when(c == last_chunk)
    def _():
        for s in range(nslab):
            for h in range(HEADS_PER_SLAB):
                hs = slice(h * HEAD, (h + 1) * HEAD)
                hout_ref[0, s * HEADS_PER_SLAB + h] = h_scr[s, hs, hs]


def _wkv(r, lw, k, v, a, k_k, k_a, r_k, lnx_g, lnx_b, h0):
    b, t, d = r.shape
    rows = WKV_CHUNKS_PER_STEP * CHUNK if t % (WKV_CHUNKS_PER_STEP * CHUNK) == 0 else min(t, CHUNK)
    assert t % rows == 0 and rows % SUBLANES == 0
    nslab = d // SLAB
    heads = d // HEAD
    seq = pl.BlockSpec((1, rows, d), lambda i, c: (i, c, 0))
    vec = pl.BlockSpec((1, d), lambda i, c: (0, 0))
    hspec = pl.BlockSpec((1, heads, HEAD, HEAD), lambda i, c: (i, 0, 0, 0))
    return pl.pallas_call(
        functools.partial(_wkv_kernel, last_chunk=t // rows - 1),
        grid=(b, t // rows),
        in_specs=[seq] * 5 + [vec] * 5 + [hspec],
        out_specs=[seq, hspec],
        out_shape=[jax.ShapeDtypeStruct((b, t, d), F32),
                   jax.ShapeDtypeStruct((b, heads, HEAD, HEAD), F32)],
        scratch_shapes=[pltpu.VMEM((nslab, SLAB, SLAB), F32)],
        compiler_params=_params("parallel", "arbitrary"),
        name="wkv_chunked",
    )(r, lw, k, v, a, k_k, k_a, r_k, lnx_g, lnx_b, h0)


def _post_kernel(*refs, alpha, gated, ff_chunk):
    if gated:
        pre_ref, gate_ref, x_ref, wo_ref, w1_ref, w2_ref, ln_ref, out_ref = refs
        pre = pre_ref[...] * gate_ref[...]
    else:
        pre_ref, x_ref, wo_ref, w1_ref, w2_ref, ln_ref, out_ref = refs
        pre = pre_ref[...]
    h = _dot(pre.astype(BF16), wo_ref[...])
    x1 = _layer_norm(alpha * x_ref[...] + h, ln_ref[0:1, :], ln_ref[1:2, :])
    x1b = x1.astype(BF16)
    m = jnp.zeros_like(x1)
    for f in range(w1_ref.shape[1] // ff_chunk):
        fs = slice(f * ff_chunk, (f + 1) * ff_chunk)
        hid = jnp.maximum(_dot(x1b, w1_ref[:, fs]), 0.0)
        m = m + _dot((hid * hid).astype(BF16), w2_ref[fs, :])
    out_ref[...] = _layer_norm(alpha * x1 + m, ln_ref[2:3, :], ln_ref[3:4, :])


def _post(pre, gate, x, wo, w1, w2, ln, alpha, tm):
    n, d = x.shape
    tm = _row_tile(n, tm)
    row = pl.BlockSpec((tm, d), lambda i: (i, 0))
    acts = [pre, x] if gate is None else [pre, gate, x]
    consts = [wo, w1, w2, ln]
    return pl.pallas_call(
        functools.partial(_post_kernel, alpha=alpha, gated=gate is not None, ff_chunk=min(1024, w1.shape[1])),
        grid=(n // tm,),
        in_specs=[row] * len(acts) + [_const_spec(c.shape) for c in consts],
        out_specs=row,
        out_shape=jax.ShapeDtypeStruct((n, d), F32),
        compiler_params=_params("parallel"),
        name="post_mlp",
    )(*acts, *consts)


def _fox_proj_kernel(x_ref, wq_ref, wk_ref, wv_ref, wf_ref, bf_ref,
                     q_ref, k_ref, v_ref, kb_ref, vb_ref, lf_ref, *, channel_major):
    xb = x_ref[0].astype(BF16)
    q_ref[0] = (_dot(xb, wq_ref[...]) * (HEAD ** -0.5 * LOG2E)).astype(BF16)
    k = _dot(xb, wk_ref[...])
    kb_ref[0] = k.astype(BF16)
    if channel_major:
        k_ref[0] = k.T
        v = _dot_nt(wv_ref[...], xb)
    else:
        k_ref[0] = k
        v = _dot(xb, wv_ref[...])
    v_ref[0] = v
    vb_ref[0] = v.astype(BF16)
    f = _dot(xb, wf_ref[...]) + bf_ref[...]
    lf_ref[0] = jnp.minimum(f, 0.0) - jnp.log1p(jnp.exp(-jnp.abs(f)))


def _fox_proj(x, wq, wk, wv, wf, bf, tm, channel_major):
    b, t, d = x.shape
    tm = _row_tile(t, tm)
    row = pl.BlockSpec((1, tm, d), lambda bi, i: (bi, i, 0))
    rowf = pl.BlockSpec((1, tm, LANES), lambda bi, i: (bi, i, 0))
    kv_spec = pl.BlockSpec((1, d, tm), lambda bi, i: (bi, 0, i)) if channel_major else row
    kv_shape = jax.ShapeDtypeStruct((b, d, t) if channel_major else (b, t, d), F32)
    act_bf = jax.ShapeDtypeStruct((b, t, d), BF16)
    vb_shape = jax.ShapeDtypeStruct(kv_shape.shape, BF16)
    consts = [wq, wk, wv, wf, bf]
    return pl.pallas_call(
        functools.partial(_fox_proj_kernel, channel_major=channel_major),
        grid=(b, t // tm),
        in_specs=[row] + [_const_spec(c.shape) for c in consts],
        out_specs=[row, kv_spec, kv_spec, row, kv_spec, rowf],
        out_shape=[act_bf, kv_shape, kv_shape, act_bf, vb_shape, jax.ShapeDtypeStruct((b, t, LANES), F32)],
        compiler_params=_params("parallel", "parallel"),
        name="fox_proj",
    )(x, *consts)


CS_TILE = 512
SUM_ROWS = 16


def _cumsum_kernel(lf_ref, c_ref, carry_ref):
    @pl.when(pl.program_id(1) == 0)
    def _():
        carry_ref[...] = jnp.zeros_like(carry_ref)

    n = lf_ref.shape[1]
    cs = _dot3(lf_ref[0], _tri(n, upper=False), dot=lambda p, t: _dot(t, p)) + carry_ref[0:1, :]
    c_ref[0] = cs
    carry_ref[...] = jnp.broadcast_to(cs[n - 1:n, :], carry_ref.shape)


def _cumsum(lf):
    b, l, w = lf.shape
    tile = _row_tile(l, CS_TILE)
    spec = pl.BlockSpec((1, tile, w), lambda i, j: (i, j, 0))
    return pl.pallas_call(
        _cumsum_kernel,
        grid=(b, l // tile),
        in_specs=[spec],
        out_specs=spec,
        out_shape=jax.ShapeDtypeStruct((b, l, w), F32),
        scratch_shapes=[pltpu.VMEM((SUBLANES, w), F32)],
        compiler_params=_params("parallel", "arbitrary"),
        name="logf_cumsum",
    )(lf)


def _bias_lanes(c, head0, width, key_side):
    nh = width // HEAD
    src = lax.broadcasted_iota(jnp.int32, (LANES, width), 0)
    dst = lax.broadcasted_iota(jnp.int32, (LANES, width), 1)
    lane = lax.broadcasted_iota(jnp.int32, (1, width), 1)
    lo, sign = (3, -1.0) if key_side else (0, 1.0)
    base = [((hh + 1) % nh) * HEAD for hh in range(nh)]
    ones = functools.reduce(jnp.logical_or, [(lane >= bs + 3 - lo) & (lane < bs + 6 - lo) for bs in base])
    out = jnp.where(ones, 1.0, 0.0)
    for n, p in enumerate(_split3(c * LOG2E)):
        hit = functools.reduce(jnp.logical_or,
                               [(src == head0 + hh) & (dst == base[hh] + lo + n) for hh in range(nh)])
        out = out + _dot(p, jnp.where(hit, sign, 0.0).astype(BF16))
    return out


def _flash_kernel(q_ref, k_ref, vt_ref, c_ref, o_ref, kaug_scr, *, tq):
    g = pl.program_id(1)
    i = pl.program_id(2)
    t = k_ref.shape[1]
    width = k_ref.shape[2]
    nh = width // HEAD
    lane = lax.broadcasted_iota(jnp.int32, (1, width), 1)
    in_head = [(lane >> HEAD_SHIFT) == hh for hh in range(nh)]
    bias_lanes = [(lane >= ((hh + 1) % nh) * HEAD) & (lane < ((hh + 1) % nh) * HEAD + 6) for hh in range(nh)]

    def augment(x, bias):
        bias = bias.astype(BF16)
        return [jnp.where(in_head[hh], x, jnp.where(bias_lanes[hh], bias, jnp.zeros((), BF16)))
                for hh in range(nh)]

    @pl.when(i == 0)
    def _():
        for blk in range(t // tq):
            rows = slice(blk * tq, (blk + 1) * tq)
            ka = augment(k_ref[0, rows, :], _bias_lanes(c_ref[0, rows, :], nh * g, width, key_side=True))
            for hh in range(nh):
                kaug_scr[hh, rows, :] = ka[hh]

    row0 = pl.multiple_of(i * tq, tq)
    qa = augment(q_ref[0], _bias_lanes(c_ref[0, pl.ds(row0, tq), :], nh * g, width, key_side=False))

    qg = min(tq, SLAB)
    chains = [(hh, qs) for hh in range(nh) for qs in range(0, tq, qg)]
    nc = len(chains)
    qa_c = [qa[hh][qs:qs + qg] for hh, qs in chains]

    ones_rows = jnp.ones((SUM_ROWS, tq), BF16)

    def step(off, carry, mask):
        ks = [kaug_scr[hh, pl.ds(off, tq), :] for hh in range(nh)]
        vts = [jnp.concatenate([vt_ref[0, hh * HEAD:(hh + 1) * HEAD, pl.ds(off, tq)], ones_rows], axis=0)
               for hh in range(nh)]
        kext = [tq if mask is None else qs + qg for _, qs in chains]
        st = [_dot_nt(ks[hh][:kext[n]], qa_c[n]) for n, (hh, _) in enumerate(chains)]
        if mask is not None:
            st = [jnp.where(mask[:kext[n], qs:qs + qg], st[n], -jnp.inf) for n, (_, qs) in enumerate(chains)]
        m_new = [jnp.maximum(carry[n][0], jnp.max(st[n], axis=0, keepdims=True)) for n in range(nc)]
        alpha = [jnp.exp2(carry[n][0] - m_new[n]) for n in range(nc)]
        p = [jnp.exp2(st[n] - m_new[n]) for n in range(nc)]
        acc = [alpha[n] * carry[n][1] + _dot(vts[hh][:, :kext[n]], p[n].astype(BF16))
               for n, (hh, _) in enumerate(chains)]
        return tuple(zip(m_new, acc))

    init = tuple((jnp.full((1, qg), -jnp.inf, F32), jnp.zeros((HEAD + SUM_ROWS, qg), F32)) for _ in chains)
    carry = lax.fori_loop(0, i, lambda j, c: step(pl.multiple_of(j * tq, tq), c, None), init)
    causal = lax.broadcasted_iota(jnp.int32, (tq, tq), 0) <= lax.broadcasted_iota(jnp.int32, (tq, tq), 1)
    carry = step(row0, carry, causal)
    per_head = [jnp.concatenate([carry[n][1][:HEAD] / carry[n][1][HEAD:HEAD + 1]
                                 for n, (h2, _) in enumerate(chains) if h2 == hh], axis=1) for hh in range(nh)]
    o_ref[0] = jnp.concatenate(per_head, axis=0).T


def _flash(q, k, vt, c, tq, width):
    b, t, d = q.shape
    tq = _row_tile(t, tq)
    return pl.pallas_call(
        functools.partial(_flash_kernel, tq=tq),
        grid=(b, d // width, t // tq),
        in_specs=[pl.BlockSpec((1, tq, width), lambda bi, gi, i: (bi, i, gi)),
                  pl.BlockSpec((1, t, width), lambda bi, gi, i: (bi, 0, gi)),
                  pl.BlockSpec((1, width, t), lambda bi, gi, i: (bi, gi, 0)),
                  pl.BlockSpec((1, t, LANES), lambda bi, gi, i: (bi, 0, 0))],
        out_specs=pl.BlockSpec((1, tq, width), lambda bi, gi, i: (bi, i, gi)),
        out_shape=jax.ShapeDtypeStruct((b, t, d), F32),
        scratch_shapes=[pltpu.VMEM((width // HEAD, t, width), BF16)],
        compiler_params=_params("parallel", "parallel", "arbitrary"),
        name="fox_flash",
    )(q, k, vt, c)


DEC_TILE = 512


def _decode_attn_kernel(q_ref, ckt_ref, cvt_ref, kn_ref, vn_ref, clf_ref, lfn_ref, o_ref, c_scr):
    g = pl.program_id(1)
    past = ckt_ref.shape[2]
    t = q_ref.shape[1]

    @pl.when(g == 0)
    def _():
        tri = _tri(DEC_TILE, upper=True)
        carry = jnp.zeros((clf_ref.shape[1], 1), F32)
        for blk in range(past // DEC_TILE):
            cols = slice(blk * DEC_TILE, (blk + 1) * DEC_TILE)
            cs = _dot3(clf_ref[0, :, cols], tri) + carry
            c_scr[:, cols] = cs
            carry = cs[:, DEC_TILE - 1:DEC_TILE]

    width = q_ref.shape[2]
    nh = width // HEAD
    lane = lax.broadcasted_iota(jnp.int32, (1, width), 1)
    hlane = lax.broadcasted_iota(jnp.int32, (1, LANES), 1)
    q = q_ref[0]
    kct = ckt_ref[0].astype(BF16)
    vct = cvt_ref[0].astype(BF16)
    kn = kn_ref[0]
    vn = vn_ref[0]
    causal = lax.broadcasted_iota(jnp.int32, (t, t), 1) <= lax.broadcasted_iota(jnp.int32, (t, t), 0)
    lf_new = _dot3(lfn_ref[0], _tri(t, upper=False), dot=lambda p, tr: _dot(tr, p))

    H = range(nh)
    head = [nh * g + hh for hh in H]
    in_head = [(lane >> HEAD_SHIFT) == hh for hh in H]
    rows = lambda parts: jnp.concatenate(parts, axis=0)
    qs = rows([jnp.where(in_head[hh], q, jnp.zeros((), BF16)) for hh in H])
    c_cache = [c_scr[pl.ds(head[hh], 1), :] for hh in H]
    total = [c_cache[hh][:, past - 1:past] for hh in H]
    onehot = [jnp.where(hlane == head[hh], 1.0, 0.0) for hh in H]
    cn_col = [jnp.sum(lf_new * onehot[hh], axis=-1, keepdims=True) + total[hh] for hh in H]
    cn_row = [_dot3(lf_new, jnp.broadcast_to(onehot[hh], (SUBLANES, LANES)).astype(BF16),
                    dot=lambda p, e: _dot_nt(e, p))[0:1] + total[hh] for hh in H]
    s1 = _dot(qs, kct) + rows([cn_col[hh] - c_cache[hh] for hh in H]) * LOG2E
    s2 = _dot_nt(qs, kn) + rows([cn_col[hh] - cn_row[hh] for hh in H]) * LOG2E
    s2 = jnp.where(rows([causal] * nh), s2, -jnp.inf)
    m = jnp.maximum(jnp.max(s1, axis=-1, keepdims=True), jnp.max(s2, axis=-1, keepdims=True))
    e1 = jnp.exp2(s1 - m)
    e2 = jnp.exp2(s2 - m)
    l = jnp.sum(e1, axis=-1, keepdims=True) + jnp.sum(e2, axis=-1, keepdims=True)
    o = (_dot_nt(e1.astype(BF16), vct) + _dot(e2.astype(BF16), vn)) / l
    out = o[(nh - 1) * t:]
    for hh in range(nh - 2, -1, -1):
        out = jnp.where(in_head[hh], o[hh * t:(hh + 1) * t], out)
    o_ref[0] = out


def _decode_attn(q, cache_kt, cache_vt, kn, vn, cache_lf, lf_new):
    b, t, d = q.shape
    p = cache_kt.shape[2]
    h = cache_lf.shape[1]
    assert p % DEC_TILE == 0
    new = pl.BlockSpec((1, t, SLAB), lambda bi, gi: (bi, 0, gi))
    old = pl.BlockSpec((1, SLAB, p), lambda bi, gi: (bi, gi, 0))
    return pl.pallas_call(
        _decode_attn_kernel,
        grid=(b, d // SLAB),
        in_specs=[new, old, old, new, new,
                  pl.BlockSpec((1, h, p), lambda bi, gi: (bi, 0, 0)),
                  pl.BlockSpec((1, t, LANES), lambda bi, gi: (bi, 0, 0))],
        out_specs=new,
        out_shape=jax.ShapeDtypeStruct((b, t, d), F32),
        scratch_shapes=[pltpu.VMEM((h, p), F32)],
        compiler_params=_params("parallel", "arbitrary"),
        name="fox_decode_attn",
    )(q, cache_kt, cache_vt, kn, vn, cache_lf, lf_new)


def _pad_cols(w, mult):
    pad = (-w.shape[-1]) % mult
    return jnp.pad(w, ((0, 0), (0, pad))) if pad else w


def _pad_rows(w, mult):
    pad = (-w.shape[0]) % mult
    return jnp.pad(w, ((0, pad), (0, 0))) if pad else w


def kernel(x_prompt, x_sample, state_wkv, state_shift, cache_k, cache_v, cache_logf, rwkv_mu, rwkv_w0, rwkv_w1, rwkv_w2, rwkv_a0, rwkv_a1, rwkv_a2, rwkv_g1, rwkv_g2, rwkv_k_k, rwkv_k_a, rwkv_r_k, rwkv_w_r, rwkv_w_k, rwkv_w_v, rwkv_w_o, rwkv_lnx_g, rwkv_lnx_b, fox_w_in, fox_b_f, fox_w_o, ffn_w1, ffn_w2, ln_mix_g, ln_mix_b, ln_ffn_g, ln_ffn_b):
    depth = ln_mix_g.shape[0]
    alpha = (2 * depth) ** 0.25
    bp, tp, d = x_prompt.shape
    bs, ts, _ = x_sample.shape
    heads = d // HEAD
    past = cache_k.shape[2]
    tm = 512

    xs_by_group = {"p": x_prompt, "s": x_sample}
    outs = {grp: {n: [] for n in ("wkv", "shift", "k", "v", "lf")} for grp in ("p", "s")}
    for i in range(depth):
        j = i // 2
        ln = jnp.stack([ln_mix_g[i], ln_mix_b[i], ln_ffn_g[i], ln_ffn_b[i]])
        w1 = ffn_w1[i].astype(BF16)
        w2 = ffn_w2[i].astype(BF16)
        if i % 2 == 0:
            row = lambda z: z.reshape(1, d)
            proj = (rwkv_mu[j], row(rwkv_w0[j]), row(rwkv_a0[j]),
                    rwkv_w_r[j].astype(BF16), rwkv_w_k[j].astype(BF16), rwkv_w_v[j].astype(BF16),
                    _pad_cols(rwkv_w1[j], LANES).astype(BF16), _pad_rows(rwkv_w2[j], LANES).astype(BF16),
                    _pad_cols(rwkv_a1[j], LANES).astype(BF16), _pad_rows(rwkv_a2[j], LANES).astype(BF16),
                    _pad_cols(rwkv_g1[j], LANES).astype(BF16), _pad_rows(rwkv_g2[j], LANES).astype(BF16))
            wkv_vecs = (row(rwkv_k_k[j]), row(rwkv_k_a[j]), row(rwkv_r_k[j]), row(rwkv_lnx_g[j]), row(rwkv_lnx_b[j]))
            wo = rwkv_w_o[j].astype(BF16)
            starts = {"p": (jnp.zeros((bp, d), x_prompt.dtype), jnp.zeros((bp, heads, HEAD, HEAD), state_wkv.dtype)),
                      "s": (state_shift[j], state_wkv[j])}
            for grp in ("p", "s"):
                x = xs_by_group[grp]
                b, t, _ = x.shape
                shift0, wkv0 = starts[grp]
                r, lw, k, v, a, gate = _rwkv_proj(x, shift0.reshape(b, 1, d), *proj, tm=tm)
                h0 = jnp.swapaxes(wkv0.astype(F32), -1, -2)
                o, hfin = _wkv(r, lw, k, v, a, *wkv_vecs, h0)
                outs[grp]["wkv"].append(jnp.swapaxes(hfin, -1, -2).astype(wkv0.dtype))
                outs[grp]["shift"].append(x[:, -1])
                xs_by_group[grp] = _post(o.reshape(b * t, d), gate.reshape(b * t, d), x.reshape(b * t, d),
                                         wo, w1, w2, ln, alpha, tm).reshape(b, t, d)
        else:
            w_in = fox_w_in[j]
            wq = w_in[:, :d].astype(BF16)
            wk = w_in[:, d:2 * d].astype(BF16)
            wv = w_in[:, 2 * d:3 * d].astype(BF16)
            wf = _pad_cols(w_in[:, 3 * d:], LANES).astype(BF16)
            bf = _pad_cols(fox_b_f[j].reshape(1, heads), LANES)
            wo = fox_w_o[j].astype(BF16)

            xp = xs_by_group["p"]
            q, kt, vt, kb, vtb, lf = _fox_proj(xp, wq, wk, wv.T, wf, bf, tm, channel_major=True)
            o = _flash(q, kb, vtb, _cumsum(lf), tq=512, width=SLAB)
            to_heads = lambda z: jnp.transpose(z.reshape(bp, heads, HEAD, tp), (0, 3, 1, 2))
            outs["p"]["k"].append(to_heads(kt))
            outs["p"]["v"].append(to_heads(vt))
            outs["p"]["lf"].append(lf[:, :, :heads])
            xs_by_group["p"] = _post(o.reshape(bp * tp, d), None, xp.reshape(bp * tp, d),
                                     wo, w1, w2, ln, alpha, tm).reshape(bp, tp, d)

            xs = xs_by_group["s"]
            q, k, v, kb, vb, lf = _fox_proj(xs, wq, wk, wv, wf, bf, tm, channel_major=False)
            channel_major = lambda z: jnp.transpose(z, (0, 2, 3, 1)).reshape(bs, d, past)
            o = _decode_attn(q, channel_major(cache_k[j]), channel_major(cache_v[j]), kb, vb,
                             jnp.transpose(cache_logf[j].astype(F32), (0, 2, 1)), lf)
            outs["s"]["k"].append(k.reshape(bs, ts, heads, HEAD))
            outs["s"]["v"].append(v.reshape(bs, ts, heads, HEAD))
            outs["s"]["lf"].append(lf[:, :, :heads])
            xs_by_group["s"] = _post(o.reshape(bs * ts, d), None, xs.reshape(bs * ts, d),
                                     wo, w1, w2, ln, alpha, tm).reshape(bs, ts, d)

    st = jnp.stack
    op, os_ = outs["p"], outs["s"]
    return (xs_by_group["p"], xs_by_group["s"],
            st(op["wkv"]), st(op["shift"]), st(op["k"]), st(op["v"]), st(op["lf"]),
            st(os_["wkv"]), st(os_["shift"]), st(os_["k"]), st(os_["v"]), st(os_["lf"]))
```

```python
import functools
import math

import jax
import jax.numpy as jnp
from jax import lax
from jax.experimental import pallas as pl
from jax.experimental.pallas import tpu as pltpu

BF16 = jnp.bfloat16
F32 = jnp.float32

HEAD = 64
HEAD_SHIFT = 6
LANES = 128
SUBLANES = 8
SLAB = 256
HEADS_PER_SLAB = SLAB // HEAD
CHUNK = 64
WKV_CELLS_PER_STEP = 2
LN_EPS = 1e-5
GN_EPS = 64e-5
EXP_M05 = math.exp(-0.5)
LOG2E = math.log2(math.e)
VMEM_LIMIT = 56 * 1024 * 1024


def _dot(a, b):
    return jnp.dot(a, b, preferred_element_type=F32)


def _dot_nt(a, b):
    return lax.dot_general(a, b, (((1,), (1,)), ((), ())), preferred_element_type=F32)


def _dot_tn(a, b):
    return lax.dot_general(a, b, (((0,), (0,)), ((), ())), preferred_element_type=F32)


def _split3(x):
    p1 = x.astype(BF16)
    r1 = x - p1.astype(F32)
    p2 = r1.astype(BF16)
    p3 = (r1 - p2.astype(F32)).astype(BF16)
    return p1, p2, p3


def _dot3(a, b, dot=_dot, pieces=3):
    return sum(dot(p, b) for p in _split3(a)[:pieces])


def _tri(n, upper):
    r = lax.broadcasted_iota(jnp.int32, (n, n), 0)
    c = lax.broadcasted_iota(jnp.int32, (n, n), 1)
    return jnp.where((r <= c) if upper else (c <= r), 1.0, 0.0).astype(BF16)


def _layer_norm(z, g, b):
    mu = jnp.mean(z, axis=-1, keepdims=True)
    zc = z - mu
    var = jnp.mean(zc * zc, axis=-1, keepdims=True)
    return zc * lax.rsqrt(var + LN_EPS) * g + b


def _params(*sem):
    return pltpu.CompilerParams(dimension_semantics=sem, vmem_limit_bytes=VMEM_LIMIT)


def _const_spec(shape):
    nd = len(shape)
    return pl.BlockSpec(shape, lambda *_: (0,) * nd, pipeline_mode=pl.Buffered(1))


def _row_tile(n, want):
    t = min(n, want)
    assert n % t == 0 and t % SUBLANES == 0, (n, t)
    return t


def _seq_tile(b, t, want):
    tm = _row_tile(t, want)
    nb = min(b, max(want // t, 1)) if tm == t else 1
    assert b % nb == 0, (b, nb)
    return nb, tm


def _rwkv_proj_kernel(x_ref, prev_ref, first_ref, mu_ref, w0_ref, a0_ref, wr_ref, wk_ref, wv_ref,
                      w1_ref, w2_ref, a1_ref, a2_ref, g1_ref, g2_ref,
                      r_ref, lw_ref, k_ref, v_ref, a_ref, g_ref):
    nb, rows, d = x_ref.shape
    x = x_ref[...].reshape(nb * rows, d)
    row_in_seq = lax.broadcasted_iota(jnp.int32, x.shape, 0) % rows
    if nb == 1:
        row0 = jnp.where(pl.program_id(1) == 0, first_ref[0], prev_ref[0, SUBLANES - 1:SUBLANES, :])
    else:
        row0 = jnp.broadcast_to(first_ref[...], (nb, rows, d)).reshape(nb * rows, d)
    xx = jnp.where(row_in_seq == 0, row0, pltpu.roll(x, 1, 0)) - x
    out = lambda z: z.reshape(nb, rows, d)

    def mix(i):
        return (x + xx * mu_ref[i:i + 1, :]).astype(BF16)

    r_ref[...] = out(_dot(mix(0), wr_ref[...]))
    w_raw = w0_ref[...] + _dot(jnp.tanh(_dot(mix(1), w1_ref[...])).astype(BF16), w2_ref[...])
    lw_ref[...] = out(-EXP_M05 * jax.nn.sigmoid(w_raw))
    k_ref[...] = out(_dot(mix(2), wk_ref[...]))
    v_ref[...] = out(_dot(mix(3), wv_ref[...]))
    a_ref[...] = out(jax.nn.sigmoid(a0_ref[...] + _dot(_dot(mix(4), a1_ref[...]).astype(BF16), a2_ref[...])))
    g_ref[...] = out(_dot(jax.nn.sigmoid(_dot(mix(5), g1_ref[...])).astype(BF16), g2_ref[...]))


def _rwkv_proj(x, shift0, mu, w0, a0, wr, wk, wv, w1, w2, a1, a2, g1, g2, tm):
    b, t, d = x.shape
    nb, tm = _seq_tile(b, t, tm)
    per_tile = tm // SUBLANES
    row = pl.BlockSpec((nb, tm, d), lambda bi, i: (bi, i, 0))
    prev = pl.BlockSpec((1, SUBLANES, d), lambda bi, i: (bi * nb, jnp.maximum(i * per_tile - 1, 0), 0))
    first = pl.BlockSpec((nb, 1, d), lambda bi, i: (bi, 0, 0))
    consts = [mu, w0, a0, wr, wk, wv, w1, w2, a1, a2, g1, g2]
    return pl.pallas_call(
        _rwkv_proj_kernel,
        grid=(b // nb, t // tm),
        in_specs=[row, prev, first] + [_const_spec(c.shape) for c in consts],
        out_specs=[row] * 6,
        out_shape=[jax.ShapeDtypeStruct((b, t, d), F32)] * 6,
        compiler_params=_params("parallel", "parallel"),
        name="rwkv_proj",
    )(x, x, shift0, *consts)


def _wkv_kernel(r_ref, lw_ref, k_ref, v_ref, a_ref, kk_ref, ka_ref, rk_ref, lg_ref, lb_ref, h0_ref,
                o_ref, hout_ref, h_scr, *, last_chunk):
    c = pl.program_id(1)
    nslab = h_scr.shape[0]
    C = CHUNK
    nseq, rows = r_ref.shape[:2]
    nslab = nslab // nseq

    @pl.when(c == 0)
    def _():
        h_scr[...] = jnp.zeros_like(h_scr)
        for n in range(nseq * nslab):
            for h in range(HEADS_PER_SLAB):
                hs = slice(h * HEAD, (h + 1) * HEAD)
                h_scr[n, hs, hs] = h0_ref[n // nslab, (n % nslab) * HEADS_PER_SLAB + h]

    row_s = lax.broadcasted_iota(jnp.int32, (SLAB, SLAB), 0)
    col_s = lax.broadcasted_iota(jnp.int32, (SLAB, SLAB), 1)
    mask_bd = (row_s >> HEAD_SHIFT) == (col_s >> HEAD_SHIFT)
    eye_s = row_s == col_s
    row_c = lax.broadcasted_iota(jnp.int32, (C, SLAB), 0)
    colin_c = lax.broadcasted_iota(jnp.int32, (C, SLAB), 1) & (C - 1)
    strict_lo = colin_c < row_c
    incl_lo = colin_c <= row_c
    eye_c = colin_c == row_c
    ones_bd = jnp.where(mask_bd, 1.0, 0.0).astype(BF16)

    def bd(x):
        xb = x.astype(BF16)
        return jnp.where(mask_bd, jnp.concatenate([xb] * HEADS_PER_SLAB, axis=0), jnp.zeros((), BF16))

    def head_sum(x):
        return _dot(x.astype(BF16), ones_bd)

    nck = max(rows // C, 1)
    live = min(rows, C)

    ncell = nseq * nck

    def chunk_rows(ref, ci, sl):
        x = ref[ci // nck, (ci % nck) * C:(ci % nck) * C + live, sl]
        if live < C:
            x = jnp.concatenate([x, jnp.zeros((C - live, x.shape[1]), x.dtype)], axis=0)
        return x

    lw_c = [chunk_rows(lw_ref, ci, slice(None)) for ci in range(ncell)]
    L_c = [_dot3(lw_c[ci], _tri(C, upper=False), dot=lambda p, t: _dot(t, p), pieces=2) for ci in range(ncell)]

    units = [(ci, s) for ci in range(ncell) for s in range(nslab)]
    S = range(len(units))
    slabs = [slice(s * SLAB, (s + 1) * SLAB) for _, s in units]
    r = [chunk_rows(r_ref, ci, slabs[u]) for u, (ci, _) in enumerate(units)]
    v = [chunk_rows(v_ref, ci, slabs[u]) for u, (ci, _) in enumerate(units)]
    asig = [chunk_rows(a_ref, ci, slabs[u]) for u, (ci, _) in enumerate(units)]
    k0 = [chunk_rows(k_ref, ci, slabs[u]) for u, (ci, _) in enumerate(units)]
    lw = [lw_c[ci][:, slabs[u]] for u, (ci, _) in enumerate(units)]
    kk = [k0[s] * kk_ref[:, slabs[s]] for s in S]
    k = [k0[s] * (1.0 + (asig[s] - 1.0) * ka_ref[:, slabs[s]]) for s in S]
    sums = [head_sum(jnp.concatenate([kk[s] * kk[s], r[s] * k[s] * rk_ref[:, slabs[s]]], axis=0)) for s in S]
    kk = [kk[s] / jnp.maximum(jnp.sqrt(sums[s][:C]), 1e-12) for s in S]
    bonus = [sums[s][C:] * v[s] for s in S]
    bv = [kk[s] * asig[s] for s in S]
    L = [L_c[ci][:, slabs[u]] for u, (ci, _) in enumerate(units)]
    LC = [L[s][C - 1:C, :] for s in S]
    enL = [jnp.exp(-L[s]) for s in S]
    eLc = [jnp.exp(LC[s] - L[s]) for s in S]
    At = [-kk[s] * jnp.exp(L[s] - lw[s]) for s in S]
    Rt = [r[s] * jnp.exp(L[s]) for s in S]
    Bt = [bv[s] * enL[s] for s in S]
    Kt = [k[s] * enL[s] for s in S]
    Bh = [bv[s] * eLc[s] for s in S]
    Kh = [k[s] * eLc[s] for s in S]

    lhs = [jnp.concatenate([At[s], Rt[s]], axis=0).astype(BF16) for s in S]
    ab = [_dot_nt(lhs[s], bd(Bt[s])) for s in S]
    ak = [_dot_nt(lhs[s], bd(Kt[s])) for s in S]
    N = [jnp.where(strict_lo, ab[s][:C], 0.0) for s in S]
    M = [jnp.where(strict_lo, ak[s][:C], 0.0) for s in S]
    Arb = [jnp.where(incl_lo, ab[s][C:], 0.0) for s in S]
    Ark = [jnp.where(incl_lo, ak[s][C:], 0.0) for s in S]
    MV = [_dot(M[s].astype(BF16), bd(v[s])) for s in S]

    X = [jnp.where(eye_c, 1.0, 0.0) + N[s] for s in S]
    Pw = N
    for _ in range(int(math.log2(C)) - 1):
        Pw = [_dot(Pw[s].astype(BF16), bd(Pw[s])) for s in S]
        X = [X[s] + _dot(X[s].astype(BF16), bd(Pw[s])) for s in S]

    wu = [_dot(X[s].astype(BF16), jnp.concatenate([bd(At[s]), bd(MV[s])], axis=1)) for s in S]
    WA = [wu[s][:, :SLAB] for s in S]
    UV = [wu[s][:, SLAB:] for s in S]
    Yl = [_dot(jnp.concatenate([Arb[s], Ark[s]], axis=1).astype(BF16),
               jnp.concatenate([bd(UV[s]), bd(v[s])], axis=0)) for s in S]
    Rp = [Rt[s] + _dot(Arb[s].astype(BF16), bd(WA[s])) for s in S]
    rw = [jnp.concatenate([Rp[s], WA[s]], axis=0).astype(BF16) for s in S]
    bk = [jnp.concatenate([Bh[s], Kh[s]], axis=0).astype(BF16) for s in S]
    pc_col = [jnp.sum(jnp.where(eye_s, jnp.exp(LC[s]), 0.0), axis=1, keepdims=True) for s in S]

    h = [h_scr[n] for n in range(nseq * nslab)]
    y = [None] * len(units)
    for cj in range(nck):
        us = [((n // nslab) * nck + cj) * nslab + n % nslab for n in range(nseq * nslab)]
        yu = [_dot(rw[u], h[n].astype(BF16)) for n, u in enumerate(us)]
        for n, u in enumerate(us):
            y[u] = yu[n][:C] + Yl[u]
        uv = [jnp.concatenate([yu[n][C:] + UV[u], v[u]], axis=0).astype(BF16) for n, u in enumerate(us)]
        h = [h[n] * pc_col[u] + jnp.where(mask_bd, _dot_tn(bk[u], uv[n]), 0.0) for n, u in enumerate(us)]
    for n in range(nseq * nslab):
        h_scr[n] = h[n]

    mu = [head_sum(y[s]) * (1.0 / HEAD) for s in S]
    yc = [y[s] - mu[s] for s in S]
    var = [head_sum(yc[s] * yc[s]) * (1.0 / HEAD) for s in S]
    for u, (ci, _) in enumerate(units):
        sl = slabs[u]
        out = yc[u] * lax.rsqrt(var[u] + GN_EPS) * lg_ref[:, sl] + lb_ref[:, sl] + bonus[u]
        o_ref[ci // nck, (ci % nck) * C:(ci % nck) * C + live, sl] = out[:live]

    @pl.when(c == last_chunk)
    def _():
        for n in range(nseq * nslab):
            for h in range(HEADS_PER_SLAB):
                hs = slice(h * HEAD, (h + 1) * HEAD)
                hout_ref[n // nslab, (n % nslab) * HEADS_PER_SLAB + h] = h_scr[n, hs, hs]


def _wkv(r, lw, k, v, a, k_k, k_a, r_k, lnx_g, lnx_b, h0):
    b, t, d = r.shape
    if t % (WKV_CELLS_PER_STEP * CHUNK) == 0:
        nseq, rows = 1, WKV_CELLS_PER_STEP * CHUNK
    else:
        nseq, rows = (WKV_CELLS_PER_STEP if t <= CHUNK and b % WKV_CELLS_PER_STEP == 0 else 1), min(t, CHUNK)
    assert t % rows == 0 and rows % SUBLANES == 0
    nslab = d // SLAB
    heads = d // HEAD
    seq = pl.BlockSpec((nseq, rows, d), lambda i, c: (i, c, 0))
    vec = pl.BlockSpec((1, d), lambda i, c: (0, 0))
    hspec = pl.BlockSpec((nseq, heads, HEAD, HEAD), lambda i, c: (i, 0, 0, 0))
    return pl.pallas_call(
        functools.partial(_wkv_kernel, last_chunk=t // rows - 1),
        grid=(b // nseq, t // rows),
        in_specs=[seq] * 5 + [vec] * 5 + [hspec],
        out_specs=[seq, hspec],
        out_shape=[jax.ShapeDtypeStruct((b, t, d), F32),
                   jax.ShapeDtypeStruct((b, heads, HEAD, HEAD), F32)],
        scratch_shapes=[pltpu.VMEM((nseq * nslab, SLAB, SLAB), F32)],
        compiler_params=_params("parallel", "arbitrary"),
        name="wkv_chunked",
    )(r, lw, k, v, a, k_k, k_a, r_k, lnx_g, lnx_b, h0)


def _post_kernel(*refs, alpha, gated, ff_chunk):
    if gated:
        pre_ref, gate_ref, x_ref, wo_ref, w1_ref, w2_ref, ln_ref, out_ref = refs
        pre = pre_ref[...] * gate_ref[...]
    else:
        pre_ref, x_ref, wo_ref, w1_ref, w2_ref, ln_ref, out_ref = refs
        pre = pre_ref[...]
    h = _dot(pre.astype(BF16), wo_ref[...])
    x1 = _layer_norm(alpha * x_ref[...] + h, ln_ref[0:1, :], ln_ref[1:2, :])
    x1b = x1.astype(BF16)
    m = jnp.zeros_like(x1)
    for f in range(w1_ref.shape[1] // ff_chunk):
        fs = slice(f * ff_chunk, (f + 1) * ff_chunk)
        hid = jnp.maximum(_dot(x1b, w1_ref[:, fs]), 0.0)
        m = m + _dot((hid * hid).astype(BF16), w2_ref[fs, :])
    out_ref[...] = _layer_norm(alpha * x1 + m, ln_ref[2:3, :], ln_ref[3:4, :])


def _post(pre, gate, x, wo, w1, w2, ln, alpha, tm):
    n, d = x.shape
    tm = _row_tile(n, tm)
    row = pl.BlockSpec((tm, d), lambda i: (i, 0))
    acts = [pre, x] if gate is None else [pre, gate, x]
    consts = [wo, w1, w2, ln]
    return pl.pallas_call(
        functools.partial(_post_kernel, alpha=alpha, gated=gate is not None, ff_chunk=min(1024, w1.shape[1])),
        grid=(n // tm,),
        in_specs=[row] * len(acts) + [_const_spec(c.shape) for c in consts],
        out_specs=row,
        out_shape=jax.ShapeDtypeStruct((n, d), F32),
        compiler_params=_params("parallel"),
        name="post_mlp",
    )(*acts, *consts)


def _fox_proj_kernel(x_ref, wq_ref, wk_ref, wv_ref, wf_ref, bf_ref,
                     q_ref, k_ref, v_ref, kb_ref, vb_ref, lf_ref, *, channel_major):
    nb, rows, d = x_ref.shape
    xb = x_ref[...].reshape(nb * rows, d).astype(BF16)
    out = lambda z: z.reshape(nb, rows, z.shape[-1])
    q_ref[...] = out((_dot(xb, wq_ref[...]) * (HEAD ** -0.5 * LOG2E)).astype(BF16))
    k = _dot(xb, wk_ref[...])
    kb_ref[...] = out(k.astype(BF16))
    if channel_major:
        k_ref[0] = k.T
        v = _dot_nt(wv_ref[...], xb)
        v_ref[0] = v
        vb_ref[0] = v.astype(BF16)
    else:
        v = _dot(xb, wv_ref[...])
        k_ref[...] = out(k)
        v_ref[...] = out(v)
        vb_ref[...] = out(v.astype(BF16))
    f = _dot(xb, wf_ref[...]) + bf_ref[...]
    lf_ref[...] = out(jnp.minimum(f, 0.0) - jnp.log1p(jnp.exp(-jnp.abs(f))))


def _fox_proj(x, wq, wk, wv, wf, bf, tm, channel_major):
    b, t, d = x.shape
    nb, tm = _seq_tile(b, t, tm)
    assert nb == 1 or not channel_major
    row = pl.BlockSpec((nb, tm, d), lambda bi, i: (bi, i, 0))
    rowf = pl.BlockSpec((nb, tm, LANES), lambda bi, i: (bi, i, 0))
    kv_spec = pl.BlockSpec((1, d, tm), lambda bi, i: (bi, 0, i)) if channel_major else row
    kv_shape = jax.ShapeDtypeStruct((b, d, t) if channel_major else (b, t, d), F32)
    act_bf = jax.ShapeDtypeStruct((b, t, d), BF16)
    vb_shape = jax.ShapeDtypeStruct(kv_shape.shape, BF16)
    consts = [wq, wk, wv, wf, bf]
    return pl.pallas_call(
        functools.partial(_fox_proj_kernel, channel_major=channel_major),
        grid=(b // nb, t // tm),
        in_specs=[row] + [_const_spec(c.shape) for c in consts],
        out_specs=[row, kv_spec, kv_spec, row, kv_spec, rowf],
        out_shape=[act_bf, kv_shape, kv_shape, act_bf, vb_shape, jax.ShapeDtypeStruct((b, t, LANES), F32)],
        compiler_params=_params("parallel", "parallel"),
        name="fox_proj",
    )(x, *consts)


CS_TILE = 512
SUM_ROWS = 16


def _cumsum_kernel(lf_ref, c_ref, carry_ref):
    @pl.when(pl.program_id(1) == 0)
    def _():
        carry_ref[...] = jnp.zeros_like(carry_ref)

    n = lf_ref.shape[1]
    cs = _dot3(lf_ref[0], _tri(n, upper=False), dot=lambda p, t: _dot(t, p)) + carry_ref[0:1, :]
    c_ref[0] = cs
    carry_ref[...] = jnp.broadcast_to(cs[n - 1:n, :], carry_ref.shape)


def _cumsum(lf):
    b, l, w = lf.shape
    tile = _row_tile(l, CS_TILE)
    spec = pl.BlockSpec((1, tile, w), lambda i, j: (i, j, 0))
    return pl.pallas_call(
        _cumsum_kernel,
        grid=(b, l // tile),
        in_specs=[spec],
        out_specs=spec,
        out_shape=jax.ShapeDtypeStruct((b, l, w), F32),
        scratch_shapes=[pltpu.VMEM((SUBLANES, w), F32)],
        compiler_params=_params("parallel", "arbitrary"),
        name="logf_cumsum",
    )(lf)


def _bias_lanes(c, head0, width, key_side):
    nh = width // HEAD
    src = lax.broadcasted_iota(jnp.int32, (LANES, width), 0)
    dst = lax.broadcasted_iota(jnp.int32, (LANES, width), 1)
    lane = lax.broadcasted_iota(jnp.int32, (1, width), 1)
    lo, sign = (3, -1.0) if key_side else (0, 1.0)
    base = [((hh + 1) % nh) * HEAD for hh in range(nh)]
    ones = functools.reduce(jnp.logical_or, [(lane >= bs + 3 - lo) & (lane < bs + 6 - lo) for bs in base])
    out = jnp.where(ones, 1.0, 0.0)
    for n, p in enumerate(_split3(c * LOG2E)):
        hit = functools.reduce(jnp.logical_or,
                               [(src == head0 + hh) & (dst == base[hh] + lo + n) for hh in range(nh)])
        out = out + _dot(p, jnp.where(hit, sign, 0.0).astype(BF16))
    return out


def _flash_kernel(q_ref, k_ref, vt_ref, c_ref, o_ref, kaug_scr, *, tq):
    g = pl.program_id(1)
    i = pl.program_id(2)
    t = k_ref.shape[1]
    width = k_ref.shape[2]
    nh = width // HEAD
    lane = lax.broadcasted_iota(jnp.int32, (1, width), 1)
    in_head = [(lane >> HEAD_SHIFT) == hh for hh in range(nh)]
    bias_lanes = [(lane >= ((hh + 1) % nh) * HEAD) & (lane < ((hh + 1) % nh) * HEAD + 6) for hh in range(nh)]

    def augment(x, bias):
        bias = bias.astype(BF16)
        return [jnp.where(in_head[hh], x, jnp.where(bias_lanes[hh], bias, jnp.zeros((), BF16)))
                for hh in range(nh)]

    @pl.when(i == 0)
    def _():
        for blk in range(t // tq):
            rows = slice(blk * tq, (blk + 1) * tq)
            ka = augment(k_ref[0, rows, :], _bias_lanes(c_ref[0, rows, :], nh * g, width, key_side=True))
            for hh in range(nh):
                kaug_scr[hh, rows, :] = ka[hh]

    row0 = pl.multiple_of(i * tq, tq)
    qa = augment(q_ref[0], _bias_lanes(c_ref[0, pl.ds(row0, tq), :], nh * g, width, key_side=False))

    qg = min(tq, SLAB)
    chains = [(hh, qs) for hh in range(nh) for qs in range(0, tq, qg)]
    nc = len(chains)
    qa_c = [qa[hh][qs:qs + qg] for hh, qs in chains]

    ones_rows = jnp.ones((SUM_ROWS, tq), BF16)

    def step(off, carry, mask):
        ks = [kaug_scr[hh, pl.ds(off, tq), :] for hh in range(nh)]
        vts = [jnp.concatenate([vt_ref[0, hh * HEAD:(hh + 1) * HEAD, pl.ds(off, tq)], ones_rows], axis=0)
               for hh in range(nh)]
        kext = [tq if mask is None else qs + qg for _, qs in chains]
        st = [_dot_nt(ks[hh][:kext[n]], qa_c[n]) for n, (hh, _) in enumerate(chains)]
        if mask is not None:
            st = [jnp.where(mask[:kext[n], qs:qs + qg], st[n], -jnp.inf) for n, (_, qs) in enumerate(chains)]
        m_new = [jnp.maximum(carry[n][0], jnp.max(st[n], axis=0, keepdims=True)) for n in range(nc)]
        alpha = [jnp.exp2(carry[n][0] - m_new[n]) for n in range(nc)]
        p = [jnp.exp2(st[n] - m_new[n]) for n in range(nc)]
        acc = [alpha[n] * carry[n][1] + _dot(vts[hh][:, :kext[n]], p[n].astype(BF16))
               for n, (hh, _) in enumerate(chains)]
        return tuple(zip(m_new, acc))

    init = tuple((jnp.full((1, qg), -jnp.inf, F32), jnp.zeros((HEAD + SUM_ROWS, qg), F32)) for _ in chains)
    carry = lax.fori_loop(0, i, lambda j, c: step(pl.multiple_of(j * tq, tq), c, None), init)
    causal = lax.broadcasted_iota(jnp.int32, (tq, tq), 0) <= lax.broadcasted_iota(jnp.int32, (tq, tq), 1)
    carry = step(row0, carry, causal)
    per_head = [jnp.concatenate([carry[n][1][:HEAD] / carry[n][1][HEAD:HEAD + 1]
                                 for n, (h2, _) in enumerate(chains) if h2 == hh], axis=1) for hh in range(nh)]
    o_ref[0] = jnp.concatenate(per_head, axis=0).T


def _flash(q, k, vt, c, tq, width):
    b, t, d = q.shape
    tq = _row_tile(t, tq)
    return pl.pallas_call(
        functools.partial(_flash_kernel, tq=tq),
        grid=(b, d // width, t // tq),
        in_specs=[pl.BlockSpec((1, tq, width), lambda bi, gi, i: (bi, i, gi)),
                  pl.BlockSpec((1, t, width), lambda bi, gi, i: (bi, 0, gi)),
                  pl.BlockSpec((1, width, t), lambda bi, gi, i: (bi, gi, 0)),
                  pl.BlockSpec((1, t, LANES), lambda bi, gi, i: (bi, 0, 0))],
        out_specs=pl.BlockSpec((1, tq, width), lambda bi, gi, i: (bi, i, gi)),
        out_shape=jax.ShapeDtypeStruct((b, t, d), F32),
        scratch_shapes=[pltpu.VMEM((width // HEAD, t, width), BF16)],
        compiler_params=_params("parallel", "parallel", "arbitrary"),
        name="fox_flash",
    )(q, k, vt, c)


DEC_TILE = 512
DEC_WIDTH = 512


def _decode_attn_kernel(q_ref, ckt_ref, cvt_ref, kn_ref, vn_ref, clf_ref, lfn_ref, o_ref, c_scr):
    g = pl.program_id(1)
    past = ckt_ref.shape[2]
    t = q_ref.shape[1]

    @pl.when(g == 0)
    def _():
        tri = _tri(DEC_TILE, upper=True)
        carry = jnp.zeros((clf_ref.shape[1], 1), F32)
        for blk in range(past // DEC_TILE):
            cols = slice(blk * DEC_TILE, (blk + 1) * DEC_TILE)
            cs = _dot3(clf_ref[0, :, cols], tri) + carry
            c_scr[:, cols] = cs
            carry = cs[:, DEC_TILE - 1:DEC_TILE]

    width = q_ref.shape[2]
    nh = width // HEAD
    lane = lax.broadcasted_iota(jnp.int32, (1, width), 1)
    hlane = lax.broadcasted_iota(jnp.int32, (1, LANES), 1)
    q = q_ref[0]
    kct = ckt_ref[0].astype(BF16)
    vct = cvt_ref[0].astype(BF16)
    kn = kn_ref[0]
    vn = vn_ref[0]
    causal = lax.broadcasted_iota(jnp.int32, (t, t), 1) <= lax.broadcasted_iota(jnp.int32, (t, t), 0)
    lf_new = _dot3(lfn_ref[0], _tri(t, upper=False), dot=lambda p, tr: _dot(tr, p))

    H = range(nh)
    head = [nh * g + hh for hh in H]
    in_head = [(lane >> HEAD_SHIFT) == hh for hh in H]
    rows = lambda parts: jnp.concatenate(parts, axis=0)
    qs = rows([jnp.where(in_head[hh], q, jnp.zeros((), BF16)) for hh in H])
    c_cache = [c_scr[pl.ds(head[hh], 1), :] for hh in H]
    total = [c_cache[hh][:, past - 1:past] for hh in H]
    onehot = [jnp.where(hlane == head[hh], 1.0, 0.0) for hh in H]
    cn_col = [jnp.sum(lf_new * onehot[hh], axis=-1, keepdims=True) + total[hh] for hh in H]
    cn_row = [_dot3(lf_new, jnp.broadcast_to(onehot[hh], (SUBLANES, LANES)).astype(BF16),
                    dot=lambda p, e: _dot_nt(e, p))[0:1] + total[hh] for hh in H]
    s1 = _dot(qs, kct) + rows([cn_col[hh] - c_cache[hh] for hh in H]) * LOG2E
    s2 = _dot_nt(qs, kn) + rows([cn_col[hh] - cn_row[hh] for hh in H]) * LOG2E
    s2 = jnp.where(rows([causal] * nh), s2, -jnp.inf)
    m = jnp.maximum(jnp.max(s1, axis=-1, keepdims=True), jnp.max(s2, axis=-1, keepdims=True))
    e1 = jnp.exp2(s1 - m)
    e2 = jnp.exp2(s2 - m)
    l = jnp.sum(e1, axis=-1, keepdims=True) + jnp.sum(e2, axis=-1, keepdims=True)
    o = (_dot_nt(e1.astype(BF16), vct) + _dot(e2.astype(BF16), vn)) / l
    out = o[(nh - 1) * t:]
    for hh in range(nh - 2, -1, -1):
        out = jnp.where(in_head[hh], o[hh * t:(hh + 1) * t], out)
    o_ref[0] = out


def _decode_attn(q, cache_kt, cache_vt, kn, vn, cache_lf, lf_new):
    b, t, d = q.shape
    p = cache_kt.shape[2]
    h = cache_lf.shape[1]
    assert p % DEC_TILE == 0
    new = pl.BlockSpec((1, t, DEC_WIDTH), lambda bi, gi: (bi, 0, gi))
    old = pl.BlockSpec((1, DEC_WIDTH, p), lambda bi, gi: (bi, gi, 0))
    return pl.pallas_call(
        _decode_attn_kernel,
        grid=(b, d // DEC_WIDTH),
        in_specs=[new, old, old, new, new,
                  pl.BlockSpec((1, h, p), lambda bi, gi: (bi, 0, 0)),
                  pl.BlockSpec((1, t, LANES), lambda bi, gi: (bi, 0, 0))],
        out_specs=new,
        out_shape=jax.ShapeDtypeStruct((b, t, d), F32),
        scratch_shapes=[pltpu.VMEM((h, p), F32)],
        compiler_params=_params("parallel", "arbitrary"),
        name="fox_decode_attn",
    )(q, cache_kt, cache_vt, kn, vn, cache_lf, lf_new)


def _pad_cols(w, mult):
    pad = (-w.shape[-1]) % mult
    return jnp.pad(w, ((0, 0), (0, pad))) if pad else w


def _pad_rows(w, mult):
    pad = (-w.shape[0]) % mult
    return jnp.pad(w, ((0, pad), (0, 0))) if pad else w


def kernel(x_prompt, x_sample, state_wkv, state_shift, cache_k, cache_v, cache_logf, rwkv_mu, rwkv_w0, rwkv_w1, rwkv_w2, rwkv_a0, rwkv_a1, rwkv_a2, rwkv_g1, rwkv_g2, rwkv_k_k, rwkv_k_a, rwkv_r_k, rwkv_w_r, rwkv_w_k, rwkv_w_v, rwkv_w_o, rwkv_lnx_g, rwkv_lnx_b, fox_w_in, fox_b_f, fox_w_o, ffn_w1, ffn_w2, ln_mix_g, ln_mix_b, ln_ffn_g, ln_ffn_b):
    depth = ln_mix_g.shape[0]
    alpha = (2 * depth) ** 0.25
    bp, tp, d = x_prompt.shape
    bs, ts, _ = x_sample.shape
    heads = d // HEAD
    past = cache_k.shape[2]
    tm = 512

    xs_by_group = {"p": x_prompt, "s": x_sample}
    outs = {grp: {n: [] for n in ("wkv", "shift", "k", "v", "lf")} for grp in ("p", "s")}
    for i in range(depth):
        j = i // 2
        ln = jnp.stack([ln_mix_g[i], ln_mix_b[i], ln_ffn_g[i], ln_ffn_b[i]])
        w1 = ffn_w1[i].astype(BF16)
        w2 = ffn_w2[i].astype(BF16)
        if i % 2 == 0:
            row = lambda z: z.reshape(1, d)
            proj = (rwkv_mu[j], row(rwkv_w0[j]), row(rwkv_a0[j]),
                    rwkv_w_r[j].astype(BF16), rwkv_w_k[j].astype(BF16), rwkv_w_v[j].astype(BF16),
                    _pad_cols(rwkv_w1[j], LANES).astype(BF16), _pad_rows(rwkv_w2[j], LANES).astype(BF16),
                    _pad_cols(rwkv_a1[j], LANES).astype(BF16), _pad_rows(rwkv_a2[j], LANES).astype(BF16),
                    _pad_cols(rwkv_g1[j], LANES).astype(BF16), _pad_rows(rwkv_g2[j], LANES).astype(BF16))
            wkv_vecs = (row(rwkv_k_k[j]), row(rwkv_k_a[j]), row(rwkv_r_k[j]), row(rwkv_lnx_g[j]), row(rwkv_lnx_b[j]))
            wo = rwkv_w_o[j].astype(BF16)
            starts = {"p": (jnp.zeros((bp, d), x_prompt.dtype), jnp.zeros((bp, heads, HEAD, HEAD), state_wkv.dtype)),
                      "s": (state_shift[j], state_wkv[j])}
            for grp in ("p", "s"):
                x = xs_by_group[grp]
                b, t, _ = x.shape
                shift0, wkv0 = starts[grp]
                r, lw, k, v, a, gate = _rwkv_proj(x, shift0.reshape(b, 1, d), *proj, tm=tm)
                h0 = jnp.swapaxes(wkv0.astype(F32), -1, -2)
                o, hfin = _wkv(r, lw, k, v, a, *wkv_vecs, h0)
                outs[grp]["wkv"].append(jnp.swapaxes(hfin, -1, -2).astype(wkv0.dtype))
                outs[grp]["shift"].append(x[:, -1])
                xs_by_group[grp] = _post(o.reshape(b * t, d), gate.reshape(b * t, d), x.reshape(b * t, d),
                                         wo, w1, w2, ln, alpha, tm).reshape(b, t, d)
        else:
            w_in = fox_w_in[j]
            wq = w_in[:, :d].astype(BF16)
            wk = w_in[:, d:2 * d].astype(BF16)
            wv = w_in[:, 2 * d:3 * d].astype(BF16)
            wf = _pad_cols(w_in[:, 3 * d:], LANES).astype(BF16)
            bf = _pad_cols(fox_b_f[j].reshape(1, heads), LANES)
            wo = fox_w_o[j].astype(BF16)

            xp = xs_by_group["p"]
            q, kt, vt, kb, vtb, lf = _fox_proj(xp, wq, wk, wv.T, wf, bf, tm, channel_major=True)
            o = _flash(q, kb, vtb, _cumsum(lf), tq=512, width=SLAB)
            to_heads = lambda z: jnp.transpose(z.reshape(bp, heads, HEAD, tp), (0, 3, 1, 2))
            outs["p"]["k"].append(to_heads(kt))
            outs["p"]["v"].append(to_heads(vt))
            outs["p"]["lf"].append(lf[:, :, :heads])
            xs_by_group["p"] = _post(o.reshape(bp * tp, d), None, xp.reshape(bp * tp, d),
                                     wo, w1, w2, ln, alpha, tm).reshape(bp, tp, d)

            xs = xs_by_group["s"]
            q, k, v, kb, vb, lf = _fox_proj(xs, wq, wk, wv, wf, bf, tm, channel_major=False)
            channel_major = lambda z: jnp.transpose(z, (0, 2, 3, 1)).reshape(bs, d, past)
            o = _decode_attn(q, channel_major(cache_k[j]), channel_major(cache_v[j]), kb, vb,
                             jnp.transpose(cache_logf[j].astype(F32), (0, 2, 1)), lf)
            outs["s"]["k"].append(k.reshape(bs, ts, heads, HEAD))
            outs["s"]["v"].append(v.reshape(bs, ts, heads, HEAD))
            outs["s"]["lf"].append(lf[:, :, :heads])
            xs_by_group["s"] = _post(o.reshape(bs * ts, d), None, xs.reshape(bs * ts, d),
                                     wo, w1, w2, ln, alpha, tm).reshape(bs, ts, d)

    st = jnp.stack
    op, os_ = outs["p"], outs["s"]
    return (xs_by_group["p"], xs_by_group["s"],
            st(op["wkv"]), st(op["shift"]), st(op["k"]), st(op["v"]), st(op["lf"]),
            st(os_["wkv"]), st(os_["shift"]), st(os_["k"]), st(os_["v"]), st(os_["lf"]))
```

```python
import functools
import math

import jax
import jax.numpy as jnp
from jax import lax
from jax.experimental import pallas as pl
from jax.experimental.pallas import tpu as pltpu

BF16 = jnp.bfloat16
F32 = jnp.float32

HEAD = 64
HEAD_SHIFT = 6
LANES = 128
SUBLANES = 8
SLAB = 256
HEADS_PER_SLAB = SLAB // HEAD
CHUNK = 64
WKV_CELLS_PER_STEP = 2
LN_EPS = 1e-5
GN_EPS = 64e-5
EXP_M05 = math.exp(-0.5)
LOG2E = math.log2(math.e)
VMEM_LIMIT = 56 * 1024 * 1024


def _dot(a, b):
    return jnp.dot(a, b, preferred_element_type=F32)


def _dot_nt(a, b):
    return lax.dot_general(a, b, (((1,), (1,)), ((), ())), preferred_element_type=F32)


def _dot_tn(a, b):
    return lax.dot_general(a, b, (((0,), (0,)), ((), ())), preferred_element_type=F32)


def _split3(x):
    p1 = x.astype(BF16)
    r1 = x - p1.astype(F32)
    p2 = r1.astype(BF16)
    p3 = (r1 - p2.astype(F32)).astype(BF16)
    return p1, p2, p3


def _dot3(a, b, dot=_dot, pieces=3):
    return sum(dot(p, b) for p in _split3(a)[:pieces])


def _tri(n, upper):
    r = lax.broadcasted_iota(jnp.int32, (n, n), 0)
    c = lax.broadcasted_iota(jnp.int32, (n, n), 1)
    return jnp.where((r <= c) if upper else (c <= r), 1.0, 0.0).astype(BF16)


def _layer_norm(z, g, b):
    mu = jnp.mean(z, axis=-1, keepdims=True)
    zc = z - mu
    var = jnp.mean(zc * zc, axis=-1, keepdims=True)
    return zc * lax.rsqrt(var + LN_EPS) * g + b


def _params(*sem):
    return pltpu.CompilerParams(dimension_semantics=sem, vmem_limit_bytes=VMEM_LIMIT)


def _const_spec(shape):
    nd = len(shape)
    return pl.BlockSpec(shape, lambda *_: (0,) * nd, pipeline_mode=pl.Buffered(1))


def _row_tile(n, want):
    t = min(n, want)
    assert n % t == 0 and t % SUBLANES == 0, (n, t)
    return t


def _seq_tile(b, t, want):
    tm = _row_tile(t, want)
    nb = min(b, max(want // t, 1)) if tm == t else 1
    assert b % nb == 0, (b, nb)
    return nb, tm


def _rwkv_proj_kernel(x_ref, prev_ref, first_ref, mu_ref, w0_ref, a0_ref, wr_ref, wk_ref, wv_ref,
                      w1_ref, w2_ref, a1_ref, a2_ref, g1_ref, g2_ref,
                      r_ref, lw_ref, k_ref, v_ref, a_ref, g_ref):
    nb, rows, d = x_ref.shape
    x = x_ref[...].reshape(nb * rows, d)
    row_in_seq = lax.broadcasted_iota(jnp.int32, x.shape, 0) % rows
    if nb == 1:
        row0 = jnp.where(pl.program_id(1) == 0, first_ref[0], prev_ref[0, SUBLANES - 1:SUBLANES, :])
    else:
        row0 = jnp.broadcast_to(first_ref[...], (nb, rows, d)).reshape(nb * rows, d)
    xx = jnp.where(row_in_seq == 0, row0, pltpu.roll(x, 1, 0)) - x
    out = lambda z: z.reshape(nb, rows, d)

    def mix(i):
        return (x + xx * mu_ref[i:i + 1, :]).astype(BF16)

    r_ref[...] = out(_dot(mix(0), wr_ref[...]))
    w_raw = w0_ref[...] + _dot(jnp.tanh(_dot(mix(1), w1_ref[...])).astype(BF16), w2_ref[...])
    lw_ref[...] = out(-EXP_M05 * jax.nn.sigmoid(w_raw))
    k_ref[...] = out(_dot(mix(2), wk_ref[...]))
    v_ref[...] = out(_dot(mix(3), wv_ref[...]))
    a_ref[...] = out(jax.nn.sigmoid(a0_ref[...] + _dot(_dot(mix(4), a1_ref[...]).astype(BF16), a2_ref[...])))
    g_ref[...] = out(_dot(jax.nn.sigmoid(_dot(mix(5), g1_ref[...])).astype(BF16), g2_ref[...]))


def _rwkv_proj(x, shift0, mu, w0, a0, wr, wk, wv, w1, w2, a1, a2, g1, g2, tm):
    b, t, d = x.shape
    nb, tm = _seq_tile(b, t, tm)
    per_tile = tm // SUBLANES
    row = pl.BlockSpec((nb, tm, d), lambda bi, i: (bi, i, 0))
    prev = pl.BlockSpec((1, SUBLANES, d), lambda bi, i: (bi * nb, jnp.maximum(i * per_tile - 1, 0), 0))
    first = pl.BlockSpec((nb, 1, d), lambda bi, i: (bi, 0, 0))
    consts = [mu, w0, a0, wr, wk, wv, w1, w2, a1, a2, g1, g2]
    return pl.pallas_call(
        _rwkv_proj_kernel,
        grid=(b // nb, t // tm),
        in_specs=[row, prev, first] + [_const_spec(c.shape) for c in consts],
        out_specs=[row] * 6,
        out_shape=[jax.ShapeDtypeStruct((b, t, d), F32)] * 6,
        compiler_params=_params("parallel", "parallel"),
        name="rwkv_proj",
    )(x, x, shift0, *consts)


def _wkv_kernel(r_ref, lw_ref, k_ref, v_ref, a_ref, kk_ref, ka_ref, rk_ref, lg_ref, lb_ref, h0_ref,
                o_ref, hout_ref, h_scr, *, last_chunk):
    c = pl.program_id(1)
    nslab = h_scr.shape[0]
    C = CHUNK
    nseq, rows = r_ref.shape[:2]
    nslab = nslab // nseq

    @pl.when(c == 0)
    def _():
        h_scr[...] = jnp.zeros_like(h_scr)
        for n in range(nseq * nslab):
            for h in range(HEADS_PER_SLAB):
                hs = slice(h * HEAD, (h + 1) * HEAD)
                h_scr[n, hs, hs] = h0_ref[n // nslab, (n % nslab) * HEADS_PER_SLAB + h]

    row_s = lax.broadcasted_iota(jnp.int32, (SLAB, SLAB), 0)
    col_s = lax.broadcasted_iota(jnp.int32, (SLAB, SLAB), 1)
    mask_bd = (row_s >> HEAD_SHIFT) == (col_s >> HEAD_SHIFT)
    eye_s = row_s == col_s
    row_c = lax.broadcasted_iota(jnp.int32, (C, SLAB), 0)
    colin_c = lax.broadcasted_iota(jnp.int32, (C, SLAB), 1) & (C - 1)
    strict_lo = colin_c < row_c
    incl_lo = colin_c <= row_c
    eye_c = colin_c == row_c
    ones_bd = jnp.where(mask_bd, 1.0, 0.0).astype(BF16)

    def bd(x):
        xb = x.astype(BF16)
        return jnp.where(mask_bd, jnp.concatenate([xb] * HEADS_PER_SLAB, axis=0), jnp.zeros((), BF16))

    def head_sum(x):
        return _dot(x.astype(BF16), ones_bd)

    nck = max(rows // C, 1)
    live = min(rows, C)

    ncell = nseq * nck

    def chunk_rows(ref, ci, sl):
        x = ref[ci // nck, (ci % nck) * C:(ci % nck) * C + live, sl]
        if live < C:
            x = jnp.concatenate([x, jnp.zeros((C - live, x.shape[1]), x.dtype)], axis=0)
        return x

    lw_c = [chunk_rows(lw_ref, ci, slice(None)) for ci in range(ncell)]
    L_c = [_dot3(lw_c[ci], _tri(C, upper=False), dot=lambda p, t: _dot(t, p), pieces=2) for ci in range(ncell)]

    units = [(ci, s) for ci in range(ncell) for s in range(nslab)]
    S = range(len(units))
    slabs = [slice(s * SLAB, (s + 1) * SLAB) for _, s in units]
    r = [chunk_rows(r_ref, ci, slabs[u]) for u, (ci, _) in enumerate(units)]
    v = [chunk_rows(v_ref, ci, slabs[u]) for u, (ci, _) in enumerate(units)]
    asig = [chunk_rows(a_ref, ci, slabs[u]) for u, (ci, _) in enumerate(units)]
    k0 = [chunk_rows(k_ref, ci, slabs[u]) for u, (ci, _) in enumerate(units)]
    lw = [lw_c[ci][:, slabs[u]] for u, (ci, _) in enumerate(units)]
    kk = [k0[s] * kk_ref[:, slabs[s]] for s in S]
    k = [k0[s] * (1.0 + (asig[s] - 1.0) * ka_ref[:, slabs[s]]) for s in S]
    sums = [head_sum(jnp.concatenate([kk[s] * kk[s], r[s] * k[s] * rk_ref[:, slabs[s]]], axis=0)) for s in S]
    kk = [kk[s] / jnp.maximum(jnp.sqrt(sums[s][:C]), 1e-12) for s in S]
    bonus = [sums[s][C:] * v[s] for s in S]
    bv = [kk[s] * asig[s] for s in S]
    L = [L_c[ci][:, slabs[u]] for u, (ci, _) in enumerate(units)]
    LC = [L[s][C - 1:C, :] for s in S]
    enL = [jnp.exp(-L[s]) for s in S]
    eLc = [jnp.exp(LC[s] - L[s]) for s in S]
    At = [-kk[s] * jnp.exp(L[s] - lw[s]) for s in S]
    Rt = [r[s] * jnp.exp(L[s]) for s in S]
    Bt = [bv[s] * enL[s] for s in S]
    Kt = [k[s] * enL[s] for s in S]
    Bh = [bv[s] * eLc[s] for s in S]
    Kh = [k[s] * eLc[s] for s in S]

    lhs = [jnp.concatenate([At[s], Rt[s]], axis=0).astype(BF16) for s in S]
    ab = [_dot_nt(lhs[s], bd(Bt[s])) for s in S]
    ak = [_dot_nt(lhs[s], bd(Kt[s])) for s in S]
    N = [jnp.where(strict_lo, ab[s][:C], 0.0) for s in S]
    M = [jnp.where(strict_lo, ak[s][:C], 0.0) for s in S]
    Arb = [jnp.where(incl_lo, ab[s][C:], 0.0) for s in S]
    Ark = [jnp.where(incl_lo, ak[s][C:], 0.0) for s in S]
    mv = [_dot(jnp.concatenate([M[s], Ark[s]], axis=0).astype(BF16), bd(v[s])) for s in S]
    MV = [mv[s][:C] for s in S]

    X = [jnp.where(eye_c, 1.0, 0.0) + N[s] for s in S]
    A = [_dot(N[s].astype(BF16), bd(N[s])) for s in S]
    for _ in range(int(math.log2(C)) - 2):
        both = [_dot(jnp.concatenate([A[s], X[s]], axis=0).astype(BF16), bd(A[s])) for s in S]
        X = [X[s] + both[s][C:] for s in S]
        A = [both[s][:C] for s in S]
    X = [X[s] + _dot(X[s].astype(BF16), bd(A[s])) for s in S]

    wu = [_dot(X[s].astype(BF16), jnp.concatenate([bd(At[s]), bd(MV[s])], axis=1)) for s in S]
    WA = [wu[s][:, :SLAB] for s in S]
    UV = [wu[s][:, SLAB:] for s in S]
    au = [_dot(Arb[s].astype(BF16), jnp.concatenate([bd(WA[s]), bd(UV[s])], axis=1)) for s in S]
    Rp = [Rt[s] + au[s][:, :SLAB] for s in S]
    Yl = [au[s][:, SLAB:] + mv[s][C:] for s in S]
    rw = [jnp.concatenate([Rp[s], WA[s]], axis=0).astype(BF16) for s in S]
    bk = [jnp.concatenate([Bh[s], Kh[s]], axis=0).astype(BF16) for s in S]
    pc_col = [jnp.sum(jnp.where(eye_s, jnp.exp(LC[s]), 0.0), axis=1, keepdims=True) for s in S]

    h = [h_scr[n] for n in range(nseq * nslab)]
    y = [None] * len(units)
    for cj in range(nck):
        us = [((n // nslab) * nck + cj) * nslab + n % nslab for n in range(nseq * nslab)]
        yu = [_dot(rw[u], h[n].astype(BF16)) for n, u in enumerate(us)]
        for n, u in enumerate(us):
            y[u] = yu[n][:C] + Yl[u]
        uv = [jnp.concatenate([yu[n][C:] + UV[u], v[u]], axis=0).astype(BF16) for n, u in enumerate(us)]
        h = [h[n] * pc_col[u] + jnp.where(mask_bd, _dot_tn(bk[u], uv[n]), 0.0) for n, u in enumerate(us)]
    for n in range(nseq * nslab):
        h_scr[n] = h[n]

    mu = [head_sum(y[s]) * (1.0 / HEAD) for s in S]
    yc = [y[s] - mu[s] for s in S]
    var = [head_sum(yc[s] * yc[s]) * (1.0 / HEAD) for s in S]
    for u, (ci, _) in enumerate(units):
        sl = slabs[u]
        out = yc[u] * lax.rsqrt(var[u] + GN_EPS) * lg_ref[:, sl] + lb_ref[:, sl] + bonus[u]
        o_ref[ci // nck, (ci % nck) * C:(ci % nck) * C + live, sl] = out[:live]

    @pl.when(c == last_chunk)
    def _():
        for n in range(nseq * nslab):
            for h in range(HEADS_PER_SLAB):
                hs = slice(h * HEAD, (h + 1) * HEAD)
                hout_ref[n // nslab, (n % nslab) * HEADS_PER_SLAB + h] = h_scr[n, hs, hs]


def _wkv(r, lw, k, v, a, k_k, k_a, r_k, lnx_g, lnx_b, h0):
    b, t, d = r.shape
    if t % (WKV_CELLS_PER_STEP * CHUNK) == 0:
        nseq, rows = 1, WKV_CELLS_PER_STEP * CHUNK
    else:
        nseq, rows = (WKV_CELLS_PER_STEP if t <= CHUNK and b % WKV_CELLS_PER_STEP == 0 else 1), min(t, CHUNK)
    assert t % rows == 0 and rows % SUBLANES == 0
    nslab = d // SLAB
    heads = d // HEAD
    seq = pl.BlockSpec((nseq, rows, d), lambda i, c: (i, c, 0))
    vec = pl.BlockSpec((1, d), lambda i, c: (0, 0))
    hspec = pl.BlockSpec((nseq, heads, HEAD, HEAD), lambda i, c: (i, 0, 0, 0))
    return pl.pallas_call(
        functools.partial(_wkv_kernel, last_chunk=t // rows - 1),
        grid=(b // nseq, t // rows),
        in_specs=[seq] * 5 + [vec] * 5 + [hspec],
        out_specs=[seq, hspec],
        out_shape=[jax.ShapeDtypeStruct((b, t, d), F32),
                   jax.ShapeDtypeStruct((b, heads, HEAD, HEAD), F32)],
        scratch_shapes=[pltpu.VMEM((nseq * nslab, SLAB, SLAB), F32)],
        compiler_params=_params("parallel", "arbitrary"),
        name="wkv_chunked",
    )(r, lw, k, v, a, k_k, k_a, r_k, lnx_g, lnx_b, h0)


def _post_kernel(*refs, alpha, gated, ff_chunk):
    if gated:
        pre_ref, gate_ref, x_ref, wo_ref, w1_ref, w2_ref, ln_ref, out_ref = refs
        pre = pre_ref[...] * gate_ref[...]
    else:
        pre_ref, x_ref, wo_ref, w1_ref, w2_ref, ln_ref, out_ref = refs
        pre = pre_ref[...]
    h = _dot(pre.astype(BF16), wo_ref[...])
    x1 = _layer_norm(alpha * x_ref[...] + h, ln_ref[0:1, :], ln_ref[1:2, :])
    x1b = x1.astype(BF16)
    m = jnp.zeros_like(x1)
    for f in range(w1_ref.shape[1] // ff_chunk):
        fs = slice(f * ff_chunk, (f + 1) * ff_chunk)
        hid = jnp.maximum(_dot(x1b, w1_ref[:, fs]), 0.0)
        m = m + _dot((hid * hid).astype(BF16), w2_ref[fs, :])
    out_ref[...] = _layer_norm(alpha * x1 + m, ln_ref[2:3, :], ln_ref[3:4, :])


def _post(pre, gate, x, wo, w1, w2, ln, alpha, tm):
    n, d = x.shape
    tm = _row_tile(n, tm)
    row = pl.BlockSpec((tm, d), lambda i: (i, 0))
    acts = [pre, x] if gate is None else [pre, gate, x]
    consts = [wo, w1, w2, ln]
    return pl.pallas_call(
        functools.partial(_post_kernel, alpha=alpha, gated=gate is not None, ff_chunk=min(1024, w1.shape[1])),
        grid=(n // tm,),
        in_specs=[row] * len(acts) + [_const_spec(c.shape) for c in consts],
        out_specs=row,
        out_shape=jax.ShapeDtypeStruct((n, d), F32),
        compiler_params=_params("parallel"),
        name="post_mlp",
    )(*acts, *consts)


def _fox_proj_kernel(x_ref, wq_ref, wk_ref, wv_ref, wf_ref, bf_ref,
                     q_ref, k_ref, v_ref, kb_ref, vb_ref, lf_ref, *, channel_major):
    nb, rows, d = x_ref.shape
    xb = x_ref[...].reshape(nb * rows, d).astype(BF16)
    out = lambda z: z.reshape(nb, rows, z.shape[-1])
    q_ref[...] = out((_dot(xb, wq_ref[...]) * (HEAD ** -0.5 * LOG2E)).astype(BF16))
    k = _dot(xb, wk_ref[...])
    kb_ref[...] = out(k.astype(BF16))
    if channel_major:
        k_ref[0] = k.T
        v = _dot_nt(wv_ref[...], xb)
        v_ref[0] = v
        vb_ref[0] = v.astype(BF16)
    else:
        v = _dot(xb, wv_ref[...])
        k_ref[...] = out(k)
        v_ref[...] = out(v)
        vb_ref[...] = out(v.astype(BF16))
    f = _dot(xb, wf_ref[...]) + bf_ref[...]
    lf_ref[...] = out(jnp.minimum(f, 0.0) - jnp.log1p(jnp.exp(-jnp.abs(f))))


def _fox_proj(x, wq, wk, wv, wf, bf, tm, channel_major):
    b, t, d = x.shape
    nb, tm = _seq_tile(b, t, tm)
    assert nb == 1 or not channel_major
    row = pl.BlockSpec((nb, tm, d), lambda bi, i: (bi, i, 0))
    rowf = pl.BlockSpec((nb, tm, LANES), lambda bi, i: (bi, i, 0))
    kv_spec = pl.BlockSpec((1, d, tm), lambda bi, i: (bi, 0, i)) if channel_major else row
    kv_shape = jax.ShapeDtypeStruct((b, d, t) if channel_major else (b, t, d), F32)
    act_bf = jax.ShapeDtypeStruct((b, t, d), BF16)
    vb_shape = jax.ShapeDtypeStruct(kv_shape.shape, BF16)
    consts = [wq, wk, wv, wf, bf]
    return pl.pallas_call(
        functools.partial(_fox_proj_kernel, channel_major=channel_major),
        grid=(b // nb, t // tm),
        in_specs=[row] + [_const_spec(c.shape) for c in consts],
        out_specs=[row, kv_spec, kv_spec, row, kv_spec, rowf],
        out_shape=[act_bf, kv_shape, kv_shape, act_bf, vb_shape, jax.ShapeDtypeStruct((b, t, LANES), F32)],
        compiler_params=_params("parallel", "parallel"),
        name="fox_proj",
    )(x, *consts)


CS_TILE = 512
SUM_ROWS = 16


def _cumsum_kernel(lf_ref, c_ref, carry_ref):
    @pl.when(pl.program_id(1) == 0)
    def _():
        carry_ref[...] = jnp.zeros_like(carry_ref)

    n = lf_ref.shape[1]
    cs = _dot3(lf_ref[0], _tri(n, upper=False), dot=lambda p, t: _dot(t, p)) + carry_ref[0:1, :]
    c_ref[0] = cs
    carry_ref[...] = jnp.broadcast_to(cs[n - 1:n, :], carry_ref.shape)


def _cumsum(lf):
    b, l, w = lf.shape
    tile = _row_tile(l, CS_TILE)
    spec = pl.BlockSpec((1, tile, w), lambda i, j: (i, j, 0))
    return pl.pallas_call(
        _cumsum_kernel,
        grid=(b, l // tile),
        in_specs=[spec],
        out_specs=spec,
        out_shape=jax.ShapeDtypeStruct((b, l, w), F32),
        scratch_shapes=[pltpu.VMEM((SUBLANES, w), F32)],
        compiler_params=_params("parallel", "arbitrary"),
        name="logf_cumsum",
    )(lf)


def _bias_lanes(c, head0, width, key_side):
    nh = width // HEAD
    src = lax.broadcasted_iota(jnp.int32, (LANES, width), 0)
    dst = lax.broadcasted_iota(jnp.int32, (LANES, width), 1)
    lane = lax.broadcasted_iota(jnp.int32, (1, width), 1)
    lo, sign = (3, -1.0) if key_side else (0, 1.0)
    base = [((hh + 1) % nh) * HEAD for hh in range(nh)]
    ones = functools.reduce(jnp.logical_or, [(lane >= bs + 3 - lo) & (lane < bs + 6 - lo) for bs in base])
    out = jnp.where(ones, 1.0, 0.0)
    for n, p in enumerate(_split3(c * LOG2E)):
        hit = functools.reduce(jnp.logical_or,
                               [(src == head0 + hh) & (dst == base[hh] + lo + n) for hh in range(nh)])
        out = out + _dot(p, jnp.where(hit, sign, 0.0).astype(BF16))
    return out


def _flash_kernel(q_ref, k_ref, vt_ref, c_ref, o_ref, kaug_scr, *, tq):
    g = pl.program_id(1)
    i = pl.program_id(2)
    t = k_ref.shape[1]
    width = k_ref.shape[2]
    nh = width // HEAD
    lane = lax.broadcasted_iota(jnp.int32, (1, width), 1)
    in_head = [(lane >> HEAD_SHIFT) == hh for hh in range(nh)]
    bias_lanes = [(lane >= ((hh + 1) % nh) * HEAD) & (lane < ((hh + 1) % nh) * HEAD + 6) for hh in range(nh)]

    def augment(x, bias):
        bias = bias.astype(BF16)
        return [jnp.where(in_head[hh], x, jnp.where(bias_lanes[hh], bias, jnp.zeros((), BF16)))
                for hh in range(nh)]

    @pl.when(i == 0)
    def _():
        for blk in range(t // tq):
            rows = slice(blk * tq, (blk + 1) * tq)
            ka = augment(k_ref[0, rows, :], _bias_lanes(c_ref[0, rows, :], nh * g, width, key_side=True))
            for hh in range(nh):
                kaug_scr[hh, rows, :] = ka[hh]

    row0 = pl.multiple_of(i * tq, tq)
    qa = augment(q_ref[0], _bias_lanes(c_ref[0, pl.ds(row0, tq), :], nh * g, width, key_side=False))

    qg = min(tq, SLAB)
    chains = [(hh, qs) for hh in range(nh) for qs in range(0, tq, qg)]
    nc = len(chains)
    qa_c = [qa[hh][qs:qs + qg] for hh, qs in chains]

    ones_rows = jnp.ones((SUM_ROWS, tq), BF16)

    def step(off, carry, mask):
        ks = [kaug_scr[hh, pl.ds(off, tq), :] for hh in range(nh)]
        vts = [jnp.concatenate([vt_ref[0, hh * HEAD:(hh + 1) * HEAD, pl.ds(off, tq)], ones_rows], axis=0)
               for hh in range(nh)]
        kext = [tq if mask is None else qs + qg for _, qs in chains]
        st = [_dot_nt(ks[hh][:kext[n]], qa_c[n]) for n, (hh, _) in enumerate(chains)]
        if mask is not None:
            st = [jnp.where(mask[:kext[n], qs:qs + qg], st[n], -jnp.inf) for n, (_, qs) in enumerate(chains)]
        m_new = [jnp.maximum(carry[n][0], jnp.max(st[n], axis=0, keepdims=True)) for n in range(nc)]
        alpha = [jnp.exp2(carry[n][0] - m_new[n]) for n in range(nc)]
        p = [jnp.exp2(st[n] - m_new[n]) for n in range(nc)]
        acc = [alpha[n] * carry[n][1] + _dot(vts[hh][:, :kext[n]], p[n].astype(BF16))
               for n, (hh, _) in enumerate(chains)]
        return tuple(zip(m_new, acc))

    init = tuple((jnp.full((1, qg), -jnp.inf, F32), jnp.zeros((HEAD + SUM_ROWS, qg), F32)) for _ in chains)
    carry = lax.fori_loop(0, i, lambda j, c: step(pl.multiple_of(j * tq, tq), c, None), init)
    causal = lax.broadcasted_iota(jnp.int32, (tq, tq), 0) <= lax.broadcasted_iota(jnp.int32, (tq, tq), 1)
    carry = step(row0, carry, causal)
    per_head = [jnp.concatenate([carry[n][1][:HEAD] / carry[n][1][HEAD:HEAD + 1]
                                 for n, (h2, _) in enumerate(chains) if h2 == hh], axis=1) for hh in range(nh)]
    o_ref[0] = jnp.concatenate(per_head, axis=0).T


def _flash(q, k, vt, c, tq, width):
    b, t, d = q.shape
    tq = _row_tile(t, tq)
    return pl.pallas_call(
        functools.partial(_flash_kernel, tq=tq),
        grid=(b, d // width, t // tq),
        in_specs=[pl.BlockSpec((1, tq, width), lambda bi, gi, i: (bi, i, gi)),
                  pl.BlockSpec((1, t, width), lambda bi, gi, i: (bi, 0, gi)),
                  pl.BlockSpec((1, width, t), lambda bi, gi, i: (bi, gi, 0)),
                  pl.BlockSpec((1, t, LANES), lambda bi, gi, i: (bi, 0, 0))],
        out_specs=pl.BlockSpec((1, tq, width), lambda bi, gi, i: (bi, i, gi)),
        out_shape=jax.ShapeDtypeStruct((b, t, d), F32),
        scratch_shapes=[pltpu.VMEM((width // HEAD, t, width), BF16)],
        compiler_params=_params("parallel", "parallel", "arbitrary"),
        name="fox_flash",
    )(q, k, vt, c)


DEC_TILE = 512
DEC_WIDTH = 512


def _decode_attn_kernel(q_ref, ckt_ref, cvt_ref, kn_ref, vn_ref, clf_ref, lfn_ref, o_ref, c_scr):
    g = pl.program_id(1)
    past = ckt_ref.shape[2]
    t = q_ref.shape[1]

    @pl.when(g == 0)
    def _():
        tri = _tri(DEC_TILE, upper=True)
        carry = jnp.zeros((clf_ref.shape[1], 1), F32)
        for blk in range(past // DEC_TILE):
            cols = slice(blk * DEC_TILE, (blk + 1) * DEC_TILE)
            cs = _dot3(clf_ref[0, :, cols], tri) + carry
            c_scr[:, cols] = cs
            carry = cs[:, DEC_TILE - 1:DEC_TILE]

    width = q_ref.shape[2]
    nh = width // HEAD
    lane = lax.broadcasted_iota(jnp.int32, (1, width), 1)
    hlane = lax.broadcasted_iota(jnp.int32, (1, LANES), 1)
    q = q_ref[0]
    kct = ckt_ref[0].astype(BF16)
    vct = cvt_ref[0].astype(BF16)
    kn = kn_ref[0]
    vn = vn_ref[0]
    causal = lax.broadcasted_iota(jnp.int32, (t, t), 1) <= lax.broadcasted_iota(jnp.int32, (t, t), 0)
    lf_new = _dot3(lfn_ref[0], _tri(t, upper=False), dot=lambda p, tr: _dot(tr, p))

    H = range(nh)
    head = [nh * g + hh for hh in H]
    in_head = [(lane >> HEAD_SHIFT) == hh for hh in H]
    rows = lambda parts: jnp.concatenate(parts, axis=0)
    qs = rows([jnp.where(in_head[hh], q, jnp.zeros((), BF16)) for hh in H])
    c_cache = [c_scr[pl.ds(head[hh], 1), :] for hh in H]
    total = [c_cache[hh][:, past - 1:past] for hh in H]
    onehot = [jnp.where(hlane == head[hh], 1.0, 0.0) for hh in H]
    cn_col = [jnp.sum(lf_new * onehot[hh], axis=-1, keepdims=True) + total[hh] for hh in H]
    cn_row = [_dot3(lf_new, jnp.broadcast_to(onehot[hh], (SUBLANES, LANES)).astype(BF16),
                    dot=lambda p, e: _dot_nt(e, p))[0:1] + total[hh] for hh in H]
    s1 = _dot(qs, kct) + rows([cn_col[hh] - c_cache[hh] for hh in H]) * LOG2E
    s2 = _dot_nt(qs, kn) + rows([cn_col[hh] - cn_row[hh] for hh in H]) * LOG2E
    s2 = jnp.where(rows([causal] * nh), s2, -jnp.inf)
    m = jnp.maximum(jnp.max(s1, axis=-1, keepdims=True), jnp.max(s2, axis=-1, keepdims=True))
    e1 = jnp.exp2(s1 - m)
    e2 = jnp.exp2(s2 - m)
    l = jnp.sum(e1, axis=-1, keepdims=True) + jnp.sum(e2, axis=-1, keepdims=True)
    o = (_dot_nt(e1.astype(BF16), vct) + _dot(e2.astype(BF16), vn)) / l
    out = o[(nh - 1) * t:]
    for hh in range(nh - 2, -1, -1):
        out = jnp.where(in_head[hh], o[hh * t:(hh + 1) * t], out)
    o_ref[0] = out


def _decode_attn(q, cache_kt, cache_vt, kn, vn, cache_lf, lf_new):
    b, t, d = q.shape
    p = cache_kt.shape[2]
    h = cache_lf.shape[1]
    assert p % DEC_TILE == 0
    new = pl.BlockSpec((1, t, DEC_WIDTH), lambda bi, gi: (bi, 0, gi))
    old = pl.BlockSpec((1, DEC_WIDTH, p), lambda bi, gi: (bi, gi, 0))
    return pl.pallas_call(
        _decode_attn_kernel,
        grid=(b, d // DEC_WIDTH),
        in_specs=[new, old, old, new, new,
                  pl.BlockSpec((1, h, p), lambda bi, gi: (bi, 0, 0)),
                  pl.BlockSpec((1, t, LANES), lambda bi, gi: (bi, 0, 0))],
        out_specs=new,
        out_shape=jax.ShapeDtypeStruct((b, t, d), F32),
        scratch_shapes=[pltpu.VMEM((h, p), F32)],
        compiler_params=_params("parallel", "arbitrary"),
        name="fox_decode_attn",
    )(q, cache_kt, cache_vt, kn, vn, cache_lf, lf_new)


def _pad_cols(w, mult):
    pad = (-w.shape[-1]) % mult
    return jnp.pad(w, ((0, 0), (0, pad))) if pad else w


def _pad_rows(w, mult):
    pad = (-w.shape[0]) % mult
    return jnp.pad(w, ((0, pad), (0, 0))) if pad else w


def kernel(x_prompt, x_sample, state_wkv, state_shift, cache_k, cache_v, cache_logf, rwkv_mu, rwkv_w0, rwkv_w1, rwkv_w2, rwkv_a0, rwkv_a1, rwkv_a2, rwkv_g1, rwkv_g2, rwkv_k_k, rwkv_k_a, rwkv_r_k, rwkv_w_r, rwkv_w_k, rwkv_w_v, rwkv_w_o, rwkv_lnx_g, rwkv_lnx_b, fox_w_in, fox_b_f, fox_w_o, ffn_w1, ffn_w2, ln_mix_g, ln_mix_b, ln_ffn_g, ln_ffn_b):
    depth = ln_mix_g.shape[0]
    alpha = (2 * depth) ** 0.25
    bp, tp, d = x_prompt.shape
    bs, ts, _ = x_sample.shape
    heads = d // HEAD
    past = cache_k.shape[2]
    tm = 512

    xs_by_group = {"p": x_prompt, "s": x_sample}
    outs = {grp: {n: [] for n in ("wkv", "shift", "k", "v", "lf")} for grp in ("p", "s")}
    for i in range(depth):
        j = i // 2
        ln = jnp.stack([ln_mix_g[i], ln_mix_b[i], ln_ffn_g[i], ln_ffn_b[i]])
        w1 = ffn_w1[i].astype(BF16)
        w2 = ffn_w2[i].astype(BF16)
        if i % 2 == 0:
            row = lambda z: z.reshape(1, d)
            proj = (rwkv_mu[j], row(rwkv_w0[j]), row(rwkv_a0[j]),
                    rwkv_w_r[j].astype(BF16), rwkv_w_k[j].astype(BF16), rwkv_w_v[j].astype(BF16),
                    _pad_cols(rwkv_w1[j], LANES).astype(BF16), _pad_rows(rwkv_w2[j], LANES).astype(BF16),
                    _pad_cols(rwkv_a1[j], LANES).astype(BF16), _pad_rows(rwkv_a2[j], LANES).astype(BF16),
                    _pad_cols(rwkv_g1[j], LANES).astype(BF16), _pad_rows(rwkv_g2[j], LANES).astype(BF16))
            wkv_vecs = (row(rwkv_k_k[j]), row(rwkv_k_a[j]), row(rwkv_r_k[j]), row(rwkv_lnx_g[j]), row(rwkv_lnx_b[j]))
            wo = rwkv_w_o[j].astype(BF16)
            starts = {"p": (jnp.zeros((bp, d), x_prompt.dtype), jnp.zeros((bp, heads, HEAD, HEAD), state_wkv.dtype)),
                      "s": (state_shift[j], state_wkv[j])}
            for grp in ("p", "s"):
                x = xs_by_group[grp]
                b, t, _ = x.shape
                shift0, wkv0 = starts[grp]
                r, lw, k, v, a, gate = _rwkv_proj(x, shift0.reshape(b, 1, d), *proj, tm=tm)
                h0 = jnp.swapaxes(wkv0.astype(F32), -1, -2)
                o, hfin = _wkv(r, lw, k, v, a, *wkv_vecs, h0)
                outs[grp]["wkv"].append(jnp.swapaxes(hfin, -1, -2).astype(wkv0.dtype))
                outs[grp]["shift"].append(x[:, -1])
                xs_by_group[grp] = _post(o.reshape(b * t, d), gate.reshape(b * t, d), x.reshape(b * t, d),
                                         wo, w1, w2, ln, alpha, tm).reshape(b, t, d)
        else:
            w_in = fox_w_in[j]
            wq = w_in[:, :d].astype(BF16)
            wk = w_in[:, d:2 * d].astype(BF16)
            wv = w_in[:, 2 * d:3 * d].astype(BF16)
            wf = _pad_cols(w_in[:, 3 * d:], LANES).astype(BF16)
            bf = _pad_cols(fox_b_f[j].reshape(1, heads), LANES)
            wo = fox_w_o[j].astype(BF16)

            xp = xs_by_group["p"]
            q, kt, vt, kb, vtb, lf = _fox_proj(xp, wq, wk, wv.T, wf, bf, tm, channel_major=True)
            o = _flash(q, kb, vtb, _cumsum(lf), tq=512, width=SLAB)
            to_heads = lambda z: jnp.transpose(z.reshape(bp, heads, HEAD, tp), (0, 3, 1, 2))
            outs["p"]["k"].append(to_heads(kt))
            outs["p"]["v"].append(to_heads(vt))
            outs["p"]["lf"].append(lf[:, :, :heads])
            xs_by_group["p"] = _post(o.reshape(bp * tp, d), None, xp.reshape(bp * tp, d),
                                     wo, w1, w2, ln, alpha, tm).reshape(bp, tp, d)

            xs = xs_by_group["s"]
            q, k, v, kb, vb, lf = _fox_proj(xs, wq, wk, wv, wf, bf, tm, channel_major=False)
            channel_major = lambda z: jnp.transpose(z, (0, 2, 3, 1)).reshape(bs, d, past)
            o = _decode_attn(q, channel_major(cache_k[j]), channel_major(cache_v[j]), kb, vb,
                             jnp.transpose(cache_logf[j].astype(F32), (0, 2, 1)), lf)
            outs["s"]["k"].append(k.reshape(bs, ts, heads, HEAD))
            outs["s"]["v"].append(v.reshape(bs, ts, heads, HEAD))
            outs["s"]["lf"].append(lf[:, :, :heads])
            xs_by_group["s"] = _post(o.reshape(bs * ts, d), None, xs.reshape(bs * ts, d),
                                     wo, w1, w2, ln, alpha, tm).reshape(bs, ts, d)

    st = jnp.stack
    op, os_ = outs["p"], outs["s"]
    return (xs_by_group["p"], xs_by_group["s"],
            st(op["wkv"]), st(op["shift"]), st(op["k"]), st(op["v"]), st(op["lf"]),
            st(os_["wkv"]), st(os_["shift"]), st(os_["k"]), st(os_["v"]), st(os_["lf"]))
```

```python
import functools
import math

import jax
import jax.numpy as jnp
from jax import lax
from jax.experimental import pallas as pl
from jax.experimental.pallas import tpu as pltpu

BF16 = jnp.bfloat16
F32 = jnp.float32

HEAD = 64
HEAD_SHIFT = 6
LANES = 128
SUBLANES = 8
SLAB = 256
HEADS_PER_SLAB = SLAB // HEAD
CHUNK = 64
WKV_CELLS_PER_STEP = 2
LN_EPS = 1e-5
GN_EPS = 64e-5
EXP_M05 = math.exp(-0.5)
LOG2E = math.log2(math.e)
VMEM_LIMIT = 56 * 1024 * 1024


def _dot(a, b):
    return jnp.dot(a, b, preferred_element_type=F32)


def _dot_nt(a, b):
    return lax.dot_general(a, b, (((1,), (1,)), ((), ())), preferred_element_type=F32)


def _dot_tn(a, b):
    return lax.dot_general(a, b, (((0,), (0,)), ((), ())), preferred_element_type=F32)


def _split3(x):
    p1 = x.astype(BF16)
    r1 = x - p1.astype(F32)
    p2 = r1.astype(BF16)
    p3 = (r1 - p2.astype(F32)).astype(BF16)
    return p1, p2, p3


def _dot3(a, b, dot=_dot, pieces=3):
    return sum(dot(p, b) for p in _split3(a)[:pieces])


def _tri(n, upper):
    r = lax.broadcasted_iota(jnp.int32, (n, n), 0)
    c = lax.broadcasted_iota(jnp.int32, (n, n), 1)
    return jnp.where((r <= c) if upper else (c <= r), 1.0, 0.0).astype(BF16)


def _layer_norm(z, g, b):
    mu = jnp.mean(z, axis=-1, keepdims=True)
    zc = z - mu
    var = jnp.mean(zc * zc, axis=-1, keepdims=True)
    return zc * lax.rsqrt(var + LN_EPS) * g + b


def _params(*sem):
    return pltpu.CompilerParams(dimension_semantics=sem, vmem_limit_bytes=VMEM_LIMIT)


def _const_spec(shape):
    nd = len(shape)
    return pl.BlockSpec(shape, lambda *_: (0,) * nd, pipeline_mode=pl.Buffered(1))


def _row_tile(n, want):
    t = min(n, want)
    assert n % t == 0 and t % SUBLANES == 0, (n, t)
    return t


def _seq_tile(b, t, want):
    tm = _row_tile(t, want)
    nb = min(b, max(want // t, 1)) if tm == t else 1
    assert b % nb == 0, (b, nb)
    return nb, tm


def _rwkv_proj_kernel(x_ref, prev_ref, first_ref, mu_ref, w0_ref, a0_ref, wr_ref, wk_ref, wv_ref,
                      w1_ref, w2_ref, a1_ref, a2_ref, g1_ref, g2_ref,
                      r_ref, lw_ref, k_ref, v_ref, a_ref, g_ref):
    nb, rows, d = x_ref.shape
    x = x_ref[...].reshape(nb * rows, d)
    row_in_seq = lax.broadcasted_iota(jnp.int32, x.shape, 0) % rows
    if nb == 1:
        row0 = jnp.where(pl.program_id(1) == 0, first_ref[0], prev_ref[0, SUBLANES - 1:SUBLANES, :])
    else:
        row0 = jnp.broadcast_to(first_ref[...], (nb, rows, d)).reshape(nb * rows, d)
    xx = jnp.where(row_in_seq == 0, row0, pltpu.roll(x, 1, 0)) - x
    out = lambda z: z.reshape(nb, rows, d)

    def mix(i):
        return (x + xx * mu_ref[i:i + 1, :]).astype(BF16)

    r_ref[...] = out(_dot(mix(0), wr_ref[...]))
    w_raw = w0_ref[...] + _dot(jnp.tanh(_dot(mix(1), w1_ref[...])).astype(BF16), w2_ref[...])
    lw_ref[...] = out(-EXP_M05 * jax.nn.sigmoid(w_raw))
    k_ref[...] = out(_dot(mix(2), wk_ref[...]))
    v_ref[...] = out(_dot(mix(3), wv_ref[...]))
    a_ref[...] = out(jax.nn.sigmoid(a0_ref[...] + _dot(_dot(mix(4), a1_ref[...]).astype(BF16), a2_ref[...])))
    g_ref[...] = out(_dot(jax.nn.sigmoid(_dot(mix(5), g1_ref[...])).astype(BF16), g2_ref[...]))


def _rwkv_proj(x, shift0, mu, w0, a0, wr, wk, wv, w1, w2, a1, a2, g1, g2, tm):
    b, t, d = x.shape
    nb, tm = _seq_tile(b, t, tm)
    per_tile = tm // SUBLANES
    row = pl.BlockSpec((nb, tm, d), lambda bi, i: (bi, i, 0))
    prev = pl.BlockSpec((1, SUBLANES, d), lambda bi, i: (bi * nb, jnp.maximum(i * per_tile - 1, 0), 0))
    first = pl.BlockSpec((nb, 1, d), lambda bi, i: (bi, 0, 0))
    consts = [mu, w0, a0, wr, wk, wv, w1, w2, a1, a2, g1, g2]
    return pl.pallas_call(
        _rwkv_proj_kernel,
        grid=(b // nb, t // tm),
        in_specs=[row, prev, first] + [_const_spec(c.shape) for c in consts],
        out_specs=[row] * 6,
        out_shape=[jax.ShapeDtypeStruct((b, t, d), F32)] * 6,
        compiler_params=_params("parallel", "parallel"),
        name="rwkv_proj",
    )(x, x, shift0, *consts)


def _wkv_kernel(r_ref, lw_ref, k_ref, v_ref, a_ref, kk_ref, ka_ref, rk_ref, lg_ref, lb_ref, h0_ref,
                o_ref, hout_ref, h_scr, *, last_chunk):
    c = pl.program_id(1)
    nslab = h_scr.shape[0]
    C = CHUNK
    nseq, rows = r_ref.shape[:2]
    nslab = nslab // nseq

    @pl.when(c == 0)
    def _():
        h_scr[...] = jnp.zeros_like(h_scr)
        for n in range(nseq * nslab):
            for h in range(HEADS_PER_SLAB):
                hs = slice(h * HEAD, (h + 1) * HEAD)
                h_scr[n, hs, hs] = h0_ref[n // nslab, (n % nslab) * HEADS_PER_SLAB + h]

    row_s = lax.broadcasted_iota(jnp.int32, (SLAB, SLAB), 0)
    col_s = lax.broadcasted_iota(jnp.int32, (SLAB, SLAB), 1)
    mask_bd = (row_s >> HEAD_SHIFT) == (col_s >> HEAD_SHIFT)
    row_c = lax.broadcasted_iota(jnp.int32, (C, SLAB), 0)
    colin_c = lax.broadcasted_iota(jnp.int32, (C, SLAB), 1) & (C - 1)
    strict_lo = colin_c < row_c
    incl_lo = colin_c <= row_c
    eye_c = colin_c == row_c
    ones_bd = jnp.where(mask_bd, 1.0, 0.0).astype(BF16)

    def bd(x):
        xb = x.astype(BF16)
        return jnp.where(mask_bd, jnp.concatenate([xb] * HEADS_PER_SLAB, axis=0), jnp.zeros((), BF16))

    def head_sum(x):
        return _dot(x.astype(BF16), ones_bd)

    nck = max(rows // C, 1)
    live = min(rows, C)

    ncell = nseq * nck

    def chunk_rows(ref, ci, sl):
        x = ref[ci // nck, (ci % nck) * C:(ci % nck) * C + live, sl]
        if live < C:
            x = jnp.concatenate([x, jnp.zeros((C - live, x.shape[1]), x.dtype)], axis=0)
        return x

    lw_c = [chunk_rows(lw_ref, ci, slice(None)) for ci in range(ncell)]
    L_c = [_dot3(lw_c[ci], _tri(C, upper=False), dot=lambda p, t: _dot(t, p), pieces=2) for ci in range(ncell)]

    units = [(ci, s) for ci in range(ncell) for s in range(nslab)]
    S = range(len(units))
    slabs = [slice(s * SLAB, (s + 1) * SLAB) for _, s in units]
    r = [chunk_rows(r_ref, ci, slabs[u]) for u, (ci, _) in enumerate(units)]
    v = [chunk_rows(v_ref, ci, slabs[u]) for u, (ci, _) in enumerate(units)]
    asig = [chunk_rows(a_ref, ci, slabs[u]) for u, (ci, _) in enumerate(units)]
    k0 = [chunk_rows(k_ref, ci, slabs[u]) for u, (ci, _) in enumerate(units)]
    lw = [lw_c[ci][:, slabs[u]] for u, (ci, _) in enumerate(units)]
    kk = [k0[s] * kk_ref[:, slabs[s]] for s in S]
    k = [k0[s] * (1.0 + (asig[s] - 1.0) * ka_ref[:, slabs[s]]) for s in S]
    sums = [head_sum(jnp.concatenate([kk[s] * kk[s], r[s] * k[s] * rk_ref[:, slabs[s]]], axis=0)) for s in S]
    kk = [kk[s] / jnp.maximum(jnp.sqrt(sums[s][:C]), 1e-12) for s in S]
    bonus = [sums[s][C:] * v[s] for s in S]
    bv = [kk[s] * asig[s] for s in S]
    L = [L_c[ci][:, slabs[u]] for u, (ci, _) in enumerate(units)]
    LC = [L[s][C - 1:C, :] for s in S]
    enL = [jnp.exp(-L[s]) for s in S]
    eLc = [jnp.exp(LC[s] - L[s]) for s in S]
    At = [-kk[s] * jnp.exp(L[s] - lw[s]) for s in S]
    Rt = [r[s] * jnp.exp(L[s]) for s in S]
    Bt = [bv[s] * enL[s] for s in S]
    Kt = [k[s] * enL[s] for s in S]
    Bh = [bv[s] * eLc[s] for s in S]
    Kh = [k[s] * eLc[s] for s in S]

    lhs = [jnp.concatenate([At[s], Rt[s]], axis=0).astype(BF16) for s in S]
    ab = [_dot_nt(lhs[s], bd(Bt[s])) for s in S]
    ak = [_dot_nt(lhs[s], bd(Kt[s])) for s in S]
    N = [jnp.where(strict_lo, ab[s][:C], 0.0) for s in S]
    M = [jnp.where(strict_lo, ak[s][:C], 0.0) for s in S]
    Arb = [jnp.where(incl_lo, ab[s][C:], 0.0) for s in S]
    Ark = [jnp.where(incl_lo, ak[s][C:], 0.0) for s in S]
    mv = [_dot(jnp.concatenate([M[s], Ark[s]], axis=0).astype(BF16), bd(v[s])) for s in S]
    MV = [mv[s][:C] for s in S]

    X = [jnp.where(eye_c, 1.0, 0.0) + N[s] for s in S]
    A = [_dot(N[s].astype(BF16), bd(N[s])) for s in S]
    for _ in range(int(math.log2(C)) - 2):
        both = [_dot(jnp.concatenate([A[s], X[s]], axis=0).astype(BF16), bd(A[s])) for s in S]
        X = [X[s] + both[s][C:] for s in S]
        A = [both[s][:C] for s in S]
    X = [X[s] + _dot(X[s].astype(BF16), bd(A[s])) for s in S]

    wu = [_dot(X[s].astype(BF16), jnp.concatenate([bd(At[s]), bd(MV[s])], axis=1)) for s in S]
    WA = [wu[s][:, :SLAB] for s in S]
    UV = [wu[s][:, SLAB:] for s in S]
    au = [_dot(Arb[s].astype(BF16), jnp.concatenate([bd(WA[s]), bd(UV[s])], axis=1)) for s in S]
    Rp = [Rt[s] + au[s][:, :SLAB] for s in S]
    Yl = [au[s][:, SLAB:] + mv[s][C:] for s in S]
    rw = [jnp.concatenate([Rp[s], WA[s]], axis=0).astype(BF16) for s in S]
    bk = [jnp.concatenate([Bh[s], Kh[s]], axis=0).astype(BF16) for s in S]
    pc = [jnp.exp(LC[s]) for s in S]

    h = [h_scr[n] for n in range(nseq * nslab)]
    y = [None] * len(units)
    for cj in range(nck):
        us = [((n // nslab) * nck + cj) * nslab + n % nslab for n in range(nseq * nslab)]
        yu = [_dot_nt(rw[u], h[n].astype(BF16)) for n, u in enumerate(us)]
        for n, u in enumerate(us):
            y[u] = yu[n][:C] + Yl[u]
        uv = [jnp.concatenate([yu[n][C:] + UV[u], v[u]], axis=0).astype(BF16) for n, u in enumerate(us)]
        h = [h[n] * pc[u] + jnp.where(mask_bd, _dot_tn(uv[n], bk[u]), 0.0) for n, u in enumerate(us)]
    for n in range(nseq * nslab):
        h_scr[n] = h[n]

    mu = [head_sum(y[s]) * (1.0 / HEAD) for s in S]
    yc = [y[s] - mu[s] for s in S]
    var = [head_sum(yc[s] * yc[s]) * (1.0 / HEAD) for s in S]
    for u, (ci, _) in enumerate(units):
        sl = slabs[u]
        out = yc[u] * lax.rsqrt(var[u] + GN_EPS) * lg_ref[:, sl] + lb_ref[:, sl] + bonus[u]
        o_ref[ci // nck, (ci % nck) * C:(ci % nck) * C + live, sl] = out[:live]

    @pl.when(c == last_chunk)
    def _():
        for n in range(nseq * nslab):
            for h in range(HEADS_PER_SLAB):
                hs = slice(h * HEAD, (h + 1) * HEAD)
                hout_ref[n // nslab, (n % nslab) * HEADS_PER_SLAB + h] = h_scr[n, hs, hs]


def _wkv(r, lw, k, v, a, k_k, k_a, r_k, lnx_g, lnx_b, h0):
    b, t, d = r.shape
    if t % (WKV_CELLS_PER_STEP * CHUNK) == 0:
        nseq, rows = 1, WKV_CELLS_PER_STEP * CHUNK
    else:
        nseq, rows = (WKV_CELLS_PER_STEP if t <= CHUNK and b % WKV_CELLS_PER_STEP == 0 else 1), min(t, CHUNK)
    assert t % rows == 0 and rows % SUBLANES == 0
    nslab = d // SLAB
    heads = d // HEAD
    seq = pl.BlockSpec((nseq, rows, d), lambda i, c: (i, c, 0))
    vec = pl.BlockSpec((1, d), lambda i, c: (0, 0))
    hspec = pl.BlockSpec((nseq, heads, HEAD, HEAD), lambda i, c: (i, 0, 0, 0))
    return pl.pallas_call(
        functools.partial(_wkv_kernel, last_chunk=t // rows - 1),
        grid=(b // nseq, t // rows),
        in_specs=[seq] * 5 + [vec] * 5 + [hspec],
        out_specs=[seq, hspec],
        out_shape=[jax.ShapeDtypeStruct((b, t, d), F32),
                   jax.ShapeDtypeStruct((b, heads, HEAD, HEAD), F32)],
        scratch_shapes=[pltpu.VMEM((nseq * nslab, SLAB, SLAB), F32)],
        compiler_params=_params("parallel", "arbitrary"),
        name="wkv_chunked",
    )(r, lw, k, v, a, k_k, k_a, r_k, lnx_g, lnx_b, h0)


def _post_kernel(*refs, alpha, gated, ff_chunk):
    if gated:
        pre_ref, gate_ref, x_ref, wo_ref, w1_ref, w2_ref, ln_ref, out_ref = refs
        pre = pre_ref[...] * gate_ref[...]
    else:
        pre_ref, x_ref, wo_ref, w1_ref, w2_ref, ln_ref, out_ref = refs
        pre = pre_ref[...]
    h = _dot(pre.astype(BF16), wo_ref[...])
    x1 = _layer_norm(alpha * x_ref[...] + h, ln_ref[0:1, :], ln_ref[1:2, :])
    x1b = x1.astype(BF16)
    m = jnp.zeros_like(x1)
    for f in range(w1_ref.shape[1] // ff_chunk):
        fs = slice(f * ff_chunk, (f + 1) * ff_chunk)
        hid = jnp.maximum(_dot(x1b, w1_ref[:, fs]), 0.0)
        m = m + _dot((hid * hid).astype(BF16), w2_ref[fs, :])
    out_ref[...] = _layer_norm(alpha * x1 + m, ln_ref[2:3, :], ln_ref[3:4, :])


def _post(pre, gate, x, wo, w1_all, w2_all, layer, ln, alpha, tm):
    n, d = x.shape
    tm = _row_tile(n, tm)
    row = pl.BlockSpec((tm, d), lambda i: (i, 0))
    acts = [pre, x] if gate is None else [pre, gate, x]
    layer_spec = lambda w: pl.BlockSpec((None,) + w.shape[1:], lambda i: (layer, 0, 0), pipeline_mode=pl.Buffered(1))
    return pl.pallas_call(
        functools.partial(_post_kernel, alpha=alpha, gated=gate is not None, ff_chunk=min(1024, w1_all.shape[2])),
        grid=(n // tm,),
        in_specs=[row] * len(acts) + [_const_spec(wo.shape), layer_spec(w1_all), layer_spec(w2_all),
                                      _const_spec(ln.shape)],
        out_specs=row,
        out_shape=jax.ShapeDtypeStruct((n, d), F32),
        compiler_params=_params("parallel"),
        name="post_mlp",
    )(*acts, wo, w1_all, w2_all, ln)


def _fox_proj_kernel(x_ref, wq_ref, wk_ref, wv_ref, wf_ref, bf_ref,
                     q_ref, k_ref, v_ref, kb_ref, vb_ref, lf_ref, *lft_ref, channel_major):
    nb, rows, d = x_ref.shape
    xb = x_ref[...].reshape(nb * rows, d).astype(BF16)
    out = lambda z: z.reshape(nb, rows, z.shape[-1])
    q_ref[...] = out((_dot(xb, wq_ref[...]) * (HEAD ** -0.5 * LOG2E)).astype(BF16))
    k = _dot(xb, wk_ref[...])
    kb_ref[...] = out(k.astype(BF16))
    if channel_major:
        k_ref[0] = k.T
        v = _dot_nt(wv_ref[...], xb)
        v_ref[0] = v
        vb_ref[0] = v.astype(BF16)
    else:
        v = _dot(xb, wv_ref[...])
        k_ref[...] = out(k)
        v_ref[...] = out(v)
        vb_ref[...] = out(v.astype(BF16))
    f = _dot(xb, wf_ref[...]) + bf_ref[...]
    lf = jnp.minimum(f, 0.0) - jnp.log1p(jnp.exp(-jnp.abs(f)))
    lf_ref[...] = out(lf)
    if channel_major:
        lft_ref[0][0] = lf.T


def _fox_proj(x, wq, wk, wv, wf, bf, tm, channel_major):
    b, t, d = x.shape
    nb, tm = _seq_tile(b, t, tm)
    assert nb == 1 or not channel_major
    row = pl.BlockSpec((nb, tm, d), lambda bi, i: (bi, i, 0))
    rowf = pl.BlockSpec((nb, tm, LANES), lambda bi, i: (bi, i, 0))
    kv_spec = pl.BlockSpec((1, d, tm), lambda bi, i: (bi, 0, i)) if channel_major else row
    kv_shape = jax.ShapeDtypeStruct((b, d, t) if channel_major else (b, t, d), F32)
    act_bf = jax.ShapeDtypeStruct((b, t, d), BF16)
    vb_shape = jax.ShapeDtypeStruct(kv_shape.shape, BF16)
    consts = [wq, wk, wv, wf, bf]
    return pl.pallas_call(
        functools.partial(_fox_proj_kernel, channel_major=channel_major),
        grid=(b // nb, t // tm),
        in_specs=[row] + [_const_spec(c.shape) for c in consts],
        out_specs=[row, kv_spec, kv_spec, row, kv_spec, rowf]
        + [pl.BlockSpec((1, LANES, tm), lambda bi, i: (bi, 0, i))] * channel_major,
        out_shape=[act_bf, kv_shape, kv_shape, act_bf, vb_shape, jax.ShapeDtypeStruct((b, t, LANES), F32)]
        + [jax.ShapeDtypeStruct((b, LANES, t), F32)] * channel_major,
        compiler_params=_params("parallel", "parallel"),
        name="fox_proj",
    )(x, *consts)


CS_TILE = 512
SUM_ROWS = 16


def _cumsum_kernel(lf_ref, c_ref, carry_ref):
    @pl.when(pl.program_id(1) == 0)
    def _():
        carry_ref[...] = jnp.zeros_like(carry_ref)

    n = lf_ref.shape[1]
    cs = _dot3(lf_ref[0], _tri(n, upper=False), dot=lambda p, t: _dot(t, p)) + carry_ref[0:1, :]
    c_ref[0] = cs
    carry_ref[...] = jnp.broadcast_to(cs[n - 1:n, :], carry_ref.shape)


def _cumsum(lf):
    b, l, w = lf.shape
    tile = _row_tile(l, CS_TILE)
    spec = pl.BlockSpec((1, tile, w), lambda i, j: (i, j, 0))
    return pl.pallas_call(
        _cumsum_kernel,
        grid=(b, l // tile),
        in_specs=[spec],
        out_specs=spec,
        out_shape=jax.ShapeDtypeStruct((b, l, w), F32),
        scratch_shapes=[pltpu.VMEM((SUBLANES, w), F32)],
        compiler_params=_params("parallel", "arbitrary"),
        name="logf_cumsum",
    )(lf)


def _bias_lanes(c, head0, width, key_side):
    nh = width // HEAD
    src = lax.broadcasted_iota(jnp.int32, (LANES, width), 0)
    dst = lax.broadcasted_iota(jnp.int32, (LANES, width), 1)
    lane = lax.broadcasted_iota(jnp.int32, (1, width), 1)
    lo, sign = (3, -1.0) if key_side else (0, 1.0)
    base = [((hh + 1) % nh) * HEAD for hh in range(nh)]
    ones = functools.reduce(jnp.logical_or, [(lane >= bs + 3 - lo) & (lane < bs + 6 - lo) for bs in base])
    out = jnp.where(ones, 1.0, 0.0)
    for n, p in enumerate(_split3(c * LOG2E)):
        hit = functools.reduce(jnp.logical_or,
                               [(src == head0 + hh) & (dst == base[hh] + lo + n) for hh in range(nh)])
        out = out + _dot(p, jnp.where(hit, sign, 0.0).astype(BF16))
    return out


def _flash_kernel(q_ref, k_ref, vt_ref, c_ref, o_ref, kaug_scr, *, tq):
    g = pl.program_id(1)
    i = pl.program_id(2)
    t = k_ref.shape[1]
    width = k_ref.shape[2]
    nh = width // HEAD
    lane = lax.broadcasted_iota(jnp.int32, (1, width), 1)
    in_head = [(lane >> HEAD_SHIFT) == hh for hh in range(nh)]
    bias_lanes = [(lane >= ((hh + 1) % nh) * HEAD) & (lane < ((hh + 1) % nh) * HEAD + 6) for hh in range(nh)]

    def augment(x, bias):
        bias = bias.astype(BF16)
        return [jnp.where(in_head[hh], x, jnp.where(bias_lanes[hh], bias, jnp.zeros((), BF16)))
                for hh in range(nh)]

    @pl.when(i == 0)
    def _():
        for blk in range(t // tq):
            rows = slice(blk * tq, (blk + 1) * tq)
            ka = augment(k_ref[0, rows, :], _bias_lanes(c_ref[0, rows, :], nh * g, width, key_side=True))
            for hh in range(nh):
                kaug_scr[hh, rows, :] = ka[hh]

    row0 = pl.multiple_of(i * tq, tq)
    qa = augment(q_ref[0], _bias_lanes(c_ref[0, pl.ds(row0, tq), :], nh * g, width, key_side=False))

    qg = min(tq, SLAB)
    chains = [(hh, qs) for hh in range(nh) for qs in range(0, tq, qg)]
    nc = len(chains)
    qa_c = [qa[hh][qs:qs + qg] for hh, qs in chains]

    ones_rows = jnp.ones((SUM_ROWS, tq), BF16)

    def step(off, carry, mask):
        ks = [kaug_scr[hh, pl.ds(off, tq), :] for hh in range(nh)]
        vts = [jnp.concatenate([vt_ref[0, hh * HEAD:(hh + 1) * HEAD, pl.ds(off, tq)], ones_rows], axis=0)
               for hh in range(nh)]
        kext = [tq if mask is None else qs + qg for _, qs in chains]
        st = [_dot_nt(ks[hh][:kext[n]], qa_c[n]) for n, (hh, _) in enumerate(chains)]
        if mask is not None:
            st = [jnp.where(mask[:kext[n], qs:qs + qg], st[n], -jnp.inf) for n, (_, qs) in enumerate(chains)]
        m_new = [jnp.maximum(carry[n][0], jnp.max(st[n], axis=0, keepdims=True)) for n in range(nc)]
        alpha = [jnp.exp2(carry[n][0] - m_new[n]) for n in range(nc)]
        p = [jnp.exp2(st[n] - m_new[n]) for n in range(nc)]
        acc = [alpha[n] * carry[n][1] + _dot(vts[hh][:, :kext[n]], p[n].astype(BF16))
               for n, (hh, _) in enumerate(chains)]
        return tuple(zip(m_new, acc))

    init = tuple((jnp.full((1, qg), -jnp.inf, F32), jnp.zeros((HEAD + SUM_ROWS, qg), F32)) for _ in chains)
    carry = lax.fori_loop(0, i, lambda j, c: step(pl.multiple_of(j * tq, tq), c, None), init)
    causal = lax.broadcasted_iota(jnp.int32, (tq, tq), 0) <= lax.broadcasted_iota(jnp.int32, (tq, tq), 1)
    carry = step(row0, carry, causal)
    per_head = [jnp.concatenate([carry[n][1][:HEAD] / carry[n][1][HEAD:HEAD + 1]
                                 for n, (h2, _) in enumerate(chains) if h2 == hh], axis=1) for hh in range(nh)]
    o_ref[0] = jnp.concatenate(per_head, axis=0).T


def _flash(q, k, vt, c, tq, width):
    b, t, d = q.shape
    tq = _row_tile(t, tq)
    return pl.pallas_call(
        functools.partial(_flash_kernel, tq=tq),
        grid=(b, d // width, t // tq),
        in_specs=[pl.BlockSpec((1, tq, width), lambda bi, gi, i: (bi, i, gi)),
                  pl.BlockSpec((1, t, width), lambda bi, gi, i: (bi, 0, gi)),
                  pl.BlockSpec((1, width, t), lambda bi, gi, i: (bi, gi, 0)),
                  pl.BlockSpec((1, t, LANES), lambda bi, gi, i: (bi, 0, 0))],
        out_specs=pl.BlockSpec((1, tq, width), lambda bi, gi, i: (bi, i, gi)),
        out_shape=jax.ShapeDtypeStruct((b, t, d), F32),
        scratch_shapes=[pltpu.VMEM((width // HEAD, t, width), BF16)],
        compiler_params=_params("parallel", "parallel", "arbitrary"),
        name="fox_flash",
    )(q, k, vt, c)


DEC_TILE = 512
DEC_WIDTH = 512


def _decode_attn_kernel(q_ref, ckt_ref, cvt_ref, kn_ref, vn_ref, clf_ref, lfn_ref, o_ref, c_scr):
    g = pl.program_id(1)
    past = ckt_ref.shape[2]
    t = q_ref.shape[1]

    @pl.when(g == 0)
    def _():
        tri = _tri(DEC_TILE, upper=True)
        carry = jnp.zeros((clf_ref.shape[1], 1), F32)
        for blk in range(past // DEC_TILE):
            cols = slice(blk * DEC_TILE, (blk + 1) * DEC_TILE)
            cs = _dot3(clf_ref[0, :, cols], tri) + carry
            c_scr[:, cols] = cs
            carry = cs[:, DEC_TILE - 1:DEC_TILE]

    width = q_ref.shape[2]
    nh = width // HEAD
    lane = lax.broadcasted_iota(jnp.int32, (1, width), 1)
    hlane = lax.broadcasted_iota(jnp.int32, (1, LANES), 1)
    q = q_ref[0]
    kct = ckt_ref[0].astype(BF16)
    vct = cvt_ref[0].astype(BF16)
    kn = kn_ref[0]
    vn = vn_ref[0]
    causal = lax.broadcasted_iota(jnp.int32, (t, t), 1) <= lax.broadcasted_iota(jnp.int32, (t, t), 0)
    lf_new = _dot3(lfn_ref[0], _tri(t, upper=False), dot=lambda p, tr: _dot(tr, p))

    H = range(nh)
    head = [nh * g + hh for hh in H]
    in_head = [(lane >> HEAD_SHIFT) == hh for hh in H]
    rows = lambda parts: jnp.concatenate(parts, axis=0)
    qs = rows([jnp.where(in_head[hh], q, jnp.zeros((), BF16)) for hh in H])
    c_cache = [c_scr[pl.ds(head[hh], 1), :] for hh in H]
    total = [c_cache[hh][:, past - 1:past] for hh in H]
    onehot = [jnp.where(hlane == head[hh], 1.0, 0.0) for hh in H]
    cn_col = [jnp.sum(lf_new * onehot[hh], axis=-1, keepdims=True) + total[hh] for hh in H]
    cn_row = [_dot3(lf_new, jnp.broadcast_to(onehot[hh], (SUBLANES, LANES)).astype(BF16),
                    dot=lambda p, e: _dot_nt(e, p))[0:1] + total[hh] for hh in H]
    s1 = _dot(qs, kct) + rows([cn_col[hh] - c_cache[hh] for hh in H]) * LOG2E
    s2 = _dot_nt(qs, kn) + rows([cn_col[hh] - cn_row[hh] for hh in H]) * LOG2E
    s2 = jnp.where(rows([causal] * nh), s2, -jnp.inf)
    m = jnp.maximum(jnp.max(s1, axis=-1, keepdims=True), jnp.max(s2, axis=-1, keepdims=True))
    e1 = jnp.exp2(s1 - m)
    e2 = jnp.exp2(s2 - m)
    l = jnp.sum(e1, axis=-1, keepdims=True) + jnp.sum(e2, axis=-1, keepdims=True)
    o = (_dot_nt(e1.astype(BF16), vct) + _dot(e2.astype(BF16), vn)) / l
    out = o[(nh - 1) * t:]
    for hh in range(nh - 2, -1, -1):
        out = jnp.where(in_head[hh], o[hh * t:(hh + 1) * t], out)
    o_ref[0] = out


def _decode_attn(q, cache_kt, cache_vt, kn, vn, cache_lf, lf_new):
    b, t, d = q.shape
    p = cache_kt.shape[2]
    h = cache_lf.shape[1]
    assert p % DEC_TILE == 0
    new = pl.BlockSpec((1, t, DEC_WIDTH), lambda bi, gi: (bi, 0, gi))
    old = pl.BlockSpec((1, DEC_WIDTH, p), lambda bi, gi: (bi, gi, 0))
    return pl.pallas_call(
        _decode_attn_kernel,
        grid=(b, d // DEC_WIDTH),
        in_specs=[new, old, old, new, new,
                  pl.BlockSpec((1, h, p), lambda bi, gi: (bi, 0, 0)),
                  pl.BlockSpec((1, t, LANES), lambda bi, gi: (bi, 0, 0))],
        out_specs=new,
        out_shape=jax.ShapeDtypeStruct((b, t, d), F32),
        scratch_shapes=[pltpu.VMEM((h, p), F32)],
        compiler_params=_params("parallel", "arbitrary"),
        name="fox_decode_attn",
    )(q, cache_kt, cache_vt, kn, vn, cache_lf, lf_new)


def _pad_cols(w, mult):
    pad = (-w.shape[-1]) % mult
    return jnp.pad(w, ((0, 0), (0, pad))) if pad else w


def _pad_rows(w, mult):
    pad = (-w.shape[0]) % mult
    return jnp.pad(w, ((0, pad), (0, 0))) if pad else w


def kernel(x_prompt, x_sample, state_wkv, state_shift, cache_k, cache_v, cache_logf, rwkv_mu, rwkv_w0, rwkv_w1, rwkv_w2, rwkv_a0, rwkv_a1, rwkv_a2, rwkv_g1, rwkv_g2, rwkv_k_k, rwkv_k_a, rwkv_r_k, rwkv_w_r, rwkv_w_k, rwkv_w_v, rwkv_w_o, rwkv_lnx_g, rwkv_lnx_b, fox_w_in, fox_b_f, fox_w_o, ffn_w1, ffn_w2, ln_mix_g, ln_mix_b, ln_ffn_g, ln_ffn_b):
    depth = ln_mix_g.shape[0]
    alpha = (2 * depth) ** 0.25
    bp, tp, d = x_prompt.shape
    bs, ts, _ = x_sample.shape
    heads = d // HEAD
    past = cache_k.shape[2]
    tm = 512

    xs_by_group = {"p": x_prompt, "s": x_sample}
    outs = {grp: {n: [] for n in ("wkv", "shift", "k", "v", "lf")} for grp in ("p", "s")}
    w1_all = ffn_w1.astype(BF16)
    w2_all = ffn_w2.astype(BF16)
    for i in range(depth):
        j = i // 2
        ln = jnp.stack([ln_mix_g[i], ln_mix_b[i], ln_ffn_g[i], ln_ffn_b[i]])
        if i % 2 == 0:
            row = lambda z: z.reshape(1, d)
            proj = (rwkv_mu[j], row(rwkv_w0[j]), row(rwkv_a0[j]),
                    rwkv_w_r[j].astype(BF16), rwkv_w_k[j].astype(BF16), rwkv_w_v[j].astype(BF16),
                    _pad_cols(rwkv_w1[j], LANES).astype(BF16), _pad_rows(rwkv_w2[j], LANES).astype(BF16),
                    _pad_cols(rwkv_a1[j], LANES).astype(BF16), _pad_rows(rwkv_a2[j], LANES).astype(BF16),
                    _pad_cols(rwkv_g1[j], LANES).astype(BF16), _pad_rows(rwkv_g2[j], LANES).astype(BF16))
            wkv_vecs = (row(rwkv_k_k[j]), row(rwkv_k_a[j]), row(rwkv_r_k[j]), row(rwkv_lnx_g[j]), row(rwkv_lnx_b[j]))
            wo = rwkv_w_o[j].astype(BF16)
            starts = {"p": (jnp.zeros((bp, d), x_prompt.dtype), jnp.zeros((bp, heads, HEAD, HEAD), state_wkv.dtype)),
                      "s": (state_shift[j], state_wkv[j])}
            for grp in ("p", "s"):
                x = xs_by_group[grp]
                b, t, _ = x.shape
                shift0, wkv0 = starts[grp]
                r, lw, k, v, a, gate = _rwkv_proj(x, shift0.reshape(b, 1, d), *proj, tm=tm)
                o, s_fin = _wkv(r, lw, k, v, a, *wkv_vecs, wkv0.astype(F32))
                outs[grp]["wkv"].append(s_fin.astype(wkv0.dtype))
                outs[grp]["shift"].append(x[:, -1])
                xs_by_group[grp] = _post(o.reshape(b * t, d), gate.reshape(b * t, d), x.reshape(b * t, d),
                                         wo, w1_all, w2_all, i, ln, alpha, tm).reshape(b, t, d)
        else:
            w_in = fox_w_in[j]
            wq = w_in[:, :d].astype(BF16)
            wk = w_in[:, d:2 * d].astype(BF16)
            wv = w_in[:, 2 * d:3 * d].astype(BF16)
            wf = _pad_cols(w_in[:, 3 * d:], LANES).astype(BF16)
            bf = _pad_cols(fox_b_f[j].reshape(1, heads), LANES)
            wo = fox_w_o[j].astype(BF16)

            xp = xs_by_group["p"]
            q, kt, vt, kb, vtb, lf, lft = _fox_proj(xp, wq, wk, wv.T, wf, bf, tm, channel_major=True)
            o = _flash(q, kb, vtb, _cumsum(lf), tq=512, width=SLAB)
            to_heads = lambda z: jnp.transpose(z.reshape(bp, heads, HEAD, tp), (0, 3, 1, 2))
            outs["p"]["k"].append(to_heads(kt))
            outs["p"]["v"].append(to_heads(vt))
            outs["p"]["lf"].append(jnp.transpose(lft[:, :heads, :], (0, 2, 1)))
            xs_by_group["p"] = _post(o.reshape(bp * tp, d), None, xp.reshape(bp * tp, d),
                                     wo, w1_all, w2_all, i, ln, alpha, tm).reshape(bp, tp, d)

            xs = xs_by_group["s"]
            q, k, v, kb, vb, lf = _fox_proj(xs, wq, wk, wv, wf, bf, tm, channel_major=False)
            channel_major = lambda z: jnp.transpose(z, (0, 2, 3, 1)).reshape(bs, d, past)
            o = _decode_attn(q, channel_major(cache_k[j]), channel_major(cache_v[j]), kb, vb,
                             jnp.transpose(cache_logf[j].astype(F32), (0, 2, 1)), lf)
            outs["s"]["k"].append(k.reshape(bs, ts, heads, HEAD))
            outs["s"]["v"].append(v.reshape(bs, ts, heads, HEAD))
            outs["s"]["lf"].append(lf[:, :, :heads])
            xs_by_group["s"] = _post(o.reshape(bs * ts, d), None, xs.reshape(bs * ts, d),
                                     wo, w1_all, w2_all, i, ln, alpha, tm).reshape(bs, ts, d)

    st = jnp.stack
    op, os_ = outs["p"], outs["s"]
    return (xs_by_group["p"], xs_by_group["s"],
            st(op["wkv"]), st(op["shift"]), st(op["k"]), st(op["v"]), st(op["lf"]),
            st(os_["wkv"]), st(os_["shift"]), st(os_["k"]), st(os_["v"]), st(os_["lf"]))
```

```python
import functools
import math

import jax
import jax.numpy as jnp
from jax import lax
from jax.experimental import pallas as pl
from jax.experimental.pallas import tpu as pltpu

BF16 = jnp.bfloat16
F32 = jnp.float32

HEAD = 64
HEAD_SHIFT = 6
LANES = 128
SUBLANES = 8
SLAB = 256
HEADS_PER_SLAB = SLAB // HEAD
CHUNK = 64
WKV_CELLS_PER_STEP = 2
LN_EPS = 1e-5
GN_EPS = 64e-5
EXP_M05 = math.exp(-0.5)
LOG2E = math.log2(math.e)
VMEM_LIMIT = 56 * 1024 * 1024


def _dot(a, b):
    return jnp.dot(a, b, preferred_element_type=F32)


def _dot_nt(a, b):
    return lax.dot_general(a, b, (((1,), (1,)), ((), ())), preferred_element_type=F32)


def _dot_tn(a, b):
    return lax.dot_general(a, b, (((0,), (0,)), ((), ())), preferred_element_type=F32)


def _split3(x):
    p1 = x.astype(BF16)
    r1 = x - p1.astype(F32)
    p2 = r1.astype(BF16)
    p3 = (r1 - p2.astype(F32)).astype(BF16)
    return p1, p2, p3


def _dot3(a, b, dot=_dot, pieces=3):
    return sum(dot(p, b) for p in _split3(a)[:pieces])


def _tri(n, upper):
    r = lax.broadcasted_iota(jnp.int32, (n, n), 0)
    c = lax.broadcasted_iota(jnp.int32, (n, n), 1)
    return jnp.where((r <= c) if upper else (c <= r), 1.0, 0.0).astype(BF16)


def _layer_norm(z, g, b):
    mu = jnp.mean(z, axis=-1, keepdims=True)
    zc = z - mu
    var = jnp.mean(zc * zc, axis=-1, keepdims=True)
    return zc * lax.rsqrt(var + LN_EPS) * g + b


def _params(*sem):
    return pltpu.CompilerParams(dimension_semantics=sem, vmem_limit_bytes=VMEM_LIMIT)


def _const_spec(shape):
    nd = len(shape)
    return pl.BlockSpec(shape, lambda *_: (0,) * nd, pipeline_mode=pl.Buffered(1))


def _row_tile(n, want):
    t = min(n, want)
    assert n % t == 0 and t % SUBLANES == 0, (n, t)
    return t


def _seq_tile(b, t, want):
    tm = _row_tile(t, want)
    nb = min(b, max(want // t, 1)) if tm == t else 1
    assert b % nb == 0, (b, nb)
    return nb, tm


def _rwkv_proj_kernel(x_ref, prev_ref, first_ref, mu_ref, w0_ref, a0_ref, wr_ref, wk_ref, wv_ref,
                      w1_ref, w2_ref, a1_ref, a2_ref, g1_ref, g2_ref,
                      r_ref, lw_ref, k_ref, v_ref, a_ref, g_ref):
    nb, rows, d = x_ref.shape
    x = x_ref[...].reshape(nb * rows, d)
    row_in_seq = lax.broadcasted_iota(jnp.int32, x.shape, 0) % rows
    if nb == 1:
        row0 = jnp.where(pl.program_id(1) == 0, first_ref[0], prev_ref[0, SUBLANES - 1:SUBLANES, :])
    else:
        row0 = jnp.broadcast_to(first_ref[...], (nb, rows, d)).reshape(nb * rows, d)
    xx = jnp.where(row_in_seq == 0, row0, pltpu.roll(x, 1, 0)) - x
    out = lambda z: z.reshape(nb, rows, d)

    def mix(i):
        return (x + xx * mu_ref[i:i + 1, :]).astype(BF16)

    r_ref[...] = out(_dot(mix(0), wr_ref[...]))
    w_raw = w0_ref[...] + _dot(jnp.tanh(_dot(mix(1), w1_ref[...])).astype(BF16), w2_ref[...])
    lw_ref[...] = out(-EXP_M05 * jax.nn.sigmoid(w_raw))
    k_ref[...] = out(_dot(mix(2), wk_ref[...]))
    v_ref[...] = out(_dot(mix(3), wv_ref[...]))
    a_ref[...] = out(jax.nn.sigmoid(a0_ref[...] + _dot(_dot(mix(4), a1_ref[...]).astype(BF16), a2_ref[...])))
    g_ref[...] = out(_dot(jax.nn.sigmoid(_dot(mix(5), g1_ref[...])).astype(BF16), g2_ref[...]))


def _rwkv_proj(x, shift0, mu, w0, a0, wr, wk, wv, w1, w2, a1, a2, g1, g2, tm):
    b, t, d = x.shape
    nb, tm = _seq_tile(b, t, tm)
    per_tile = tm // SUBLANES
    row = pl.BlockSpec((nb, tm, d), lambda bi, i: (bi, i, 0))
    prev = pl.BlockSpec((1, SUBLANES, d), lambda bi, i: (bi * nb, jnp.maximum(i * per_tile - 1, 0), 0))
    first = pl.BlockSpec((nb, 1, d), lambda bi, i: (bi, 0, 0))
    consts = [mu, w0, a0, wr, wk, wv, w1, w2, a1, a2, g1, g2]
    return pl.pallas_call(
        _rwkv_proj_kernel,
        grid=(b // nb, t // tm),
        in_specs=[row, prev, first] + [_const_spec(c.shape) for c in consts],
        out_specs=[row] * 6,
        out_shape=[jax.ShapeDtypeStruct((b, t, d), F32)] * 6,
        compiler_params=_params("parallel", "parallel"),
        name="rwkv_proj",
    )(x, x, shift0, *consts)


def _wkv_kernel(r_ref, lw_ref, k_ref, v_ref, a_ref, kk_ref, ka_ref, rk_ref, lg_ref, lb_ref, h0_ref,
                o_ref, hout_ref, h_scr, *, last_chunk):
    c = pl.program_id(1)
    nslab = h_scr.shape[0]
    C = CHUNK
    nseq, rows = r_ref.shape[:2]
    nslab = nslab // nseq

    @pl.when(c == 0)
    def _():
        h_scr[...] = jnp.zeros_like(h_scr)
        for n in range(nseq * nslab):
            for h in range(HEADS_PER_SLAB):
                hs = slice(h * HEAD, (h + 1) * HEAD)
                h_scr[n, hs, hs] = h0_ref[n // nslab, (n % nslab) * HEADS_PER_SLAB + h]

    row_s = lax.broadcasted_iota(jnp.int32, (SLAB, SLAB), 0)
    col_s = lax.broadcasted_iota(jnp.int32, (SLAB, SLAB), 1)
    mask_bd = (row_s >> HEAD_SHIFT) == (col_s >> HEAD_SHIFT)
    row_c = lax.broadcasted_iota(jnp.int32, (C, SLAB), 0)
    colin_c = lax.broadcasted_iota(jnp.int32, (C, SLAB), 1) & (C - 1)
    strict_lo = colin_c < row_c
    incl_lo = colin_c <= row_c
    eye_c = colin_c == row_c
    ones_bd = jnp.where(mask_bd, 1.0, 0.0).astype(BF16)

    def bd(x):
        xb = x.astype(BF16)
        return jnp.where(mask_bd, jnp.concatenate([xb] * HEADS_PER_SLAB, axis=0), jnp.zeros((), BF16))

    def head_sum(x):
        return _dot(x.astype(BF16), ones_bd)

    nck = max(rows // C, 1)
    live = min(rows, C)

    ncell = nseq * nck

    def chunk_rows(ref, ci, sl):
        x = ref[ci // nck, (ci % nck) * C:(ci % nck) * C + live, sl]
        if live < C:
            x = jnp.concatenate([x, jnp.zeros((C - live, x.shape[1]), x.dtype)], axis=0)
        return x

    lw_c = [chunk_rows(lw_ref, ci, slice(None)) for ci in range(ncell)]
    L_c = [_dot3(lw_c[ci], _tri(C, upper=False), dot=lambda p, t: _dot(t, p), pieces=2) for ci in range(ncell)]

    units = [(ci, s) for ci in range(ncell) for s in range(nslab)]
    S = range(len(units))
    slabs = [slice(s * SLAB, (s + 1) * SLAB) for _, s in units]
    r = [chunk_rows(r_ref, ci, slabs[u]) for u, (ci, _) in enumerate(units)]
    v = [chunk_rows(v_ref, ci, slabs[u]) for u, (ci, _) in enumerate(units)]
    asig = [chunk_rows(a_ref, ci, slabs[u]) for u, (ci, _) in enumerate(units)]
    k0 = [chunk_rows(k_ref, ci, slabs[u]) for u, (ci, _) in enumerate(units)]
    lw = [lw_c[ci][:, slabs[u]] for u, (ci, _) in enumerate(units)]
    kk = [k0[s] * kk_ref[:, slabs[s]] for s in S]
    k = [k0[s] * (1.0 + (asig[s] - 1.0) * ka_ref[:, slabs[s]]) for s in S]
    sums = [head_sum(jnp.concatenate([kk[s] * kk[s], r[s] * k[s] * rk_ref[:, slabs[s]]], axis=0)) for s in S]
    kk = [kk[s] / jnp.maximum(jnp.sqrt(sums[s][:C]), 1e-12) for s in S]
    bonus = [sums[s][C:] * v[s] for s in S]
    bv = [kk[s] * asig[s] for s in S]
    L = [L_c[ci][:, slabs[u]] for u, (ci, _) in enumerate(units)]
    LC = [L[s][C - 1:C, :] for s in S]
    enL = [jnp.exp(-L[s]) for s in S]
    eLc = [jnp.exp(LC[s] - L[s]) for s in S]
    At = [-kk[s] * jnp.exp(L[s] - lw[s]) for s in S]
    Rt = [r[s] * jnp.exp(L[s]) for s in S]
    Bt = [bv[s] * enL[s] for s in S]
    Kt = [k[s] * enL[s] for s in S]
    Bh = [bv[s] * eLc[s] for s in S]
    Kh = [k[s] * eLc[s] for s in S]

    lhs = [jnp.concatenate([At[s], Rt[s]], axis=0).astype(BF16) for s in S]
    ab = [_dot_nt(lhs[s], bd(Bt[s])) for s in S]
    ak = [_dot_nt(lhs[s], bd(Kt[s])) for s in S]
    N = [jnp.where(strict_lo, ab[s][:C], 0.0) for s in S]
    M = [jnp.where(strict_lo, ak[s][:C], 0.0) for s in S]
    Arb = [jnp.where(incl_lo, ab[s][C:], 0.0) for s in S]
    Ark = [jnp.where(incl_lo, ak[s][C:], 0.0) for s in S]
    mv = [_dot(jnp.concatenate([M[s], Ark[s]], axis=0).astype(BF16), bd(v[s])) for s in S]
    MV = [mv[s][:C] for s in S]

    X = [jnp.where(eye_c, 1.0, 0.0) + N[s] for s in S]
    A = [_dot(N[s].astype(BF16), bd(N[s])) for s in S]
    for _ in range(int(math.log2(C)) - 2):
        both = [_dot(jnp.concatenate([A[s], X[s]], axis=0).astype(BF16), bd(A[s])) for s in S]
        X = [X[s] + both[s][C:] for s in S]
        A = [both[s][:C] for s in S]
    X = [X[s] + _dot(X[s].astype(BF16), bd(A[s])) for s in S]

    wu = [_dot(X[s].astype(BF16), jnp.concatenate([bd(At[s]), bd(MV[s])], axis=1)) for s in S]
    WA = [wu[s][:, :SLAB] for s in S]
    UV = [wu[s][:, SLAB:] for s in S]
    au = [_dot(Arb[s].astype(BF16), jnp.concatenate([bd(WA[s]), bd(UV[s])], axis=1)) for s in S]
    Rp = [Rt[s] + au[s][:, :SLAB] for s in S]
    Yl = [au[s][:, SLAB:] + mv[s][C:] for s in S]
    rw = [jnp.concatenate([Rp[s], WA[s]], axis=0).astype(BF16) for s in S]
    bk = [jnp.concatenate([Bh[s], Kh[s]], axis=0).astype(BF16) for s in S]
    pc = [jnp.exp(LC[s]) for s in S]

    h = [h_scr[n] for n in range(nseq * nslab)]
    y = [None] * len(units)
    for cj in range(nck):
        us = [((n // nslab) * nck + cj) * nslab + n % nslab for n in range(nseq * nslab)]
        yu = [_dot_nt(rw[u], h[n].astype(BF16)) for n, u in enumerate(us)]
        for n, u in enumerate(us):
            y[u] = yu[n][:C] + Yl[u]
        uv = [jnp.concatenate([yu[n][C:] + UV[u], v[u]], axis=0).astype(BF16) for n, u in enumerate(us)]
        h = [h[n] * pc[u] + jnp.where(mask_bd, _dot_tn(uv[n], bk[u]), 0.0) for n, u in enumerate(us)]
    for n in range(nseq * nslab):
        h_scr[n] = h[n]

    mu = [head_sum(y[s]) * (1.0 / HEAD) for s in S]
    yc = [y[s] - mu[s] for s in S]
    var = [head_sum(yc[s] * yc[s]) * (1.0 / HEAD) for s in S]
    for u, (ci, _) in enumerate(units):
        sl = slabs[u]
        out = yc[u] * lax.rsqrt(var[u] + GN_EPS) * lg_ref[:, sl] + lb_ref[:, sl] + bonus[u]
        o_ref[ci // nck, (ci % nck) * C:(ci % nck) * C + live, sl] = out[:live]

    @pl.when(c == last_chunk)
    def _():
        for n in range(nseq * nslab):
            for h in range(HEADS_PER_SLAB):
                hs = slice(h * HEAD, (h + 1) * HEAD)
                hout_ref[n // nslab, (n % nslab) * HEADS_PER_SLAB + h] = h_scr[n, hs, hs]


def _wkv(r, lw, k, v, a, k_k, k_a, r_k, lnx_g, lnx_b, h0):
    b, t, d = r.shape
    if t % (WKV_CELLS_PER_STEP * CHUNK) == 0:
        nseq, rows = 1, WKV_CELLS_PER_STEP * CHUNK
    else:
        nseq, rows = (WKV_CELLS_PER_STEP if t <= CHUNK and b % WKV_CELLS_PER_STEP == 0 else 1), min(t, CHUNK)
    assert t % rows == 0 and rows % SUBLANES == 0
    nslab = d // SLAB
    heads = d // HEAD
    seq = pl.BlockSpec((nseq, rows, d), lambda i, c: (i, c, 0))
    vec = pl.BlockSpec((1, d), lambda i, c: (0, 0))
    hspec = pl.BlockSpec((nseq, heads, HEAD, HEAD), lambda i, c: (i, 0, 0, 0))
    return pl.pallas_call(
        functools.partial(_wkv_kernel, last_chunk=t // rows - 1),
        grid=(b // nseq, t // rows),
        in_specs=[seq] * 5 + [vec] * 5 + [hspec],
        out_specs=[seq, hspec],
        out_shape=[jax.ShapeDtypeStruct((b, t, d), F32),
                   jax.ShapeDtypeStruct((b, heads, HEAD, HEAD), F32)],
        scratch_shapes=[pltpu.VMEM((nseq * nslab, SLAB, SLAB), F32)],
        compiler_params=_params("parallel", "arbitrary"),
        name="wkv_chunked",
    )(r, lw, k, v, a, k_k, k_a, r_k, lnx_g, lnx_b, h0)


def _post_kernel(*refs, alpha, gated, ff_chunk):
    if gated:
        pre_ref, gate_ref, x_ref, wo_ref, w1_ref, w2_ref, ln_ref, out_ref = refs
        pre = pre_ref[...] * gate_ref[...]
    else:
        pre_ref, x_ref, wo_ref, w1_ref, w2_ref, ln_ref, out_ref = refs
        pre = pre_ref[...]
    h = _dot(pre.astype(BF16), wo_ref[...])
    x1 = _layer_norm(alpha * x_ref[...] + h, ln_ref[0:1, :], ln_ref[1:2, :])
    x1b = x1.astype(BF16)
    m = jnp.zeros_like(x1)
    for f in range(w1_ref.shape[1] // ff_chunk):
        fs = slice(f * ff_chunk, (f + 1) * ff_chunk)
        hid = jnp.maximum(_dot(x1b, w1_ref[:, fs]), 0.0)
        m = m + _dot((hid * hid).astype(BF16), w2_ref[fs, :])
    out_ref[...] = _layer_norm(alpha * x1 + m, ln_ref[2:3, :], ln_ref[3:4, :])


def _post(pre, gate, x, wo, w1_all, w2_all, layer, ln, alpha, tm):
    n, d = x.shape
    tm = _row_tile(n, tm)
    row = pl.BlockSpec((tm, d), lambda i: (i, 0))
    acts = [pre, x] if gate is None else [pre, gate, x]
    layer_spec = lambda w: pl.BlockSpec((None,) + w.shape[1:], lambda i: (layer, 0, 0), pipeline_mode=pl.Buffered(1))
    return pl.pallas_call(
        functools.partial(_post_kernel, alpha=alpha, gated=gate is not None, ff_chunk=min(1024, w1_all.shape[2])),
        grid=(n // tm,),
        in_specs=[row] * len(acts) + [_const_spec(wo.shape), layer_spec(w1_all), layer_spec(w2_all),
                                      _const_spec(ln.shape)],
        out_specs=row,
        out_shape=jax.ShapeDtypeStruct((n, d), F32),
        compiler_params=_params("parallel"),
        name="post_mlp",
    )(*acts, wo, w1_all, w2_all, ln)


def _fox_proj_kernel(x_ref, wq_ref, wk_ref, wv_ref, wf_ref, bf_ref,
                     q_ref, k_ref, v_ref, kb_ref, vb_ref, lf_ref, *lft_ref, channel_major):
    nb, rows, d = x_ref.shape
    xb = x_ref[...].reshape(nb * rows, d).astype(BF16)
    out = lambda z: z.reshape(nb, rows, z.shape[-1])
    q_ref[...] = out((_dot(xb, wq_ref[...]) * (HEAD ** -0.5 * LOG2E)).astype(BF16))
    k = _dot(xb, wk_ref[...])
    kb_ref[...] = out(k.astype(BF16))
    if channel_major:
        k_ref[0] = k.T
        v = _dot_nt(wv_ref[...], xb)
        v_ref[0] = v
        vb_ref[0] = v.astype(BF16)
    else:
        v = _dot(xb, wv_ref[...])
        k_ref[...] = out(k)
        v_ref[...] = out(v)
        vb_ref[...] = out(v.astype(BF16))
    f = _dot(xb, wf_ref[...]) + bf_ref[...]
    lf = jnp.minimum(f, 0.0) - jnp.log1p(jnp.exp(-jnp.abs(f)))
    lf_ref[...] = out(lf)
    if channel_major:
        lft_ref[0][0] = lf.T


def _fox_proj(x, wq, wk, wv, wf, bf, tm, channel_major):
    b, t, d = x.shape
    nb, tm = _seq_tile(b, t, tm)
    assert nb == 1 or not channel_major
    row = pl.BlockSpec((nb, tm, d), lambda bi, i: (bi, i, 0))
    rowf = pl.BlockSpec((nb, tm, LANES), lambda bi, i: (bi, i, 0))
    kv_spec = pl.BlockSpec((1, d, tm), lambda bi, i: (bi, 0, i)) if channel_major else row
    kv_shape = jax.ShapeDtypeStruct((b, d, t) if channel_major else (b, t, d), F32)
    act_bf = jax.ShapeDtypeStruct((b, t, d), BF16)
    vb_shape = jax.ShapeDtypeStruct(kv_shape.shape, BF16)
    consts = [wq, wk, wv, wf, bf]
    return pl.pallas_call(
        functools.partial(_fox_proj_kernel, channel_major=channel_major),
        grid=(b // nb, t // tm),
        in_specs=[row] + [_const_spec(c.shape) for c in consts],
        out_specs=[row, kv_spec, kv_spec, row, kv_spec, rowf]
        + [pl.BlockSpec((1, LANES, tm), lambda bi, i: (bi, 0, i))] * channel_major,
        out_shape=[act_bf, kv_shape, kv_shape, act_bf, vb_shape, jax.ShapeDtypeStruct((b, t, LANES), F32)]
        + [jax.ShapeDtypeStruct((b, LANES, t), F32)] * channel_major,
        compiler_params=_params("parallel", "parallel"),
        name="fox_proj",
    )(x, *consts)


CS_TILE = 512
SUM_ROWS = 16


def _cumsum_kernel(lf_ref, c_ref, carry_ref):
    @pl.when(pl.program_id(1) == 0)
    def _():
        carry_ref[...] = jnp.zeros_like(carry_ref)

    n = lf_ref.shape[1]
    cs = _dot3(lf_ref[0], _tri(n, upper=False), dot=lambda p, t: _dot(t, p)) + carry_ref[0:1, :]
    c_ref[0] = cs
    carry_ref[...] = jnp.broadcast_to(cs[n - 1:n, :], carry_ref.shape)


def _cumsum(lf):
    b, l, w = lf.shape
    tile = _row_tile(l, CS_TILE)
    spec = pl.BlockSpec((1, tile, w), lambda i, j: (i, j, 0))
    return pl.pallas_call(
        _cumsum_kernel,
        grid=(b, l // tile),
        in_specs=[spec],
        out_specs=spec,
        out_shape=jax.ShapeDtypeStruct((b, l, w), F32),
        scratch_shapes=[pltpu.VMEM((SUBLANES, w), F32)],
        compiler_params=_params("parallel", "arbitrary"),
        name="logf_cumsum",
    )(lf)


def _bias_lanes(c, head0, width, key_side):
    nh = width // HEAD
    src = lax.broadcasted_iota(jnp.int32, (LANES, width), 0)
    dst = lax.broadcasted_iota(jnp.int32, (LANES, width), 1)
    lane = lax.broadcasted_iota(jnp.int32, (1, width), 1)
    lo, sign = (3, -1.0) if key_side else (0, 1.0)
    base = [((hh + 1) % nh) * HEAD for hh in range(nh)]
    ones = functools.reduce(jnp.logical_or, [(lane >= bs + 3 - lo) & (lane < bs + 6 - lo) for bs in base])
    out = jnp.where(ones, 1.0, 0.0)
    for n, p in enumerate(_split3(c * LOG2E)):
        hit = functools.reduce(jnp.logical_or,
                               [(src == head0 + hh) & (dst == base[hh] + lo + n) for hh in range(nh)])
        out = out + _dot(p, jnp.where(hit, sign, 0.0).astype(BF16))
    return out


def _flash_kernel(q_ref, k_ref, vt_ref, c_ref, o_ref, qaug_scr, kaug_scr, *, tq):
    g = pl.program_id(1)
    i = pl.program_id(2)
    t = k_ref.shape[1]
    width = k_ref.shape[2]
    nh = width // HEAD
    lane = lax.broadcasted_iota(jnp.int32, (1, width), 1)
    in_head = [(lane >> HEAD_SHIFT) == hh for hh in range(nh)]
    bias_lanes = [(lane >= ((hh + 1) % nh) * HEAD) & (lane < ((hh + 1) % nh) * HEAD + 6) for hh in range(nh)]

    def augment(x, bias):
        bias = bias.astype(BF16)
        return [jnp.where(in_head[hh], x, jnp.where(bias_lanes[hh], bias, jnp.zeros((), BF16)))
                for hh in range(nh)]

    @pl.when(i == 0)
    def _():
        for blk in range(t // tq):
            rows = slice(blk * tq, (blk + 1) * tq)
            c = c_ref[0, rows, :]
            qa = augment(q_ref[0, rows, :], _bias_lanes(c, nh * g, width, key_side=False))
            ka = augment(k_ref[0, rows, :], _bias_lanes(c, nh * g, width, key_side=True))
            for hh in range(nh):
                qaug_scr[hh, rows, :] = qa[hh]
                kaug_scr[hh, rows, :] = ka[hh]

    row0 = pl.multiple_of(i * tq, tq)

    qg = min(tq, SLAB)
    chains = [(hh, qs) for hh in range(nh) for qs in range(0, tq, qg)]
    nc = len(chains)
    qa_c = [qaug_scr[hh, pl.ds(row0 + qs, qg), :] for hh, qs in chains]

    ones_rows = jnp.ones((SUM_ROWS, tq), BF16)

    def step(off, carry, mask):
        ks = [kaug_scr[hh, pl.ds(off, tq), :] for hh in range(nh)]
        vts = [jnp.concatenate([vt_ref[0, hh * HEAD:(hh + 1) * HEAD, pl.ds(off, tq)], ones_rows], axis=0)
               for hh in range(nh)]
        kext = [tq if mask is None else qs + qg for _, qs in chains]
        st = [_dot_nt(ks[hh][:kext[n]], qa_c[n]) for n, (hh, _) in enumerate(chains)]
        if mask is not None:
            st = [jnp.where(mask[:kext[n], qs:qs + qg], st[n], -jnp.inf) for n, (_, qs) in enumerate(chains)]
        m_new = [jnp.maximum(carry[n][0], jnp.max(st[n], axis=0, keepdims=True)) for n in range(nc)]
        alpha = [jnp.exp2(carry[n][0] - m_new[n]) for n in range(nc)]
        p = [jnp.exp2(st[n] - m_new[n]) for n in range(nc)]
        acc = [alpha[n] * carry[n][1] + _dot(vts[hh][:, :kext[n]], p[n].astype(BF16))
               for n, (hh, _) in enumerate(chains)]
        return tuple(zip(m_new, acc))

    init = tuple((jnp.full((1, qg), -jnp.inf, F32), jnp.zeros((HEAD + SUM_ROWS, qg), F32)) for _ in chains)
    carry = lax.fori_loop(0, i, lambda j, c: step(pl.multiple_of(j * tq, tq), c, None), init)
    causal = lax.broadcasted_iota(jnp.int32, (tq, tq), 0) <= lax.broadcasted_iota(jnp.int32, (tq, tq), 1)
    carry = step(row0, carry, causal)
    per_head = [jnp.concatenate([carry[n][1][:HEAD] / carry[n][1][HEAD:HEAD + 1]
                                 for n, (h2, _) in enumerate(chains) if h2 == hh], axis=1) for hh in range(nh)]
    o_ref[0] = jnp.concatenate(per_head, axis=0).T


def _flash(q, k, vt, c, tq, width):
    b, t, d = q.shape
    tq = _row_tile(t, tq)
    return pl.pallas_call(
        functools.partial(_flash_kernel, tq=tq),
        grid=(b, d // width, t // tq),
        in_specs=[pl.BlockSpec((1, t, width), lambda bi, gi, i: (bi, 0, gi)),
                  pl.BlockSpec((1, t, width), lambda bi, gi, i: (bi, 0, gi)),
                  pl.BlockSpec((1, width, t), lambda bi, gi, i: (bi, gi, 0)),
                  pl.BlockSpec((1, t, LANES), lambda bi, gi, i: (bi, 0, 0))],
        out_specs=pl.BlockSpec((1, tq, width), lambda bi, gi, i: (bi, i, gi)),
        out_shape=jax.ShapeDtypeStruct((b, t, d), F32),
        scratch_shapes=[pltpu.VMEM((width // HEAD, t, width), BF16)] * 2,
        compiler_params=_params("parallel", "parallel", "arbitrary"),
        name="fox_flash",
    )(q, k, vt, c)


DEC_TILE = 512
DEC_WIDTH = 512


def _decode_attn_kernel(q_ref, ckt_ref, cvt_ref, kn_ref, vn_ref, clf_ref, lfn_ref, o_ref, c_scr):
    g = pl.program_id(1)
    past = ckt_ref.shape[2]
    t = q_ref.shape[1]

    @pl.when(g == 0)
    def _():
        tri = _tri(DEC_TILE, upper=True)
        carry = jnp.zeros((clf_ref.shape[1], 1), F32)
        for blk in range(past // DEC_TILE):
            cols = slice(blk * DEC_TILE, (blk + 1) * DEC_TILE)
            cs = _dot3(clf_ref[0, :, cols], tri) + carry
            c_scr[:, cols] = cs
            carry = cs[:, DEC_TILE - 1:DEC_TILE]

    width = q_ref.shape[2]
    nh = width // HEAD
    lane = lax.broadcasted_iota(jnp.int32, (1, width), 1)
    hlane = lax.broadcasted_iota(jnp.int32, (1, LANES), 1)
    q = q_ref[0]
    kct = ckt_ref[0].astype(BF16)
    vct = cvt_ref[0].astype(BF16)
    kn = kn_ref[0]
    vn = vn_ref[0]
    causal = lax.broadcasted_iota(jnp.int32, (t, t), 1) <= lax.broadcasted_iota(jnp.int32, (t, t), 0)
    lf_new = _dot3(lfn_ref[0], _tri(t, upper=False), dot=lambda p, tr: _dot(tr, p))

    H = range(nh)
    head = [nh * g + hh for hh in H]
    in_head = [(lane >> HEAD_SHIFT) == hh for hh in H]
    rows = lambda parts: jnp.concatenate(parts, axis=0)
    qs = rows([jnp.where(in_head[hh], q, jnp.zeros((), BF16)) for hh in H])
    c_cache = [c_scr[pl.ds(head[hh], 1), :] for hh in H]
    total = [c_cache[hh][:, past - 1:past] for hh in H]
    onehot = [jnp.where(hlane == head[hh], 1.0, 0.0) for hh in H]
    cn_col = [jnp.sum(lf_new * onehot[hh], axis=-1, keepdims=True) + total[hh] for hh in H]
    cn_row = [_dot3(lf_new, jnp.broadcast_to(onehot[hh], (SUBLANES, LANES)).astype(BF16),
                    dot=lambda p, e: _dot_nt(e, p))[0:1] + total[hh] for hh in H]
    s1 = _dot(qs, kct) + rows([cn_col[hh] - c_cache[hh] for hh in H]) * LOG2E
    s2 = _dot_nt(qs, kn) + rows([cn_col[hh] - cn_row[hh] for hh in H]) * LOG2E
    s2 = jnp.where(rows([causal] * nh), s2, -jnp.inf)
    m = jnp.maximum(jnp.max(s1, axis=-1, keepdims=True), jnp.max(s2, axis=-1, keepdims=True))
    e1 = jnp.exp2(s1 - m)
    e2 = jnp.exp2(s2 - m)
    l = jnp.sum(e1, axis=-1, keepdims=True) + jnp.sum(e2, axis=-1, keepdims=True)
    o = (_dot_nt(e1.astype(BF16), vct) + _dot(e2.astype(BF16), vn)) / l
    out = o[(nh - 1) * t:]
    for hh in range(nh - 2, -1, -1):
        out = jnp.where(in_head[hh], o[hh * t:(hh + 1) * t], out)
    o_ref[0] = out


def _decode_attn(q, cache_kt, cache_vt, kn, vn, cache_lf, lf_new):
    b, t, d = q.shape
    p = cache_kt.shape[2]
    h = cache_lf.shape[1]
    assert p % DEC_TILE == 0
    new = pl.BlockSpec((1, t, DEC_WIDTH), lambda bi, gi: (bi, 0, gi))
    old = pl.BlockSpec((1, DEC_WIDTH, p), lambda bi, gi: (bi, gi, 0))
    return pl.pallas_call(
        _decode_attn_kernel,
        grid=(b, d // DEC_WIDTH),
        in_specs=[new, old, old, new, new,
                  pl.BlockSpec((1, h, p), lambda bi, gi: (bi, 0, 0)),
                  pl.BlockSpec((1, t, LANES), lambda bi, gi: (bi, 0, 0))],
        out_specs=new,
        out_shape=jax.ShapeDtypeStruct((b, t, d), F32),
        scratch_shapes=[pltpu.VMEM((h, p), F32)],
        compiler_params=_params("parallel", "arbitrary"),
        name="fox_decode_attn",
    )(q, cache_kt, cache_vt, kn, vn, cache_lf, lf_new)


def _pad_cols(w, mult):
    pad = (-w.shape[-1]) % mult
    return jnp.pad(w, ((0, 0), (0, pad))) if pad else w


def _pad_rows(w, mult):
    pad = (-w.shape[0]) % mult
    return jnp.pad(w, ((0, pad), (0, 0))) if pad else w


def kernel(x_prompt, x_sample, state_wkv, state_shift, cache_k, cache_v, cache_logf, rwkv_mu, rwkv_w0, rwkv_w1, rwkv_w2, rwkv_a0, rwkv_a1, rwkv_a2, rwkv_g1, rwkv_g2, rwkv_k_k, rwkv_k_a, rwkv_r_k, rwkv_w_r, rwkv_w_k, rwkv_w_v, rwkv_w_o, rwkv_lnx_g, rwkv_lnx_b, fox_w_in, fox_b_f, fox_w_o, ffn_w1, ffn_w2, ln_mix_g, ln_mix_b, ln_ffn_g, ln_ffn_b):
    depth = ln_mix_g.shape[0]
    alpha = (2 * depth) ** 0.25
    bp, tp, d = x_prompt.shape
    bs, ts, _ = x_sample.shape
    heads = d // HEAD
    past = cache_k.shape[2]
    tm = 512

    xs_by_group = {"p": x_prompt, "s": x_sample}
    outs = {grp: {n: [] for n in ("wkv", "shift", "k", "v", "lf")} for grp in ("p", "s")}
    w1_all = ffn_w1.astype(BF16)
    w2_all = ffn_w2.astype(BF16)
    for i in range(depth):
        j = i // 2
        ln = jnp.stack([ln_mix_g[i], ln_mix_b[i], ln_ffn_g[i], ln_ffn_b[i]])
        if i % 2 == 0:
            row = lambda z: z.reshape(1, d)
            proj = (rwkv_mu[j], row(rwkv_w0[j]), row(rwkv_a0[j]),
                    rwkv_w_r[j].astype(BF16), rwkv_w_k[j].astype(BF16), rwkv_w_v[j].astype(BF16),
                    _pad_cols(rwkv_w1[j], LANES).astype(BF16), _pad_rows(rwkv_w2[j], LANES).astype(BF16),
                    _pad_cols(rwkv_a1[j], LANES).astype(BF16), _pad_rows(rwkv_a2[j], LANES).astype(BF16),
                    _pad_cols(rwkv_g1[j], LANES).astype(BF16), _pad_rows(rwkv_g2[j], LANES).astype(BF16))
            wkv_vecs = (row(rwkv_k_k[j]), row(rwkv_k_a[j]), row(rwkv_r_k[j]), row(rwkv_lnx_g[j]), row(rwkv_lnx_b[j]))
            wo = rwkv_w_o[j].astype(BF16)
            starts = {"p": (jnp.zeros((bp, d), x_prompt.dtype), jnp.zeros((bp, heads, HEAD, HEAD), state_wkv.dtype)),
                      "s": (state_shift[j], state_wkv[j])}
            for grp in ("p", "s"):
                x = xs_by_group[grp]
                b, t, _ = x.shape
                shift0, wkv0 = starts[grp]
                r, lw, k, v, a, gate = _rwkv_proj(x, shift0.reshape(b, 1, d), *proj, tm=tm)
                o, s_fin = _wkv(r, lw, k, v, a, *wkv_vecs, wkv0.astype(F32))
                outs[grp]["wkv"].append(s_fin.astype(wkv0.dtype))
                outs[grp]["shift"].append(x[:, -1])
                xs_by_group[grp] = _post(o.reshape(b * t, d), gate.reshape(b * t, d), x.reshape(b * t, d),
                                         wo, w1_all, w2_all, i, ln, alpha, tm).reshape(b, t, d)
        else:
            w_in = fox_w_in[j]
            wq = w_in[:, :d].astype(BF16)
            wk = w_in[:, d:2 * d].astype(BF16)
            wv = w_in[:, 2 * d:3 * d].astype(BF16)
            wf = _pad_cols(w_in[:, 3 * d:], LANES).astype(BF16)
            bf = _pad_cols(fox_b_f[j].reshape(1, heads), LANES)
            wo = fox_w_o[j].astype(BF16)

            xp = xs_by_group["p"]
            q, kt, vt, kb, vtb, lf, lft = _fox_proj(xp, wq, wk, wv.T, wf, bf, tm, channel_major=True)
            o = _flash(q, kb, vtb, _cumsum(lf), tq=512, width=SLAB)
            to_heads = lambda z: jnp.transpose(z.reshape(bp, heads, HEAD, tp), (0, 3, 1, 2))
            outs["p"]["k"].append(to_heads(kt))
            outs["p"]["v"].append(to_heads(vt))
            outs["p"]["lf"].append(jnp.transpose(lft[:, :heads, :], (0, 2, 1)))
            xs_by_group["p"] = _post(o.reshape(bp * tp, d), None, xp.reshape(bp * tp, d),
                                     wo, w1_all, w2_all, i, ln, alpha, tm).reshape(bp, tp, d)

            xs = xs_by_group["s"]
            q, k, v, kb, vb, lf = _fox_proj(xs, wq, wk, wv, wf, bf, tm, channel_major=False)
            channel_major = lambda z: jnp.transpose(z, (0, 2, 3, 1)).reshape(bs, d, past)
            o = _decode_attn(q, channel_major(cache_k[j]), channel_major(cache_v[j]), kb, vb,
                             jnp.transpose(cache_logf[j].astype(F32), (0, 2, 1)), lf)
            outs["s"]["k"].append(k.reshape(bs, ts, heads, HEAD))
            outs["s"]["v"].append(v.reshape(bs, ts, heads, HEAD))
            outs["s"]["lf"].append(lf[:, :, :heads])
            xs_by_group["s"] = _post(o.reshape(bs * ts, d), None, xs.reshape(bs * ts, d),
                                     wo, w1_all, w2_all, i, ln, alpha, tm).reshape(bs, ts, d)

    st = jnp.stack
    op, os_ = outs["p"], outs["s"]
    return (xs_by_group["p"], xs_by_group["s"],
            st(op["wkv"]), st(op["shift"]), st(op["k"]), st(op["v"]), st(op["lf"]),
            st(os_["wkv"]), st(os_["shift"]), st(os_["k"]), st(os_["v"]), st(os_["lf"]))
```

```python
import functools
import math

import jax
import jax.numpy as jnp
from jax import lax
from jax.experimental import pallas as pl
from jax.experimental.pallas import tpu as pltpu

BF16 = jnp.bfloat16
F32 = jnp.float32

HEAD = 64
HEAD_SHIFT = 6
LANES = 128
SUBLANES = 8
SLAB = 256
HEADS_PER_SLAB = SLAB // HEAD
CHUNK = 64
WKV_CELLS_PER_STEP = 4
LN_EPS = 1e-5
GN_EPS = 64e-5
EXP_M05 = math.exp(-0.5)
LOG2E = math.log2(math.e)
VMEM_LIMIT = 56 * 1024 * 1024


def _dot(a, b):
    return jnp.dot(a, b, preferred_element_type=F32)


def _dot_nt(a, b):
    return lax.dot_general(a, b, (((1,), (1,)), ((), ())), preferred_element_type=F32)


def _dot_tn(a, b):
    return lax.dot_general(a, b, (((0,), (0,)), ((), ())), preferred_element_type=F32)


def _split3(x):
    p1 = x.astype(BF16)
    r1 = x - p1.astype(F32)
    p2 = r1.astype(BF16)
    p3 = (r1 - p2.astype(F32)).astype(BF16)
    return p1, p2, p3


def _dot3(a, b, dot=_dot, pieces=3):
    return sum(dot(p, b) for p in _split3(a)[:pieces])


def _tri(n, upper):
    r = lax.broadcasted_iota(jnp.int32, (n, n), 0)
    c = lax.broadcasted_iota(jnp.int32, (n, n), 1)
    return jnp.where((r <= c) if upper else (c <= r), 1.0, 0.0).astype(BF16)


def _layer_norm(z, g, b):
    mu = jnp.mean(z, axis=-1, keepdims=True)
    zc = z - mu
    var = jnp.mean(zc * zc, axis=-1, keepdims=True)
    return zc * lax.rsqrt(var + LN_EPS) * g + b


def _params(*sem):
    return pltpu.CompilerParams(dimension_semantics=sem, vmem_limit_bytes=VMEM_LIMIT)


def _const_spec(shape):
    nd = len(shape)
    return pl.BlockSpec(shape, lambda *_: (0,) * nd, pipeline_mode=pl.Buffered(1))


def _row_tile(n, want):
    t = min(n, want)
    assert n % t == 0 and t % SUBLANES == 0, (n, t)
    return t


def _seq_tile(b, t, want):
    tm = _row_tile(t, want)
    nb = min(b, max(want // t, 1)) if tm == t else 1
    assert b % nb == 0, (b, nb)
    return nb, tm


def _rwkv_proj_kernel(x_ref, prev_ref, first_ref, mu_ref, w0_ref, a0_ref, wr_ref, wk_ref, wv_ref,
                      w1_ref, w2_ref, a1_ref, a2_ref, g1_ref, g2_ref,
                      r_ref, lw_ref, k_ref, v_ref, a_ref, g_ref):
    nb, rows, d = x_ref.shape
    x = x_ref[...].reshape(nb * rows, d)
    row_in_seq = lax.broadcasted_iota(jnp.int32, x.shape, 0) % rows
    if nb == 1:
        row0 = jnp.where(pl.program_id(1) == 0, first_ref[0], prev_ref[0, SUBLANES - 1:SUBLANES, :])
    else:
        row0 = jnp.broadcast_to(first_ref[...], (nb, rows, d)).reshape(nb * rows, d)
    xx = jnp.where(row_in_seq == 0, row0, pltpu.roll(x, 1, 0)) - x
    out = lambda z: z.reshape(nb, rows, d)

    def mix(i):
        return (x + xx * mu_ref[i:i + 1, :]).astype(BF16)

    r_ref[...] = out(_dot(mix(0), wr_ref[...]))
    w_raw = w0_ref[...] + _dot(jnp.tanh(_dot(mix(1), w1_ref[...])).astype(BF16), w2_ref[...])
    lw_ref[...] = out(-EXP_M05 * jax.nn.sigmoid(w_raw))
    k_ref[...] = out(_dot(mix(2), wk_ref[...]))
    v_ref[...] = out(_dot(mix(3), wv_ref[...]))
    a_ref[...] = out(jax.nn.sigmoid(a0_ref[...] + _dot(_dot(mix(4), a1_ref[...]).astype(BF16), a2_ref[...])))
    g_ref[...] = out(_dot(jax.nn.sigmoid(_dot(mix(5), g1_ref[...])).astype(BF16), g2_ref[...]))


def _rwkv_proj(x, shift0, mu, w0, a0, wr, wk, wv, w1, w2, a1, a2, g1, g2, tm):
    b, t, d = x.shape
    nb, tm = _seq_tile(b, t, tm)
    per_tile = tm // SUBLANES
    row = pl.BlockSpec((nb, tm, d), lambda bi, i: (bi, i, 0))
    prev = pl.BlockSpec((1, SUBLANES, d), lambda bi, i: (bi * nb, jnp.maximum(i * per_tile - 1, 0), 0))
    first = pl.BlockSpec((nb, 1, d), lambda bi, i: (bi, 0, 0))
    consts = [mu, w0, a0, wr, wk, wv, w1, w2, a1, a2, g1, g2]
    return pl.pallas_call(
        _rwkv_proj_kernel,
        grid=(b // nb, t // tm),
        in_specs=[row, prev, first] + [_const_spec(c.shape) for c in consts],
        out_specs=[row] * 6,
        out_shape=[jax.ShapeDtypeStruct((b, t, d), F32)] * 6,
        compiler_params=_params("parallel", "parallel"),
        name="rwkv_proj",
    )(x, x, shift0, *consts)


def _wkv_kernel(r_ref, lw_ref, k_ref, v_ref, a_ref, kk_ref, ka_ref, rk_ref, lg_ref, lb_ref, h0_ref,
                o_ref, hout_ref, h_scr, *, last_chunk):
    c = pl.program_id(1)
    nslab = h_scr.shape[0]
    C = CHUNK
    nseq, rows = r_ref.shape[:2]
    nslab = nslab // nseq

    @pl.when(c == 0)
    def _():
        h_scr[...] = jnp.zeros_like(h_scr)
        for n in range(nseq * nslab):
            for h in range(HEADS_PER_SLAB):
                hs = slice(h * HEAD, (h + 1) * HEAD)
                h_scr[n, hs, hs] = h0_ref[n // nslab, (n % nslab) * HEADS_PER_SLAB + h]

    row_s = lax.broadcasted_iota(jnp.int32, (SLAB, SLAB), 0)
    col_s = lax.broadcasted_iota(jnp.int32, (SLAB, SLAB), 1)
    mask_bd = (row_s >> HEAD_SHIFT) == (col_s >> HEAD_SHIFT)
    row_c = lax.broadcasted_iota(jnp.int32, (C, SLAB), 0)
    colin_c = lax.broadcasted_iota(jnp.int32, (C, SLAB), 1) & (C - 1)
    strict_lo = colin_c < row_c
    incl_lo = colin_c <= row_c
    eye_c = colin_c == row_c
    ones_bd = jnp.where(mask_bd, 1.0, 0.0).astype(BF16)

    def bd(x):
        xb = x.astype(BF16)
        return jnp.where(mask_bd, jnp.concatenate([xb] * HEADS_PER_SLAB, axis=0), jnp.zeros((), BF16))

    def head_sum(x):
        return _dot(x.astype(BF16), ones_bd)

    nck = max(rows // C, 1)
    live = min(rows, C)

    ncell = nseq * nck

    def chunk_rows(ref, ci, sl):
        x = ref[ci // nck, (ci % nck) * C:(ci % nck) * C + live, sl]
        if live < C:
            x = jnp.concatenate([x, jnp.zeros((C - live, x.shape[1]), x.dtype)], axis=0)
        return x

    lw_c = [chunk_rows(lw_ref, ci, slice(None)) for ci in range(ncell)]
    L_c = [_dot3(lw_c[ci], _tri(C, upper=False), dot=lambda p, t: _dot(t, p), pieces=2) for ci in range(ncell)]

    units = [(ci, s) for ci in range(ncell) for s in range(nslab)]
    S = range(len(units))
    slabs = [slice(s * SLAB, (s + 1) * SLAB) for _, s in units]
    r = [chunk_rows(r_ref, ci, slabs[u]) for u, (ci, _) in enumerate(units)]
    v = [chunk_rows(v_ref, ci, slabs[u]) for u, (ci, _) in enumerate(units)]
    asig = [chunk_rows(a_ref, ci, slabs[u]) for u, (ci, _) in enumerate(units)]
    k0 = [chunk_rows(k_ref, ci, slabs[u]) for u, (ci, _) in enumerate(units)]
    lw = [lw_c[ci][:, slabs[u]] for u, (ci, _) in enumerate(units)]
    kk = [k0[s] * kk_ref[:, slabs[s]] for s in S]
    k = [k0[s] * (1.0 + (asig[s] - 1.0) * ka_ref[:, slabs[s]]) for s in S]
    sums = [head_sum(jnp.concatenate([kk[s] * kk[s], r[s] * k[s] * rk_ref[:, slabs[s]]], axis=0)) for s in S]
    kk = [kk[s] / jnp.maximum(jnp.sqrt(sums[s][:C]), 1e-12) for s in S]
    bonus = [sums[s][C:] * v[s] for s in S]
    bv = [kk[s] * asig[s] for s in S]
    L = [L_c[ci][:, slabs[u]] for u, (ci, _) in enumerate(units)]
    LC = [L[s][C - 1:C, :] for s in S]
    enL = [jnp.exp(-L[s]) for s in S]
    eLc = [jnp.exp(LC[s] - L[s]) for s in S]
    At = [-kk[s] * jnp.exp(L[s] - lw[s]) for s in S]
    Rt = [r[s] * jnp.exp(L[s]) for s in S]
    Bt = [bv[s] * enL[s] for s in S]
    Kt = [k[s] * enL[s] for s in S]
    Bh = [bv[s] * eLc[s] for s in S]
    Kh = [k[s] * eLc[s] for s in S]

    lhs = [jnp.concatenate([At[s], Rt[s]], axis=0).astype(BF16) for s in S]
    ab = [_dot_nt(lhs[s], bd(Bt[s])) for s in S]
    ak = [_dot_nt(lhs[s], bd(Kt[s])) for s in S]
    N = [jnp.where(strict_lo, ab[s][:C], 0.0) for s in S]
    M = [jnp.where(strict_lo, ak[s][:C], 0.0) for s in S]
    Arb = [jnp.where(incl_lo, ab[s][C:], 0.0) for s in S]
    Ark = [jnp.where(incl_lo, ak[s][C:], 0.0) for s in S]
    mv = [_dot(jnp.concatenate([M[s], Ark[s]], axis=0).astype(BF16), bd(v[s])) for s in S]
    MV = [mv[s][:C] for s in S]

    X = [jnp.where(eye_c, 1.0, 0.0) + N[s] for s in S]
    A = [_dot(N[s].astype(BF16), bd(N[s])) for s in S]
    for _ in range(int(math.log2(C)) - 2):
        both = [_dot(jnp.concatenate([A[s], X[s]], axis=0).astype(BF16), bd(A[s])) for s in S]
        X = [X[s] + both[s][C:] for s in S]
        A = [both[s][:C] for s in S]
    X = [X[s] + _dot(X[s].astype(BF16), bd(A[s])) for s in S]

    wu = [_dot(X[s].astype(BF16), jnp.concatenate([bd(At[s]), bd(MV[s])], axis=1)) for s in S]
    WA = [wu[s][:, :SLAB] for s in S]
    UV = [wu[s][:, SLAB:] for s in S]
    au = [_dot(Arb[s].astype(BF16), jnp.concatenate([bd(WA[s]), bd(UV[s])], axis=1)) for s in S]
    Rp = [Rt[s] + au[s][:, :SLAB] for s in S]
    Yl = [au[s][:, SLAB:] + mv[s][C:] for s in S]
    rw = [jnp.concatenate([Rp[s], WA[s]], axis=0).astype(BF16) for s in S]
    bk = [jnp.concatenate([Bh[s], Kh[s]], axis=0).astype(BF16) for s in S]
    pc = [jnp.exp(LC[s]) for s in S]

    h = [h_scr[n] for n in range(nseq * nslab)]
    y = [None] * len(units)
    for cj in range(nck):
        us = [((n // nslab) * nck + cj) * nslab + n % nslab for n in range(nseq * nslab)]
        yu = [_dot_nt(rw[u], h[n].astype(BF16)) for n, u in enumerate(us)]
        for n, u in enumerate(us):
            y[u] = yu[n][:C] + Yl[u]
        uv = [jnp.concatenate([yu[n][C:] + UV[u], v[u]], axis=0).astype(BF16) for n, u in enumerate(us)]
        h = [h[n] * pc[u] + jnp.where(mask_bd, _dot_tn(uv[n], bk[u]), 0.0) for n, u in enumerate(us)]
    for n in range(nseq * nslab):
        h_scr[n] = h[n]

    mu = [head_sum(y[s]) * (1.0 / HEAD) for s in S]
    yc = [y[s] - mu[s] for s in S]
    var = [head_sum(yc[s] * yc[s]) * (1.0 / HEAD) for s in S]
    for u, (ci, _) in enumerate(units):
        sl = slabs[u]
        out = yc[u] * lax.rsqrt(var[u] + GN_EPS) * lg_ref[:, sl] + lb_ref[:, sl] + bonus[u]
        o_ref[ci // nck, (ci % nck) * C:(ci % nck) * C + live, sl] = out[:live]

    @pl.when(c == last_chunk)
    def _():
        for n in range(nseq * nslab):
            for h in range(HEADS_PER_SLAB):
                hs = slice(h * HEAD, (h + 1) * HEAD)
                hout_ref[n // nslab, (n % nslab) * HEADS_PER_SLAB + h] = h_scr[n, hs, hs]


def _wkv(r, lw, k, v, a, k_k, k_a, r_k, lnx_g, lnx_b, h0):
    b, t, d = r.shape
    if t % (WKV_CELLS_PER_STEP * CHUNK) == 0:
        nseq, rows = 1, WKV_CELLS_PER_STEP * CHUNK
    else:
        nseq, rows = (WKV_CELLS_PER_STEP if t <= CHUNK and b % WKV_CELLS_PER_STEP == 0 else 1), min(t, CHUNK)
    assert t % rows == 0 and rows % SUBLANES == 0
    nslab = d // SLAB
    heads = d // HEAD
    seq = pl.BlockSpec((nseq, rows, d), lambda i, c: (i, c, 0))
    vec = pl.BlockSpec((1, d), lambda i, c: (0, 0))
    hspec = pl.BlockSpec((nseq, heads, HEAD, HEAD), lambda i, c: (i, 0, 0, 0))
    return pl.pallas_call(
        functools.partial(_wkv_kernel, last_chunk=t // rows - 1),
        grid=(b // nseq, t // rows),
        in_specs=[seq] * 5 + [vec] * 5 + [hspec],
        out_specs=[seq, hspec],
        out_shape=[jax.ShapeDtypeStruct((b, t, d), F32),
                   jax.ShapeDtypeStruct((b, heads, HEAD, HEAD), F32)],
        scratch_shapes=[pltpu.VMEM((nseq * nslab, SLAB, SLAB), F32)],
        compiler_params=_params("parallel", "arbitrary"),
        name="wkv_chunked",
    )(r, lw, k, v, a, k_k, k_a, r_k, lnx_g, lnx_b, h0)


def _post_kernel(*refs, alpha, gated, ff_chunk):
    if gated:
        pre_ref, gate_ref, x_ref, wo_ref, w1_ref, w2_ref, ln_ref, out_ref = refs
        pre = pre_ref[...] * gate_ref[...]
    else:
        pre_ref, x_ref, wo_ref, w1_ref, w2_ref, ln_ref, out_ref = refs
        pre = pre_ref[...]
    h = _dot(pre.astype(BF16), wo_ref[...])
    x1 = _layer_norm(alpha * x_ref[...] + h, ln_ref[0:1, :], ln_ref[1:2, :])
    x1b = x1.astype(BF16)
    m = jnp.zeros_like(x1)
    for f in range(w1_ref.shape[1] // ff_chunk):
        fs = slice(f * ff_chunk, (f + 1) * ff_chunk)
        hid = jnp.maximum(_dot(x1b, w1_ref[:, fs]), 0.0)
        m = m + _dot((hid * hid).astype(BF16), w2_ref[fs, :])
    out_ref[...] = _layer_norm(alpha * x1 + m, ln_ref[2:3, :], ln_ref[3:4, :])


def _post(pre, gate, x, wo, w1_all, w2_all, layer, ln, alpha, tm):
    n, d = x.shape
    tm = _row_tile(n, tm)
    row = pl.BlockSpec((tm, d), lambda i: (i, 0))
    acts = [pre, x] if gate is None else [pre, gate, x]
    layer_spec = lambda w: pl.BlockSpec((None,) + w.shape[1:], lambda i: (layer, 0, 0), pipeline_mode=pl.Buffered(1))
    return pl.pallas_call(
        functools.partial(_post_kernel, alpha=alpha, gated=gate is not None, ff_chunk=min(1024, w1_all.shape[2])),
        grid=(n // tm,),
        in_specs=[row] * len(acts) + [_const_spec(wo.shape), layer_spec(w1_all), layer_spec(w2_all),
                                      _const_spec(ln.shape)],
        out_specs=row,
        out_shape=jax.ShapeDtypeStruct((n, d), F32),
        compiler_params=_params("parallel"),
        name="post_mlp",
    )(*acts, wo, w1_all, w2_all, ln)


def _fox_proj_kernel(x_ref, wq_ref, wk_ref, wv_ref, wf_ref, bf_ref,
                     q_ref, k_ref, v_ref, kb_ref, vb_ref, lf_ref, *lft_ref, channel_major):
    nb, rows, d = x_ref.shape
    xb = x_ref[...].reshape(nb * rows, d).astype(BF16)
    out = lambda z: z.reshape(nb, rows, z.shape[-1])
    q_ref[...] = out((_dot(xb, wq_ref[...]) * (HEAD ** -0.5 * LOG2E)).astype(BF16))
    k = _dot(xb, wk_ref[...])
    kb_ref[...] = out(k.astype(BF16))
    if channel_major:
        k_ref[0] = k.T
        v = _dot_nt(wv_ref[...], xb)
        v_ref[0] = v
        vb_ref[0] = v.astype(BF16)
    else:
        v = _dot(xb, wv_ref[...])
        k_ref[...] = out(k)
        v_ref[...] = out(v)
        vb_ref[...] = out(v.astype(BF16))
    f = _dot(xb, wf_ref[...]) + bf_ref[...]
    lf = jnp.minimum(f, 0.0) - jnp.log1p(jnp.exp(-jnp.abs(f)))
    lf_ref[...] = out(lf)
    if channel_major:
        lft_ref[0][0] = lf.T


def _fox_proj(x, wq, wk, wv, wf, bf, tm, channel_major):
    b, t, d = x.shape
    nb, tm = _seq_tile(b, t, tm)
    assert nb == 1 or not channel_major
    row = pl.BlockSpec((nb, tm, d), lambda bi, i: (bi, i, 0))
    rowf = pl.BlockSpec((nb, tm, LANES), lambda bi, i: (bi, i, 0))
    kv_spec = pl.BlockSpec((1, d, tm), lambda bi, i: (bi, 0, i)) if channel_major else row
    kv_shape = jax.ShapeDtypeStruct((b, d, t) if channel_major else (b, t, d), F32)
    act_bf = jax.ShapeDtypeStruct((b, t, d), BF16)
    vb_shape = jax.ShapeDtypeStruct(kv_shape.shape, BF16)
    consts = [wq, wk, wv, wf, bf]
    return pl.pallas_call(
        functools.partial(_fox_proj_kernel, channel_major=channel_major),
        grid=(b // nb, t // tm),
        in_specs=[row] + [_const_spec(c.shape) for c in consts],
        out_specs=[row, kv_spec, kv_spec, row, kv_spec, rowf]
        + [pl.BlockSpec((1, LANES, tm), lambda bi, i: (bi, 0, i))] * channel_major,
        out_shape=[act_bf, kv_shape, kv_shape, act_bf, vb_shape, jax.ShapeDtypeStruct((b, t, LANES), F32)]
        + [jax.ShapeDtypeStruct((b, LANES, t), F32)] * channel_major,
        compiler_params=_params("parallel", "parallel"),
        name="fox_proj",
    )(x, *consts)


CS_TILE = 512
SUM_ROWS = 16


def _cumsum_kernel(lf_ref, c_ref, carry_ref):
    @pl.when(pl.program_id(1) == 0)
    def _():
        carry_ref[...] = jnp.zeros_like(carry_ref)

    n = lf_ref.shape[1]
    cs = _dot3(lf_ref[0], _tri(n, upper=False), dot=lambda p, t: _dot(t, p)) + carry_ref[0:1, :]
    c_ref[0] = cs
    carry_ref[...] = jnp.broadcast_to(cs[n - 1:n, :], carry_ref.shape)


def _cumsum(lf):
    b, l, w = lf.shape
    tile = _row_tile(l, CS_TILE)
    spec = pl.BlockSpec((1, tile, w), lambda i, j: (i, j, 0))
    return pl.pallas_call(
        _cumsum_kernel,
        grid=(b, l // tile),
        in_specs=[spec],
        out_specs=spec,
        out_shape=jax.ShapeDtypeStruct((b, l, w), F32),
        scratch_shapes=[pltpu.VMEM((SUBLANES, w), F32)],
        compiler_params=_params("parallel", "arbitrary"),
        name="logf_cumsum",
    )(lf)


def _bias_lanes(c, head0, width, key_side):
    nh = width // HEAD
    src = lax.broadcasted_iota(jnp.int32, (LANES, width), 0)
    dst = lax.broadcasted_iota(jnp.int32, (LANES, width), 1)
    lane = lax.broadcasted_iota(jnp.int32, (1, width), 1)
    lo, sign = (3, -1.0) if key_side else (0, 1.0)
    base = [((hh + 1) % nh) * HEAD for hh in range(nh)]
    ones = functools.reduce(jnp.logical_or, [(lane >= bs + 3 - lo) & (lane < bs + 6 - lo) for bs in base])
    out = jnp.where(ones, 1.0, 0.0)
    for n, p in enumerate(_split3(c * LOG2E)):
        hit = functools.reduce(jnp.logical_or,
                               [(src == head0 + hh) & (dst == base[hh] + lo + n) for hh in range(nh)])
        out = out + _dot(p, jnp.where(hit, sign, 0.0).astype(BF16))
    return out


def _flash_kernel(q_ref, k_ref, vt_ref, c_ref, o_ref, qaug_scr, kaug_scr, *, tq):
    g = pl.program_id(1)
    i = pl.program_id(2)
    t = k_ref.shape[1]
    width = k_ref.shape[2]
    nh = width // HEAD
    lane = lax.broadcasted_iota(jnp.int32, (1, width), 1)
    in_head = [(lane >> HEAD_SHIFT) == hh for hh in range(nh)]
    bias_lanes = [(lane >= ((hh + 1) % nh) * HEAD) & (lane < ((hh + 1) % nh) * HEAD + 6) for hh in range(nh)]

    def augment(x, bias):
        bias = bias.astype(BF16)
        return [jnp.where(in_head[hh], x, jnp.where(bias_lanes[hh], bias, jnp.zeros((), BF16)))
                for hh in range(nh)]

    @pl.when(i == 0)
    def _():
        for blk in range(t // tq):
            rows = slice(blk * tq, (blk + 1) * tq)
            c = c_ref[0, rows, :]
            qa = augment(q_ref[0, rows, :], _bias_lanes(c, nh * g, width, key_side=False))
            ka = augment(k_ref[0, rows, :], _bias_lanes(c, nh * g, width, key_side=True))
            for hh in range(nh):
                qaug_scr[hh, rows, :] = qa[hh]
                kaug_scr[hh, rows, :] = ka[hh]

    row0 = pl.multiple_of(i * tq, tq)

    qg = min(tq, SLAB)
    chains = [(hh, qs) for hh in range(nh) for qs in range(0, tq, qg)]
    nc = len(chains)
    qa_c = [qaug_scr[hh, pl.ds(row0 + qs, qg), :] for hh, qs in chains]

    ones_rows = jnp.ones((SUM_ROWS, tq), BF16)

    def step(off, carry, mask):
        ks = [kaug_scr[hh, pl.ds(off, tq), :] for hh in range(nh)]
        vts = [jnp.concatenate([vt_ref[0, hh * HEAD:(hh + 1) * HEAD, pl.ds(off, tq)], ones_rows], axis=0)
               for hh in range(nh)]
        kext = [tq if mask is None else qs + qg for _, qs in chains]
        st = [_dot_nt(ks[hh][:kext[n]], qa_c[n]) for n, (hh, _) in enumerate(chains)]
        if mask is not None:
            st = [jnp.where(mask[:kext[n], qs:qs + qg], st[n], -jnp.inf) for n, (_, qs) in enumerate(chains)]
        m_new = [jnp.maximum(carry[n][0], jnp.max(st[n], axis=0, keepdims=True)) for n in range(nc)]
        alpha = [jnp.exp2(carry[n][0] - m_new[n]) for n in range(nc)]
        p = [jnp.exp2(st[n] - m_new[n]) for n in range(nc)]
        acc = [alpha[n] * carry[n][1] + _dot(vts[hh][:, :kext[n]], p[n].astype(BF16))
               for n, (hh, _) in enumerate(chains)]
        return tuple(zip(m_new, acc))

    init = tuple((jnp.full((1, qg), -jnp.inf, F32), jnp.zeros((HEAD + SUM_ROWS, qg), F32)) for _ in chains)
    carry = lax.fori_loop(0, i, lambda j, c: step(pl.multiple_of(j * tq, tq), c, None), init)
    causal = lax.broadcasted_iota(jnp.int32, (tq, tq), 0) <= lax.broadcasted_iota(jnp.int32, (tq, tq), 1)
    carry = step(row0, carry, causal)
    per_head = [jnp.concatenate([carry[n][1][:HEAD] / carry[n][1][HEAD:HEAD + 1]
                                 for n, (h2, _) in enumerate(chains) if h2 == hh], axis=1) for hh in range(nh)]
    o_ref[0] = jnp.concatenate(per_head, axis=0).T


def _flash(q, k, vt, c, tq, width):
    b, t, d = q.shape
    tq = _row_tile(t, tq)
    return pl.pallas_call(
        functools.partial(_flash_kernel, tq=tq),
        grid=(b, d // width, t // tq),
        in_specs=[pl.BlockSpec((1, t, width), lambda bi, gi, i: (bi, 0, gi)),
                  pl.BlockSpec((1, t, width), lambda bi, gi, i: (bi, 0, gi)),
                  pl.BlockSpec((1, width, t), lambda bi, gi, i: (bi, gi, 0)),
                  pl.BlockSpec((1, t, LANES), lambda bi, gi, i: (bi, 0, 0))],
        out_specs=pl.BlockSpec((1, tq, width), lambda bi, gi, i: (bi, i, gi)),
        out_shape=jax.ShapeDtypeStruct((b, t, d), F32),
        scratch_shapes=[pltpu.VMEM((width // HEAD, t, width), BF16)] * 2,
        compiler_params=_params("parallel", "parallel", "arbitrary"),
        name="fox_flash",
    )(q, k, vt, c)


DEC_TILE = 512
DEC_WIDTH = 512


def _decode_attn_kernel(q_ref, ckt_ref, cvt_ref, kn_ref, vn_ref, clf_ref, lfn_ref, o_ref, c_scr):
    g = pl.program_id(1)
    past = ckt_ref.shape[2]
    t = q_ref.shape[1]

    @pl.when(g == 0)
    def _():
        tri = _tri(DEC_TILE, upper=True)
        carry = jnp.zeros((clf_ref.shape[1], 1), F32)
        for blk in range(past // DEC_TILE):
            cols = slice(blk * DEC_TILE, (blk + 1) * DEC_TILE)
            cs = _dot3(clf_ref[0, :, cols], tri) + carry
            c_scr[:, cols] = cs
            carry = cs[:, DEC_TILE - 1:DEC_TILE]

    width = q_ref.shape[2]
    nh = width // HEAD
    lane = lax.broadcasted_iota(jnp.int32, (1, width), 1)
    hlane = lax.broadcasted_iota(jnp.int32, (1, LANES), 1)
    q = q_ref[0]
    kct = ckt_ref[0].astype(BF16)
    vct = cvt_ref[0].astype(BF16)
    kn = kn_ref[0]
    vn = vn_ref[0]
    causal = lax.broadcasted_iota(jnp.int32, (t, t), 1) <= lax.broadcasted_iota(jnp.int32, (t, t), 0)
    lf_new = _dot3(lfn_ref[0], _tri(t, upper=False), dot=lambda p, tr: _dot(tr, p))

    H = range(nh)
    head = [nh * g + hh for hh in H]
    in_head = [(lane >> HEAD_SHIFT) == hh for hh in H]
    rows = lambda parts: jnp.concatenate(parts, axis=0)
    qs = rows([jnp.where(in_head[hh], q, jnp.zeros((), BF16)) for hh in H])
    c_cache = [c_scr[pl.ds(head[hh], 1), :] for hh in H]
    total = [c_cache[hh][:, past - 1:past] for hh in H]
    onehot = [jnp.where(hlane == head[hh], 1.0, 0.0) for hh in H]
    cn_col = [jnp.sum(lf_new * onehot[hh], axis=-1, keepdims=True) + total[hh] for hh in H]
    cn_row = [_dot3(lf_new, jnp.broadcast_to(onehot[hh], (SUBLANES, LANES)).astype(BF16),
                    dot=lambda p, e: _dot_nt(e, p))[0:1] + total[hh] for hh in H]
    s1 = _dot(qs, kct) + rows([cn_col[hh] - c_cache[hh] for hh in H]) * LOG2E
    s2 = _dot_nt(qs, kn) + rows([cn_col[hh] - cn_row[hh] for hh in H]) * LOG2E
    s2 = jnp.where(rows([causal] * nh), s2, -jnp.inf)
    m = jnp.maximum(jnp.max(s1, axis=-1, keepdims=True), jnp.max(s2, axis=-1, keepdims=True))
    e1 = jnp.exp2(s1 - m)
    e2 = jnp.exp2(s2 - m)
    l = jnp.sum(e1, axis=-1, keepdims=True) + jnp.sum(e2, axis=-1, keepdims=True)
    o = (_dot_nt(e1.astype(BF16), vct) + _dot(e2.astype(BF16), vn)) / l
    out = o[(nh - 1) * t:]
    for hh in range(nh - 2, -1, -1):
        out = jnp.where(in_head[hh], o[hh * t:(hh + 1) * t], out)
    o_ref[0] = out


def _decode_attn(q, cache_kt, cache_vt, kn, vn, cache_lf, lf_new):
    b, t, d = q.shape
    p = cache_kt.shape[2]
    h = cache_lf.shape[1]
    assert p % DEC_TILE == 0
    new = pl.BlockSpec((1, t, DEC_WIDTH), lambda bi, gi: (bi, 0, gi))
    old = pl.BlockSpec((1, DEC_WIDTH, p), lambda bi, gi: (bi, gi, 0))
    return pl.pallas_call(
        _decode_attn_kernel,
        grid=(b, d // DEC_WIDTH),
        in_specs=[new, old, old, new, new,
                  pl.BlockSpec((1, h, p), lambda bi, gi: (bi, 0, 0)),
                  pl.BlockSpec((1, t, LANES), lambda bi, gi: (bi, 0, 0))],
        out_specs=new,
        out_shape=jax.ShapeDtypeStruct((b, t, d), F32),
        scratch_shapes=[pltpu.VMEM((h, p), F32)],
        compiler_params=_params("parallel", "arbitrary"),
        name="fox_decode_attn",
    )(q, cache_kt, cache_vt, kn, vn, cache_lf, lf_new)


def _pad_cols(w, mult):
    pad = (-w.shape[-1]) % mult
    return jnp.pad(w, ((0, 0), (0, pad))) if pad else w


def _pad_rows(w, mult):
    pad = (-w.shape[0]) % mult
    return jnp.pad(w, ((0, pad), (0, 0))) if pad else w


def kernel(x_prompt, x_sample, state_wkv, state_shift, cache_k, cache_v, cache_logf, rwkv_mu, rwkv_w0, rwkv_w1, rwkv_w2, rwkv_a0, rwkv_a1, rwkv_a2, rwkv_g1, rwkv_g2, rwkv_k_k, rwkv_k_a, rwkv_r_k, rwkv_w_r, rwkv_w_k, rwkv_w_v, rwkv_w_o, rwkv_lnx_g, rwkv_lnx_b, fox_w_in, fox_b_f, fox_w_o, ffn_w1, ffn_w2, ln_mix_g, ln_mix_b, ln_ffn_g, ln_ffn_b):
    depth = ln_mix_g.shape[0]
    alpha = (2 * depth) ** 0.25
    bp, tp, d = x_prompt.shape
    bs, ts, _ = x_sample.shape
    heads = d // HEAD
    past = cache_k.shape[2]
    tm = 512

    xs_by_group = {"p": x_prompt, "s": x_sample}
    outs = {grp: {n: [] for n in ("wkv", "shift", "k", "v", "lf")} for grp in ("p", "s")}
    w1_all = ffn_w1.astype(BF16)
    w2_all = ffn_w2.astype(BF16)
    for i in range(depth):
        j = i // 2
        ln = jnp.stack([ln_mix_g[i], ln_mix_b[i], ln_ffn_g[i], ln_ffn_b[i]])
        if i % 2 == 0:
            row = lambda z: z.reshape(1, d)
            proj = (rwkv_mu[j], row(rwkv_w0[j]), row(rwkv_a0[j]),
                    rwkv_w_r[j].astype(BF16), rwkv_w_k[j].astype(BF16), rwkv_w_v[j].astype(BF16),
                    _pad_cols(rwkv_w1[j], LANES).astype(BF16), _pad_rows(rwkv_w2[j], LANES).astype(BF16),
                    _pad_cols(rwkv_a1[j], LANES).astype(BF16), _pad_rows(rwkv_a2[j], LANES).astype(BF16),
                    _pad_cols(rwkv_g1[j], LANES).astype(BF16), _pad_rows(rwkv_g2[j], LANES).astype(BF16))
            wkv_vecs = (row(rwkv_k_k[j]), row(rwkv_k_a[j]), row(rwkv_r_k[j]), row(rwkv_lnx_g[j]), row(rwkv_lnx_b[j]))
            wo = rwkv_w_o[j].astype(BF16)
            starts = {"p": (jnp.zeros((bp, d), x_prompt.dtype), jnp.zeros((bp, heads, HEAD, HEAD), state_wkv.dtype)),
                      "s": (state_shift[j], state_wkv[j])}
            for grp in ("p", "s"):
                x = xs_by_group[grp]
                b, t, _ = x.shape
                shift0, wkv0 = starts[grp]
                r, lw, k, v, a, gate = _rwkv_proj(x, shift0.reshape(b, 1, d), *proj, tm=tm)
                o, s_fin = _wkv(r, lw, k, v, a, *wkv_vecs, wkv0.astype(F32))
                outs[grp]["wkv"].append(s_fin.astype(wkv0.dtype))
                outs[grp]["shift"].append(x[:, -1])
                xs_by_group[grp] = _post(o.reshape(b * t, d), gate.reshape(b * t, d), x.reshape(b * t, d),
                                         wo, w1_all, w2_all, i, ln, alpha, tm).reshape(b, t, d)
        else:
            w_in = fox_w_in[j]
            wq = w_in[:, :d].astype(BF16)
            wk = w_in[:, d:2 * d].astype(BF16)
            wv = w_in[:, 2 * d:3 * d].astype(BF16)
            wf = _pad_cols(w_in[:, 3 * d:], LANES).astype(BF16)
            bf = _pad_cols(fox_b_f[j].reshape(1, heads), LANES)
            wo = fox_w_o[j].astype(BF16)

            xp = xs_by_group["p"]
            q, kt, vt, kb, vtb, lf, lft = _fox_proj(xp, wq, wk, wv.T, wf, bf, tm, channel_major=True)
            o = _flash(q, kb, vtb, _cumsum(lf), tq=512, width=SLAB)
            to_heads = lambda z: jnp.transpose(z.reshape(bp, heads, HEAD, tp), (0, 3, 1, 2))
            outs["p"]["k"].append(to_heads(kt))
            outs["p"]["v"].append(to_heads(vt))
            outs["p"]["lf"].append(jnp.transpose(lft[:, :heads, :], (0, 2, 1)))
            xs_by_group["p"] = _post(o.reshape(bp * tp, d), None, xp.reshape(bp * tp, d),
                                     wo, w1_all, w2_all, i, ln, alpha, tm).reshape(bp, tp, d)

            xs = xs_by_group["s"]
            q, k, v, kb, vb, lf = _fox_proj(xs, wq, wk, wv, wf, bf, tm, channel_major=False)
            channel_major = lambda z: jnp.transpose(z, (0, 2, 3, 1)).reshape(bs, d, past)
            o = _decode_attn(q, channel_major(cache_k[j]), channel_major(cache_v[j]), kb, vb,
                             jnp.transpose(cache_logf[j].astype(F32), (0, 2, 1)), lf)
            outs["s"]["k"].append(k.reshape(bs, ts, heads, HEAD))
            outs["s"]["v"].append(v.reshape(bs, ts, heads, HEAD))
            outs["s"]["lf"].append(lf[:, :, :heads])
            xs_by_group["s"] = _post(o.reshape(bs * ts, d), None, xs.reshape(bs * ts, d),
                                     wo, w1_all, w2_all, i, ln, alpha, tm).reshape(bs, ts, d)

    st = jnp.stack
    op, os_ = outs["p"], outs["s"]
    return (xs_by_group["p"], xs_by_group["s"],
            st(op["wkv"]), st(op["shift"]), st(op["k"]), st(op["v"]), st(op["lf"]),
            st(os_["wkv"]), st(os_["shift"]), st(os_["k"]), st(os_["v"]), st(os_["lf"]))
```

```python
import functools
import math

import jax
import jax.numpy as jnp
from jax import lax
from jax.experimental import pallas as pl
from jax.experimental.pallas import tpu as pltpu

BF16 = jnp.bfloat16
F32 = jnp.float32

HEAD = 64
HEAD_SHIFT = 6
LANES = 128
SUBLANES = 8
SLAB = 256
HEADS_PER_SLAB = SLAB // HEAD
CHUNK = 64
WKV_CELLS_PER_STEP = 8
LN_EPS = 1e-5
GN_EPS = 64e-5
EXP_M05 = math.exp(-0.5)
LOG2E = math.log2(math.e)
VMEM_LIMIT = 56 * 1024 * 1024


def _dot(a, b):
    return jnp.dot(a, b, preferred_element_type=F32)


def _dot_nt(a, b):
    return lax.dot_general(a, b, (((1,), (1,)), ((), ())), preferred_element_type=F32)


def _dot_tn(a, b):
    return lax.dot_general(a, b, (((0,), (0,)), ((), ())), preferred_element_type=F32)


def _split3(x):
    p1 = x.astype(BF16)
    r1 = x - p1.astype(F32)
    p2 = r1.astype(BF16)
    p3 = (r1 - p2.astype(F32)).astype(BF16)
    return p1, p2, p3


def _dot3(a, b, dot=_dot, pieces=3):
    return sum(dot(p, b) for p in _split3(a)[:pieces])


def _tri(n, upper):
    r = lax.broadcasted_iota(jnp.int32, (n, n), 0)
    c = lax.broadcasted_iota(jnp.int32, (n, n), 1)
    return jnp.where((r <= c) if upper else (c <= r), 1.0, 0.0).astype(BF16)


def _layer_norm(z, g, b):
    mu = jnp.mean(z, axis=-1, keepdims=True)
    zc = z - mu
    var = jnp.mean(zc * zc, axis=-1, keepdims=True)
    return zc * lax.rsqrt(var + LN_EPS) * g + b


def _params(*sem):
    return pltpu.CompilerParams(dimension_semantics=sem, vmem_limit_bytes=VMEM_LIMIT)


def _const_spec(shape):
    nd = len(shape)
    return pl.BlockSpec(shape, lambda *_: (0,) * nd, pipeline_mode=pl.Buffered(1))


def _row_tile(n, want):
    t = min(n, want)
    assert n % t == 0 and t % SUBLANES == 0, (n, t)
    return t


def _seq_tile(b, t, want):
    tm = _row_tile(t, want)
    nb = min(b, max(want // t, 1)) if tm == t else 1
    assert b % nb == 0, (b, nb)
    return nb, tm


def _rwkv_proj_kernel(x_ref, prev_ref, first_ref, mu_ref, w0_ref, a0_ref, wr_ref, wk_ref, wv_ref,
                      w1_ref, w2_ref, a1_ref, a2_ref, g1_ref, g2_ref,
                      r_ref, lw_ref, k_ref, v_ref, a_ref, g_ref):
    nb, rows, d = x_ref.shape
    x = x_ref[...].reshape(nb * rows, d)
    row_in_seq = lax.broadcasted_iota(jnp.int32, x.shape, 0) % rows
    if nb == 1:
        row0 = jnp.where(pl.program_id(1) == 0, first_ref[0], prev_ref[0, SUBLANES - 1:SUBLANES, :])
    else:
        row0 = jnp.broadcast_to(first_ref[...], (nb, rows, d)).reshape(nb * rows, d)
    xx = jnp.where(row_in_seq == 0, row0, pltpu.roll(x, 1, 0)) - x
    out = lambda z: z.reshape(nb, rows, d)

    def mix(i):
        return (x + xx * mu_ref[i:i + 1, :]).astype(BF16)

    r_ref[...] = out(_dot(mix(0), wr_ref[...]))
    w_raw = w0_ref[...] + _dot(jnp.tanh(_dot(mix(1), w1_ref[...])).astype(BF16), w2_ref[...])
    lw_ref[...] = out(-EXP_M05 * jax.nn.sigmoid(w_raw))
    k_ref[...] = out(_dot(mix(2), wk_ref[...]))
    v_ref[...] = out(_dot(mix(3), wv_ref[...]))
    a_ref[...] = out(jax.nn.sigmoid(a0_ref[...] + _dot(_dot(mix(4), a1_ref[...]).astype(BF16), a2_ref[...])))
    g_ref[...] = out(_dot(jax.nn.sigmoid(_dot(mix(5), g1_ref[...])).astype(BF16), g2_ref[...]))


def _rwkv_proj(x, shift0, mu, w0, a0, wr, wk, wv, w1, w2, a1, a2, g1, g2, tm):
    b, t, d = x.shape
    nb, tm = _seq_tile(b, t, tm)
    per_tile = tm // SUBLANES
    row = pl.BlockSpec((nb, tm, d), lambda bi, i: (bi, i, 0))
    prev = pl.BlockSpec((1, SUBLANES, d), lambda bi, i: (bi * nb, jnp.maximum(i * per_tile - 1, 0), 0))
    first = pl.BlockSpec((nb, 1, d), lambda bi, i: (bi, 0, 0))
    consts = [mu, w0, a0, wr, wk, wv, w1, w2, a1, a2, g1, g2]
    return pl.pallas_call(
        _rwkv_proj_kernel,
        grid=(b // nb, t // tm),
        in_specs=[row, prev, first] + [_const_spec(c.shape) for c in consts],
        out_specs=[row] * 6,
        out_shape=[jax.ShapeDtypeStruct((b, t, d), F32)] * 6,
        compiler_params=_params("parallel", "parallel"),
        name="rwkv_proj",
    )(x, x, shift0, *consts)


def _wkv_kernel(r_ref, lw_ref, k_ref, v_ref, a_ref, kk_ref, ka_ref, rk_ref, lg_ref, lb_ref, h0_ref,
                o_ref, hout_ref, h_scr, *, last_chunk):
    c = pl.program_id(1)
    nslab = h_scr.shape[0]
    C = CHUNK
    nseq, rows = r_ref.shape[:2]
    nslab = nslab // nseq

    @pl.when(c == 0)
    def _():
        h_scr[...] = jnp.zeros_like(h_scr)
        for n in range(nseq * nslab):
            for h in range(HEADS_PER_SLAB):
                hs = slice(h * HEAD, (h + 1) * HEAD)
                h_scr[n, hs, hs] = h0_ref[n // nslab, (n % nslab) * HEADS_PER_SLAB + h]

    row_s = lax.broadcasted_iota(jnp.int32, (SLAB, SLAB), 0)
    col_s = lax.broadcasted_iota(jnp.int32, (SLAB, SLAB), 1)
    mask_bd = (row_s >> HEAD_SHIFT) == (col_s >> HEAD_SHIFT)
    row_c = lax.broadcasted_iota(jnp.int32, (C, SLAB), 0)
    colin_c = lax.broadcasted_iota(jnp.int32, (C, SLAB), 1) & (C - 1)
    strict_lo = colin_c < row_c
    incl_lo = colin_c <= row_c
    eye_c = colin_c == row_c
    ones_bd = jnp.where(mask_bd, 1.0, 0.0).astype(BF16)

    def bd(x):
        xb = x.astype(BF16)
        return jnp.where(mask_bd, jnp.concatenate([xb] * HEADS_PER_SLAB, axis=0), jnp.zeros((), BF16))

    def head_sum(x):
        return _dot(x.astype(BF16), ones_bd)

    nck = max(rows // C, 1)
    live = min(rows, C)

    ncell = nseq * nck

    def chunk_rows(ref, ci, sl):
        x = ref[ci // nck, (ci % nck) * C:(ci % nck) * C + live, sl]
        if live < C:
            x = jnp.concatenate([x, jnp.zeros((C - live, x.shape[1]), x.dtype)], axis=0)
        return x

    lw_c = [chunk_rows(lw_ref, ci, slice(None)) for ci in range(ncell)]
    L_c = [_dot3(lw_c[ci], _tri(C, upper=False), dot=lambda p, t: _dot(t, p), pieces=2) for ci in range(ncell)]

    units = [(ci, s) for ci in range(ncell) for s in range(nslab)]
    S = range(len(units))
    slabs = [slice(s * SLAB, (s + 1) * SLAB) for _, s in units]
    r = [chunk_rows(r_ref, ci, slabs[u]) for u, (ci, _) in enumerate(units)]
    v = [chunk_rows(v_ref, ci, slabs[u]) for u, (ci, _) in enumerate(units)]
    asig = [chunk_rows(a_ref, ci, slabs[u]) for u, (ci, _) in enumerate(units)]
    k0 = [chunk_rows(k_ref, ci, slabs[u]) for u, (ci, _) in enumerate(units)]
    lw = [lw_c[ci][:, slabs[u]] for u, (ci, _) in enumerate(units)]
    kk = [k0[s] * kk_ref[:, slabs[s]] for s in S]
    k = [k0[s] * (1.0 + (asig[s] - 1.0) * ka_ref[:, slabs[s]]) for s in S]
    sums = [head_sum(jnp.concatenate([kk[s] * kk[s], r[s] * k[s] * rk_ref[:, slabs[s]]], axis=0)) for s in S]
    kk = [kk[s] / jnp.maximum(jnp.sqrt(sums[s][:C]), 1e-12) for s in S]
    bonus = [sums[s][C:] * v[s] for s in S]
    bv = [kk[s] * asig[s] for s in S]
    L = [L_c[ci][:, slabs[u]] for u, (ci, _) in enumerate(units)]
    LC = [L[s][C - 1:C, :] for s in S]
    enL = [jnp.exp(-L[s]) for s in S]
    eLc = [jnp.exp(LC[s] - L[s]) for s in S]
    At = [-kk[s] * jnp.exp(L[s] - lw[s]) for s in S]
    Rt = [r[s] * jnp.exp(L[s]) for s in S]
    Bt = [bv[s] * enL[s] for s in S]
    Kt = [k[s] * enL[s] for s in S]
    Bh = [bv[s] * eLc[s] for s in S]
    Kh = [k[s] * eLc[s] for s in S]

    lhs = [jnp.concatenate([At[s], Rt[s]], axis=0).astype(BF16) for s in S]
    ab = [_dot_nt(lhs[s], bd(Bt[s])) for s in S]
    ak = [_dot_nt(lhs[s], bd(Kt[s])) for s in S]
    N = [jnp.where(strict_lo, ab[s][:C], 0.0) for s in S]
    M = [jnp.where(strict_lo, ak[s][:C], 0.0) for s in S]
    Arb = [jnp.where(incl_lo, ab[s][C:], 0.0) for s in S]
    Ark = [jnp.where(incl_lo, ak[s][C:], 0.0) for s in S]
    mv = [_dot(jnp.concatenate([M[s], Ark[s]], axis=0).astype(BF16), bd(v[s])) for s in S]
    MV = [mv[s][:C] for s in S]

    X = [jnp.where(eye_c, 1.0, 0.0) + N[s] for s in S]
    A = [_dot(N[s].astype(BF16), bd(N[s])) for s in S]
    for _ in range(int(math.log2(C)) - 2):
        both = [_dot(jnp.concatenate([A[s], X[s]], axis=0).astype(BF16), bd(A[s])) for s in S]
        X = [X[s] + both[s][C:] for s in S]
        A = [both[s][:C] for s in S]
    X = [X[s] + _dot(X[s].astype(BF16), bd(A[s])) for s in S]

    wu = [_dot(X[s].astype(BF16), jnp.concatenate([bd(At[s]), bd(MV[s])], axis=1)) for s in S]
    WA = [wu[s][:, :SLAB] for s in S]
    UV = [wu[s][:, SLAB:] for s in S]
    au = [_dot(Arb[s].astype(BF16), jnp.concatenate([bd(WA[s]), bd(UV[s])], axis=1)) for s in S]
    Rp = [Rt[s] + au[s][:, :SLAB] for s in S]
    Yl = [au[s][:, SLAB:] + mv[s][C:] for s in S]
    rw = [jnp.concatenate([Rp[s], WA[s]], axis=0).astype(BF16) for s in S]
    bk = [jnp.concatenate([Bh[s], Kh[s]], axis=0).astype(BF16) for s in S]
    pc = [jnp.exp(LC[s]) for s in S]

    h = [h_scr[n] for n in range(nseq * nslab)]
    y = [None] * len(units)
    for cj in range(nck):
        us = [((n // nslab) * nck + cj) * nslab + n % nslab for n in range(nseq * nslab)]
        yu = [_dot_nt(rw[u], h[n].astype(BF16)) for n, u in enumerate(us)]
        for n, u in enumerate(us):
            y[u] = yu[n][:C] + Yl[u]
        uv = [jnp.concatenate([yu[n][C:] + UV[u], v[u]], axis=0).astype(BF16) for n, u in enumerate(us)]
        h = [h[n] * pc[u] + jnp.where(mask_bd, _dot_tn(uv[n], bk[u]), 0.0) for n, u in enumerate(us)]
    for n in range(nseq * nslab):
        h_scr[n] = h[n]

    mu = [head_sum(y[s]) * (1.0 / HEAD) for s in S]
    yc = [y[s] - mu[s] for s in S]
    var = [head_sum(yc[s] * yc[s]) * (1.0 / HEAD) for s in S]
    for u, (ci, _) in enumerate(units):
        sl = slabs[u]
        out = yc[u] * lax.rsqrt(var[u] + GN_EPS) * lg_ref[:, sl] + lb_ref[:, sl] + bonus[u]
        o_ref[ci // nck, (ci % nck) * C:(ci % nck) * C + live, sl] = out[:live]

    @pl.when(c == last_chunk)
    def _():
        for n in range(nseq * nslab):
            for h in range(HEADS_PER_SLAB):
                hs = slice(h * HEAD, (h + 1) * HEAD)
                hout_ref[n // nslab, (n % nslab) * HEADS_PER_SLAB + h] = h_scr[n, hs, hs]


def _wkv(r, lw, k, v, a, k_k, k_a, r_k, lnx_g, lnx_b, h0):
    b, t, d = r.shape
    if t % (WKV_CELLS_PER_STEP * CHUNK) == 0:
        nseq, rows = 1, WKV_CELLS_PER_STEP * CHUNK
    else:
        nseq, rows = (WKV_CELLS_PER_STEP if t <= CHUNK and b % WKV_CELLS_PER_STEP == 0 else 1), min(t, CHUNK)
    assert t % rows == 0 and rows % SUBLANES == 0
    nslab = d // SLAB
    heads = d // HEAD
    seq = pl.BlockSpec((nseq, rows, d), lambda i, c: (i, c, 0))
    vec = pl.BlockSpec((1, d), lambda i, c: (0, 0))
    hspec = pl.BlockSpec((nseq, heads, HEAD, HEAD), lambda i, c: (i, 0, 0, 0))
    return pl.pallas_call(
        functools.partial(_wkv_kernel, last_chunk=t // rows - 1),
        grid=(b // nseq, t // rows),
        in_specs=[seq] * 5 + [vec] * 5 + [hspec],
        out_specs=[seq, hspec],
        out_shape=[jax.ShapeDtypeStruct((b, t, d), F32),
                   jax.ShapeDtypeStruct((b, heads, HEAD, HEAD), F32)],
        scratch_shapes=[pltpu.VMEM((nseq * nslab, SLAB, SLAB), F32)],
        compiler_params=_params("parallel", "arbitrary"),
        name="wkv_chunked",
    )(r, lw, k, v, a, k_k, k_a, r_k, lnx_g, lnx_b, h0)


def _post_kernel(*refs, alpha, gated, ff_chunk):
    if gated:
        pre_ref, gate_ref, x_ref, wo_ref, w1_ref, w2_ref, ln_ref, out_ref = refs
        pre = pre_ref[...] * gate_ref[...]
    else:
        pre_ref, x_ref, wo_ref, w1_ref, w2_ref, ln_ref, out_ref = refs
        pre = pre_ref[...]
    h = _dot(pre.astype(BF16), wo_ref[...])
    x1 = _layer_norm(alpha * x_ref[...] + h, ln_ref[0:1, :], ln_ref[1:2, :])
    x1b = x1.astype(BF16)
    m = jnp.zeros_like(x1)
    for f in range(w1_ref.shape[1] // ff_chunk):
        fs = slice(f * ff_chunk, (f + 1) * ff_chunk)
        hid = jnp.maximum(_dot(x1b, w1_ref[:, fs]), 0.0)
        m = m + _dot((hid * hid).astype(BF16), w2_ref[fs, :])
    out_ref[...] = _layer_norm(alpha * x1 + m, ln_ref[2:3, :], ln_ref[3:4, :])


def _post(pre, gate, x, wo, w1_all, w2_all, layer, ln, alpha, tm):
    n, d = x.shape
    tm = _row_tile(n, tm)
    row = pl.BlockSpec((tm, d), lambda i: (i, 0))
    acts = [pre, x] if gate is None else [pre, gate, x]
    layer_spec = lambda w: pl.BlockSpec((None,) + w.shape[1:], lambda i: (layer, 0, 0), pipeline_mode=pl.Buffered(1))
    return pl.pallas_call(
        functools.partial(_post_kernel, alpha=alpha, gated=gate is not None, ff_chunk=min(1024, w1_all.shape[2])),
        grid=(n // tm,),
        in_specs=[row] * len(acts) + [_const_spec(wo.shape), layer_spec(w1_all), layer_spec(w2_all),
                                      _const_spec(ln.shape)],
        out_specs=row,
        out_shape=jax.ShapeDtypeStruct((n, d), F32),
        compiler_params=_params("parallel"),
        name="post_mlp",
    )(*acts, wo, w1_all, w2_all, ln)


def _fox_proj_kernel(x_ref, wq_ref, wk_ref, wv_ref, wf_ref, bf_ref,
                     q_ref, k_ref, v_ref, kb_ref, vb_ref, lf_ref, *lft_ref, channel_major):
    nb, rows, d = x_ref.shape
    xb = x_ref[...].reshape(nb * rows, d).astype(BF16)
    out = lambda z: z.reshape(nb, rows, z.shape[-1])
    q_ref[...] = out((_dot(xb, wq_ref[...]) * (HEAD ** -0.5 * LOG2E)).astype(BF16))
    k = _dot(xb, wk_ref[...])
    kb_ref[...] = out(k.astype(BF16))
    if channel_major:
        k_ref[0] = k.T
        v = _dot_nt(wv_ref[...], xb)
        v_ref[0] = v
        vb_ref[0] = v.astype(BF16)
    else:
        v = _dot(xb, wv_ref[...])
        k_ref[...] = out(k)
        v_ref[...] = out(v)
        vb_ref[...] = out(v.astype(BF16))
    f = _dot(xb, wf_ref[...]) + bf_ref[...]
    lf = jnp.minimum(f, 0.0) - jnp.log1p(jnp.exp(-jnp.abs(f)))
    lf_ref[...] = out(lf)
    if channel_major:
        lft_ref[0][0] = lf.T


def _fox_proj(x, wq, wk, wv, wf, bf, tm, channel_major):
    b, t, d = x.shape
    nb, tm = _seq_tile(b, t, tm)
    assert nb == 1 or not channel_major
    row = pl.BlockSpec((nb, tm, d), lambda bi, i: (bi, i, 0))
    rowf = pl.BlockSpec((nb, tm, LANES), lambda bi, i: (bi, i, 0))
    kv_spec = pl.BlockSpec((1, d, tm), lambda bi, i: (bi, 0, i)) if channel_major else row
    kv_shape = jax.ShapeDtypeStruct((b, d, t) if channel_major else (b, t, d), F32)
    act_bf = jax.ShapeDtypeStruct((b, t, d), BF16)
    vb_shape = jax.ShapeDtypeStruct(kv_shape.shape, BF16)
    consts = [wq, wk, wv, wf, bf]
    return pl.pallas_call(
        functools.partial(_fox_proj_kernel, channel_major=channel_major),
        grid=(b // nb, t // tm),
        in_specs=[row] + [_const_spec(c.shape) for c in consts],
        out_specs=[row, kv_spec, kv_spec, row, kv_spec, rowf]
        + [pl.BlockSpec((1, LANES, tm), lambda bi, i: (bi, 0, i))] * channel_major,
        out_shape=[act_bf, kv_shape, kv_shape, act_bf, vb_shape, jax.ShapeDtypeStruct((b, t, LANES), F32)]
        + [jax.ShapeDtypeStruct((b, LANES, t), F32)] * channel_major,
        compiler_params=_params("parallel", "parallel"),
        name="fox_proj",
    )(x, *consts)


CS_TILE = 512
SUM_ROWS = 16


def _cumsum_kernel(lf_ref, c_ref, carry_ref):
    @pl.when(pl.program_id(1) == 0)
    def _():
        carry_ref[...] = jnp.zeros_like(carry_ref)

    n = lf_ref.shape[1]
    cs = _dot3(lf_ref[0], _tri(n, upper=False), dot=lambda p, t: _dot(t, p)) + carry_ref[0:1, :]
    c_ref[0] = cs
    carry_ref[...] = jnp.broadcast_to(cs[n - 1:n, :], carry_ref.shape)


def _cumsum(lf):
    b, l, w = lf.shape
    tile = _row_tile(l, CS_TILE)
    spec = pl.BlockSpec((1, tile, w), lambda i, j: (i, j, 0))
    return pl.pallas_call(
        _cumsum_kernel,
        grid=(b, l // tile),
        in_specs=[spec],
        out_specs=spec,
        out_shape=jax.ShapeDtypeStruct((b, l, w), F32),
        scratch_shapes=[pltpu.VMEM((SUBLANES, w), F32)],
        compiler_params=_params("parallel", "arbitrary"),
        name="logf_cumsum",
    )(lf)


def _bias_lanes(c, head0, width, key_side):
    nh = width // HEAD
    src = lax.broadcasted_iota(jnp.int32, (LANES, width), 0)
    dst = lax.broadcasted_iota(jnp.int32, (LANES, width), 1)
    lane = lax.broadcasted_iota(jnp.int32, (1, width), 1)
    lo, sign = (3, -1.0) if key_side else (0, 1.0)
    base = [((hh + 1) % nh) * HEAD for hh in range(nh)]
    ones = functools.reduce(jnp.logical_or, [(lane >= bs + 3 - lo) & (lane < bs + 6 - lo) for bs in base])
    out = jnp.where(ones, 1.0, 0.0)
    for n, p in enumerate(_split3(c * LOG2E)):
        hit = functools.reduce(jnp.logical_or,
                               [(src == head0 + hh) & (dst == base[hh] + lo + n) for hh in range(nh)])
        out = out + _dot(p, jnp.where(hit, sign, 0.0).astype(BF16))
    return out


def _flash_kernel(q_ref, k_ref, vt_ref, c_ref, o_ref, qaug_scr, kaug_scr, *, tq):
    g = pl.program_id(1)
    i = pl.program_id(2)
    t = k_ref.shape[1]
    width = k_ref.shape[2]
    nh = width // HEAD
    lane = lax.broadcasted_iota(jnp.int32, (1, width), 1)
    in_head = [(lane >> HEAD_SHIFT) == hh for hh in range(nh)]
    bias_lanes = [(lane >= ((hh + 1) % nh) * HEAD) & (lane < ((hh + 1) % nh) * HEAD + 6) for hh in range(nh)]

    def augment(x, bias):
        bias = bias.astype(BF16)
        return [jnp.where(in_head[hh], x, jnp.where(bias_lanes[hh], bias, jnp.zeros((), BF16)))
                for hh in range(nh)]

    @pl.when(i == 0)
    def _():
        for blk in range(t // tq):
            rows = slice(blk * tq, (blk + 1) * tq)
            c = c_ref[0, rows, :]
            qa = augment(q_ref[0, rows, :], _bias_lanes(c, nh * g, width, key_side=False))
            ka = augment(k_ref[0, rows, :], _bias_lanes(c, nh * g, width, key_side=True))
            for hh in range(nh):
                qaug_scr[hh, rows, :] = qa[hh]
                kaug_scr[hh, rows, :] = ka[hh]

    row0 = pl.multiple_of(i * tq, tq)

    qg = min(tq, SLAB)
    chains = [(hh, qs) for hh in range(nh) for qs in range(0, tq, qg)]
    nc = len(chains)
    qa_c = [qaug_scr[hh, pl.ds(row0 + qs, qg), :] for hh, qs in chains]

    ones_rows = jnp.ones((SUM_ROWS, tq), BF16)

    def step(off, carry, mask):
        ks = [kaug_scr[hh, pl.ds(off, tq), :] for hh in range(nh)]
        vts = [jnp.concatenate([vt_ref[0, hh * HEAD:(hh + 1) * HEAD, pl.ds(off, tq)], ones_rows], axis=0)
               for hh in range(nh)]
        kext = [tq if mask is None else qs + qg for _, qs in chains]
        st = [_dot_nt(ks[hh][:kext[n]], qa_c[n]) for n, (hh, _) in enumerate(chains)]
        if mask is not None:
            st = [jnp.where(mask[:kext[n], qs:qs + qg], st[n], -jnp.inf) for n, (_, qs) in enumerate(chains)]
        m_new = [jnp.maximum(carry[n][0], jnp.max(st[n], axis=0, keepdims=True)) for n in range(nc)]
        alpha = [jnp.exp2(carry[n][0] - m_new[n]) for n in range(nc)]
        p = [jnp.exp2(st[n] - m_new[n]) for n in range(nc)]
        acc = [alpha[n] * carry[n][1] + _dot(vts[hh][:, :kext[n]], p[n].astype(BF16))
               for n, (hh, _) in enumerate(chains)]
        return tuple(zip(m_new, acc))

    init = tuple((jnp.full((1, qg), -jnp.inf, F32), jnp.zeros((HEAD + SUM_ROWS, qg), F32)) for _ in chains)
    carry = lax.fori_loop(0, i, lambda j, c: step(pl.multiple_of(j * tq, tq), c, None), init)
    causal = lax.broadcasted_iota(jnp.int32, (tq, tq), 0) <= lax.broadcasted_iota(jnp.int32, (tq, tq), 1)
    carry = step(row0, carry, causal)
    per_head = [jnp.concatenate([carry[n][1][:HEAD] / carry[n][1][HEAD:HEAD + 1]
                                 for n, (h2, _) in enumerate(chains) if h2 == hh], axis=1) for hh in range(nh)]
    o_ref[0] = jnp.concatenate(per_head, axis=0).T


def _flash(q, k, vt, c, tq, width):
    b, t, d = q.shape
    tq = _row_tile(t, tq)
    return pl.pallas_call(
        functools.partial(_flash_kernel, tq=tq),
        grid=(b, d // width, t // tq),
        in_specs=[pl.BlockSpec((1, t, width), lambda bi, gi, i: (bi, 0, gi)),
                  pl.BlockSpec((1, t, width), lambda bi, gi, i: (bi, 0, gi)),
                  pl.BlockSpec((1, width, t), lambda bi, gi, i: (bi, gi, 0)),
                  pl.BlockSpec((1, t, LANES), lambda bi, gi, i: (bi, 0, 0))],
        out_specs=pl.BlockSpec((1, tq, width), lambda bi, gi, i: (bi, i, gi)),
        out_shape=jax.ShapeDtypeStruct((b, t, d), F32),
        scratch_shapes=[pltpu.VMEM((width // HEAD, t, width), BF16)] * 2,
        compiler_params=_params("parallel", "parallel", "arbitrary"),
        name="fox_flash",
    )(q, k, vt, c)


DEC_TILE = 512
DEC_WIDTH = 512


def _decode_attn_kernel(q_ref, ckt_ref, cvt_ref, kn_ref, vn_ref, clf_ref, lfn_ref, o_ref, c_scr):
    g = pl.program_id(1)
    past = ckt_ref.shape[2]
    t = q_ref.shape[1]

    @pl.when(g == 0)
    def _():
        tri = _tri(DEC_TILE, upper=True)
        carry = jnp.zeros((clf_ref.shape[1], 1), F32)
        for blk in range(past // DEC_TILE):
            cols = slice(blk * DEC_TILE, (blk + 1) * DEC_TILE)
            cs = _dot3(clf_ref[0, :, cols], tri) + carry
            c_scr[:, cols] = cs
            carry = cs[:, DEC_TILE - 1:DEC_TILE]

    width = q_ref.shape[2]
    nh = width // HEAD
    lane = lax.broadcasted_iota(jnp.int32, (1, width), 1)
    hlane = lax.broadcasted_iota(jnp.int32, (1, LANES), 1)
    q = q_ref[0]
    kct = ckt_ref[0].astype(BF16)
    vct = cvt_ref[0].astype(BF16)
    kn = kn_ref[0]
    vn = vn_ref[0]
    causal = lax.broadcasted_iota(jnp.int32, (t, t), 1) <= lax.broadcasted_iota(jnp.int32, (t, t), 0)
    lf_new = _dot3(lfn_ref[0], _tri(t, upper=False), dot=lambda p, tr: _dot(tr, p))

    H = range(nh)
    head = [nh * g + hh for hh in H]
    in_head = [(lane >> HEAD_SHIFT) == hh for hh in H]
    rows = lambda parts: jnp.concatenate(parts, axis=0)
    qs = rows([jnp.where(in_head[hh], q, jnp.zeros((), BF16)) for hh in H])
    c_cache = [c_scr[pl.ds(head[hh], 1), :] for hh in H]
    total = [c_cache[hh][:, past - 1:past] for hh in H]
    onehot = [jnp.where(hlane == head[hh], 1.0, 0.0) for hh in H]
    cn_col = [jnp.sum(lf_new * onehot[hh], axis=-1, keepdims=True) + total[hh] for hh in H]
    cn_row = [_dot3(lf_new, jnp.broadcast_to(onehot[hh], (SUBLANES, LANES)).astype(BF16),
                    dot=lambda p, e: _dot_nt(e, p))[0:1] + total[hh] for hh in H]
    s1 = _dot(qs, kct) + rows([cn_col[hh] - c_cache[hh] for hh in H]) * LOG2E
    s2 = _dot_nt(qs, kn) + rows([cn_col[hh] - cn_row[hh] for hh in H]) * LOG2E
    s2 = jnp.where(rows([causal] * nh), s2, -jnp.inf)
    m = jnp.maximum(jnp.max(s1, axis=-1, keepdims=True), jnp.max(s2, axis=-1, keepdims=True))
    e1 = jnp.exp2(s1 - m)
    e2 = jnp.exp2(s2 - m)
    l = jnp.sum(e1, axis=-1, keepdims=True) + jnp.sum(e2, axis=-1, keepdims=True)
    o = (_dot_nt(e1.astype(BF16), vct) + _dot(e2.astype(BF16), vn)) / l
    out = o[(nh - 1) * t:]
    for hh in range(nh - 2, -1, -1):
        out = jnp.where(in_head[hh], o[hh * t:(hh + 1) * t], out)
    o_ref[0] = out


def _decode_attn(q, cache_kt, cache_vt, kn, vn, cache_lf, lf_new):
    b, t, d = q.shape
    p = cache_kt.shape[2]
    h = cache_lf.shape[1]
    assert p % DEC_TILE == 0
    new = pl.BlockSpec((1, t, DEC_WIDTH), lambda bi, gi: (bi, 0, gi))
    old = pl.BlockSpec((1, DEC_WIDTH, p), lambda bi, gi: (bi, gi, 0))
    return pl.pallas_call(
        _decode_attn_kernel,
        grid=(b, d // DEC_WIDTH),
        in_specs=[new, old, old, new, new,
                  pl.BlockSpec((1, h, p), lambda bi, gi: (bi, 0, 0)),
                  pl.BlockSpec((1, t, LANES), lambda bi, gi: (bi, 0, 0))],
        out_specs=new,
        out_shape=jax.ShapeDtypeStruct((b, t, d), F32),
        scratch_shapes=[pltpu.VMEM((h, p), F32)],
        compiler_params=_params("parallel", "arbitrary"),
        name="fox_decode_attn",
    )(q, cache_kt, cache_vt, kn, vn, cache_lf, lf_new)


def _pad_cols(w, mult):
    pad = (-w.shape[-1]) % mult
    return jnp.pad(w, ((0, 0), (0, pad))) if pad else w


def _pad_rows(w, mult):
    pad = (-w.shape[0]) % mult
    return jnp.pad(w, ((0, pad), (0, 0))) if pad else w


def kernel(x_prompt, x_sample, state_wkv, state_shift, cache_k, cache_v, cache_logf, rwkv_mu, rwkv_w0, rwkv_w1, rwkv_w2, rwkv_a0, rwkv_a1, rwkv_a2, rwkv_g1, rwkv_g2, rwkv_k_k, rwkv_k_a, rwkv_r_k, rwkv_w_r, rwkv_w_k, rwkv_w_v, rwkv_w_o, rwkv_lnx_g, rwkv_lnx_b, fox_w_in, fox_b_f, fox_w_o, ffn_w1, ffn_w2, ln_mix_g, ln_mix_b, ln_ffn_g, ln_ffn_b):
    depth = ln_mix_g.shape[0]
    alpha = (2 * depth) ** 0.25
    bp, tp, d = x_prompt.shape
    bs, ts, _ = x_sample.shape
    heads = d // HEAD
    past = cache_k.shape[2]
    tm = 512

    xs_by_group = {"p": x_prompt, "s": x_sample}
    outs = {grp: {n: [] for n in ("wkv", "shift", "k", "v", "lf")} for grp in ("p", "s")}
    w1_all = ffn_w1.astype(BF16)
    w2_all = ffn_w2.astype(BF16)
    for i in range(depth):
        j = i // 2
        ln = jnp.stack([ln_mix_g[i], ln_mix_b[i], ln_ffn_g[i], ln_ffn_b[i]])
        if i % 2 == 0:
            row = lambda z: z.reshape(1, d)
            proj = (rwkv_mu[j], row(rwkv_w0[j]), row(rwkv_a0[j]),
                    rwkv_w_r[j].astype(BF16), rwkv_w_k[j].astype(BF16), rwkv_w_v[j].astype(BF16),
                    _pad_cols(rwkv_w1[j], LANES).astype(BF16), _pad_rows(rwkv_w2[j], LANES).astype(BF16),
                    _pad_cols(rwkv_a1[j], LANES).astype(BF16), _pad_rows(rwkv_a2[j], LANES).astype(BF16),
                    _pad_cols(rwkv_g1[j], LANES).astype(BF16), _pad_rows(rwkv_g2[j], LANES).astype(BF16))
            wkv_vecs = (row(rwkv_k_k[j]), row(rwkv_k_a[j]), row(rwkv_r_k[j]), row(rwkv_lnx_g[j]), row(rwkv_lnx_b[j]))
            wo = rwkv_w_o[j].astype(BF16)
            starts = {"p": (jnp.zeros((bp, d), x_prompt.dtype), jnp.zeros((bp, heads, HEAD, HEAD), state_wkv.dtype)),
                      "s": (state_shift[j], state_wkv[j])}
            for grp in ("p", "s"):
                x = xs_by_group[grp]
                b, t, _ = x.shape
                shift0, wkv0 = starts[grp]
                r, lw, k, v, a, gate = _rwkv_proj(x, shift0.reshape(b, 1, d), *proj, tm=tm)
                o, s_fin = _wkv(r, lw, k, v, a, *wkv_vecs, wkv0.astype(F32))
                outs[grp]["wkv"].append(s_fin.astype(wkv0.dtype))
                outs[grp]["shift"].append(x[:, -1])
                xs_by_group[grp] = _post(o.reshape(b * t, d), gate.reshape(b * t, d), x.reshape(b * t, d),
                                         wo, w1_all, w2_all, i, ln, alpha, tm).reshape(b, t, d)
        else:
            w_in = fox_w_in[j]
            wq = w_in[:, :d].astype(BF16)
            wk = w_in[:, d:2 * d].astype(BF16)
            wv = w_in[:, 2 * d:3 * d].astype(BF16)
            wf = _pad_cols(w_in[:, 3 * d:], LANES).astype(BF16)
            bf = _pad_cols(fox_b_f[j].reshape(1, heads), LANES)
            wo = fox_w_o[j].astype(BF16)

            xp = xs_by_group["p"]
            q, kt, vt, kb, vtb, lf, lft = _fox_proj(xp, wq, wk, wv.T, wf, bf, tm, channel_major=True)
            o = _flash(q, kb, vtb, _cumsum(lf), tq=512, width=SLAB)
            to_heads = lambda z: jnp.transpose(z.reshape(bp, heads, HEAD, tp), (0, 3, 1, 2))
            outs["p"]["k"].append(to_heads(kt))
            outs["p"]["v"].append(to_heads(vt))
            outs["p"]["lf"].append(jnp.transpose(lft[:, :heads, :], (0, 2, 1)))
            xs_by_group["p"] = _post(o.reshape(bp * tp, d), None, xp.reshape(bp * tp, d),
                                     wo, w1_all, w2_all, i, ln, alpha, tm).reshape(bp, tp, d)

            xs = xs_by_group["s"]
            q, k, v, kb, vb, lf = _fox_proj(xs, wq, wk, wv, wf, bf, tm, channel_major=False)
            channel_major = lambda z: jnp.transpose(z, (0, 2, 3, 1)).reshape(bs, d, past)
            o = _decode_attn(q, channel_major(cache_k[j]), channel_major(cache_v[j]), kb, vb,
                             jnp.transpose(cache_logf[j].astype(F32), (0, 2, 1)), lf)
            outs["s"]["k"].append(k.reshape(bs, ts, heads, HEAD))
            outs["s"]["v"].append(v.reshape(bs, ts, heads, HEAD))
            outs["s"]["lf"].append(lf[:, :, :heads])
            xs_by_group["s"] = _post(o.reshape(bs * ts, d), None, xs.reshape(bs * ts, d),
                                     wo, w1_all, w2_all, i, ln, alpha, tm).reshape(bs, ts, d)

    st = jnp.stack
    op, os_ = outs["p"], outs["s"]
    return (xs_by_group["p"], xs_by_group["s"],
            st(op["wkv"]), st(op["shift"]), st(op["k"]), st(op["v"]), st(op["lf"]),
            st(os_["wkv"]), st(os_["shift"]), st(os_["k"]), st(os_["v"]), st(os_["lf"]))
```

```python
import functools
import math

import jax
import jax.numpy as jnp
from jax import lax
from jax.experimental import pallas as pl
from jax.experimental.pallas import tpu as pltpu

BF16 = jnp.bfloat16
F32 = jnp.float32

HEAD = 64
HEAD_SHIFT = 6
LANES = 128
SUBLANES = 8
SLAB = 256
HEADS_PER_SLAB = SLAB // HEAD
CHUNK = 64
WKV_CELLS_PER_STEP = 4
LN_EPS = 1e-5
GN_EPS = 64e-5
EXP_M05 = math.exp(-0.5)
LOG2E = math.log2(math.e)
VMEM_LIMIT = 56 * 1024 * 1024


def _dot(a, b):
    return jnp.dot(a, b, preferred_element_type=F32)


def _dot_nt(a, b):
    return lax.dot_general(a, b, (((1,), (1,)), ((), ())), preferred_element_type=F32)


def _dot_tn(a, b):
    return lax.dot_general(a, b, (((0,), (0,)), ((), ())), preferred_element_type=F32)


def _split3(x):
    p1 = x.astype(BF16)
    r1 = x - p1.astype(F32)
    p2 = r1.astype(BF16)
    p3 = (r1 - p2.astype(F32)).astype(BF16)
    return p1, p2, p3


def _dot3(a, b, dot=_dot, pieces=3):
    return sum(dot(p, b) for p in _split3(a)[:pieces])


def _tri(n, upper):
    r = lax.broadcasted_iota(jnp.int32, (n, n), 0)
    c = lax.broadcasted_iota(jnp.int32, (n, n), 1)
    return jnp.where((r <= c) if upper else (c <= r), 1.0, 0.0).astype(BF16)


def _layer_norm(z, g, b):
    mu = jnp.mean(z, axis=-1, keepdims=True)
    zc = z - mu
    var = jnp.mean(zc * zc, axis=-1, keepdims=True)
    return zc * lax.rsqrt(var + LN_EPS) * g + b


def _params(*sem):
    return pltpu.CompilerParams(dimension_semantics=sem, vmem_limit_bytes=VMEM_LIMIT)


def _const_spec(shape):
    nd = len(shape)
    return pl.BlockSpec(shape, lambda *_: (0,) * nd, pipeline_mode=pl.Buffered(1))


def _row_tile(n, want):
    t = min(n, want)
    assert n % t == 0 and t % SUBLANES == 0, (n, t)
    return t


def _seq_tile(b, t, want):
    tm = _row_tile(t, want)
    nb = min(b, max(want // t, 1)) if tm == t else 1
    assert b % nb == 0, (b, nb)
    return nb, tm


def _rwkv_proj_kernel(x_ref, prev_ref, first_ref, mu_ref, w0_ref, a0_ref, wr_ref, wk_ref, wv_ref,
                      w1_ref, w2_ref, a1_ref, a2_ref, g1_ref, g2_ref,
                      r_ref, lw_ref, k_ref, v_ref, a_ref, g_ref):
    nb, rows, d = x_ref.shape
    x = x_ref[...].reshape(nb * rows, d)
    row_in_seq = lax.broadcasted_iota(jnp.int32, x.shape, 0) % rows
    if nb == 1:
        row0 = jnp.where(pl.program_id(1) == 0, first_ref[0], prev_ref[0, SUBLANES - 1:SUBLANES, :])
    else:
        row0 = jnp.broadcast_to(first_ref[...], (nb, rows, d)).reshape(nb * rows, d)
    xx = jnp.where(row_in_seq == 0, row0, pltpu.roll(x, 1, 0)) - x
    out = lambda z: z.reshape(nb, rows, d)

    def mix(i):
        return (x + xx * mu_ref[i:i + 1, :]).astype(BF16)

    r_ref[...] = out(_dot(mix(0), wr_ref[...]))
    w_raw = w0_ref[...] + _dot(jnp.tanh(_dot(mix(1), w1_ref[...])).astype(BF16), w2_ref[...])
    lw_ref[...] = out(-EXP_M05 * jax.nn.sigmoid(w_raw))
    k_ref[...] = out(_dot(mix(2), wk_ref[...]))
    v_ref[...] = out(_dot(mix(3), wv_ref[...]))
    a_ref[...] = out(jax.nn.sigmoid(a0_ref[...] + _dot(_dot(mix(4), a1_ref[...]).astype(BF16), a2_ref[...])))
    g_ref[...] = out(_dot(jax.nn.sigmoid(_dot(mix(5), g1_ref[...])).astype(BF16), g2_ref[...]))


def _rwkv_proj(x, shift0, mu, w0, a0, wr, wk, wv, w1, w2, a1, a2, g1, g2, tm):
    b, t, d = x.shape
    nb, tm = _seq_tile(b, t, tm)
    per_tile = tm // SUBLANES
    row = pl.BlockSpec((nb, tm, d), lambda bi, i: (bi, i, 0))
    prev = pl.BlockSpec((1, SUBLANES, d), lambda bi, i: (bi * nb, jnp.maximum(i * per_tile - 1, 0), 0))
    first = pl.BlockSpec((nb, 1, d), lambda bi, i: (bi, 0, 0))
    consts = [mu, w0, a0, wr, wk, wv, w1, w2, a1, a2, g1, g2]
    return pl.pallas_call(
        _rwkv_proj_kernel,
        grid=(b // nb, t // tm),
        in_specs=[row, prev, first] + [_const_spec(c.shape) for c in consts],
        out_specs=[row] * 6,
        out_shape=[jax.ShapeDtypeStruct((b, t, d), F32)] * 6,
        compiler_params=_params("parallel", "parallel"),
        name="rwkv_proj",
    )(x, x, shift0, *consts)


def _wkv_kernel(r_ref, lw_ref, k_ref, v_ref, a_ref, kk_ref, ka_ref, rk_ref, lg_ref, lb_ref, h0_ref,
                o_ref, hout_ref, h_scr, *, last_chunk):
    c = pl.program_id(1)
    nslab = h_scr.shape[0]
    C = CHUNK
    nseq, rows = r_ref.shape[:2]
    nslab = nslab // nseq

    @pl.when(c == 0)
    def _():
        h_scr[...] = jnp.zeros_like(h_scr)
        for n in range(nseq * nslab):
            for h in range(HEADS_PER_SLAB):
                hs = slice(h * HEAD, (h + 1) * HEAD)
                h_scr[n, hs, hs] = h0_ref[n // nslab, (n % nslab) * HEADS_PER_SLAB + h]

    row_s = lax.broadcasted_iota(jnp.int32, (SLAB, SLAB), 0)
    col_s = lax.broadcasted_iota(jnp.int32, (SLAB, SLAB), 1)
    mask_bd = (row_s >> HEAD_SHIFT) == (col_s >> HEAD_SHIFT)
    row_c = lax.broadcasted_iota(jnp.int32, (C, SLAB), 0)
    colin_c = lax.broadcasted_iota(jnp.int32, (C, SLAB), 1) & (C - 1)
    strict_lo = colin_c < row_c
    incl_lo = colin_c <= row_c
    eye_c = colin_c == row_c
    ones_bd = jnp.where(mask_bd, 1.0, 0.0).astype(BF16)

    def bd(x):
        xb = x.astype(BF16)
        return jnp.where(mask_bd, jnp.concatenate([xb] * HEADS_PER_SLAB, axis=0), jnp.zeros((), BF16))

    def head_sum(x):
        return _dot(x.astype(BF16), ones_bd)

    nck = max(rows // C, 1)
    live = min(rows, C)

    ncell = nseq * nck

    def chunk_rows(ref, ci, sl):
        x = ref[ci // nck, (ci % nck) * C:(ci % nck) * C + live, sl]
        if live < C:
            x = jnp.concatenate([x, jnp.zeros((C - live, x.shape[1]), x.dtype)], axis=0)
        return x

    lw_c = [chunk_rows(lw_ref, ci, slice(None)) for ci in range(ncell)]
    L_c = [_dot3(lw_c[ci], _tri(C, upper=False), dot=lambda p, t: _dot(t, p), pieces=2) for ci in range(ncell)]

    units = [(ci, s) for ci in range(ncell) for s in range(nslab)]
    S = range(len(units))
    slabs = [slice(s * SLAB, (s + 1) * SLAB) for _, s in units]
    r = [chunk_rows(r_ref, ci, slabs[u]) for u, (ci, _) in enumerate(units)]
    v = [chunk_rows(v_ref, ci, slabs[u]) for u, (ci, _) in enumerate(units)]
    asig = [chunk_rows(a_ref, ci, slabs[u]) for u, (ci, _) in enumerate(units)]
    k0 = [chunk_rows(k_ref, ci, slabs[u]) for u, (ci, _) in enumerate(units)]
    lw = [lw_c[ci][:, slabs[u]] for u, (ci, _) in enumerate(units)]
    kk = [k0[s] * kk_ref[:, slabs[s]] for s in S]
    k = [k0[s] * (1.0 + (asig[s] - 1.0) * ka_ref[:, slabs[s]]) for s in S]
    sums = [head_sum(jnp.concatenate([kk[s] * kk[s], r[s] * k[s] * rk_ref[:, slabs[s]]], axis=0)) for s in S]
    kk = [kk[s] / jnp.maximum(jnp.sqrt(sums[s][:C]), 1e-12) for s in S]
    bonus = [sums[s][C:] * v[s] for s in S]
    bv = [kk[s] * asig[s] for s in S]
    L = [L_c[ci][:, slabs[u]] for u, (ci, _) in enumerate(units)]
    LC = [L[s][C - 1:C, :] for s in S]
    enL = [jnp.exp(-L[s]) for s in S]
    eLc = [jnp.exp(LC[s] - L[s]) for s in S]
    At = [-kk[s] * jnp.exp(L[s] - lw[s]) for s in S]
    Rt = [r[s] * jnp.exp(L[s]) for s in S]
    Bt = [bv[s] * enL[s] for s in S]
    Kt = [k[s] * enL[s] for s in S]
    Bh = [bv[s] * eLc[s] for s in S]
    Kh = [k[s] * eLc[s] for s in S]

    lhs = [jnp.concatenate([At[s], Rt[s]], axis=0).astype(BF16) for s in S]
    ab = [_dot_nt(lhs[s], bd(Bt[s])) for s in S]
    ak = [_dot_nt(lhs[s], bd(Kt[s])) for s in S]
    N = [jnp.where(strict_lo, ab[s][:C], 0.0) for s in S]
    M = [jnp.where(strict_lo, ak[s][:C], 0.0) for s in S]
    Arb = [jnp.where(incl_lo, ab[s][C:], 0.0) for s in S]
    Ark = [jnp.where(incl_lo, ak[s][C:], 0.0) for s in S]
    mv = [_dot(jnp.concatenate([M[s], Ark[s]], axis=0).astype(BF16), bd(v[s])) for s in S]
    MV = [mv[s][:C] for s in S]

    X = [jnp.where(eye_c, 1.0, 0.0) + N[s] for s in S]
    A = [_dot(N[s].astype(BF16), bd(N[s])) for s in S]
    for _ in range(int(math.log2(C)) - 2):
        both = [_dot(jnp.concatenate([A[s], X[s]], axis=0).astype(BF16), bd(A[s])) for s in S]
        X = [X[s] + both[s][C:] for s in S]
        A = [both[s][:C] for s in S]
    X = [X[s] + _dot(X[s].astype(BF16), bd(A[s])) for s in S]

    wu = [_dot(X[s].astype(BF16), jnp.concatenate([bd(At[s]), bd(MV[s])], axis=1)) for s in S]
    WA = [wu[s][:, :SLAB] for s in S]
    UV = [wu[s][:, SLAB:] for s in S]
    au = [_dot(Arb[s].astype(BF16), jnp.concatenate([bd(WA[s]), bd(UV[s])], axis=1)) for s in S]
    Rp = [Rt[s] + au[s][:, :SLAB] for s in S]
    Yl = [au[s][:, SLAB:] + mv[s][C:] for s in S]
    rw = [jnp.concatenate([Rp[s], WA[s]], axis=0).astype(BF16) for s in S]
    bk = [jnp.concatenate([Bh[s], Kh[s]], axis=0).astype(BF16) for s in S]
    pc = [jnp.exp(LC[s]) for s in S]

    h = [h_scr[n] for n in range(nseq * nslab)]
    y = [None] * len(units)
    for cj in range(nck):
        us = [((n // nslab) * nck + cj) * nslab + n % nslab for n in range(nseq * nslab)]
        yu = [_dot_nt(rw[u], h[n].astype(BF16)) for n, u in enumerate(us)]
        for n, u in enumerate(us):
            y[u] = yu[n][:C] + Yl[u]
        uv = [jnp.concatenate([yu[n][C:] + UV[u], v[u]], axis=0).astype(BF16) for n, u in enumerate(us)]
        h = [h[n] * pc[u] + jnp.where(mask_bd, _dot_tn(uv[n], bk[u]), 0.0) for n, u in enumerate(us)]
    for n in range(nseq * nslab):
        h_scr[n] = h[n]

    mu = [head_sum(y[s]) * (1.0 / HEAD) for s in S]
    yc = [y[s] - mu[s] for s in S]
    var = [head_sum(yc[s] * yc[s]) * (1.0 / HEAD) for s in S]
    for u, (ci, _) in enumerate(units):
        sl = slabs[u]
        out = yc[u] * lax.rsqrt(var[u] + GN_EPS) * lg_ref[:, sl] + lb_ref[:, sl] + bonus[u]
        o_ref[ci // nck, (ci % nck) * C:(ci % nck) * C + live, sl] = out[:live]

    @pl.when(c == last_chunk)
    def _():
        for n in range(nseq * nslab):
            for h in range(HEADS_PER_SLAB):
                hs = slice(h * HEAD, (h + 1) * HEAD)
                hout_ref[n // nslab, (n % nslab) * HEADS_PER_SLAB + h] = h_scr[n, hs, hs]


def _wkv(r, lw, k, v, a, k_k, k_a, r_k, lnx_g, lnx_b, h0):
    b, t, d = r.shape
    if t % (WKV_CELLS_PER_STEP * CHUNK) == 0:
        nseq, rows = 1, WKV_CELLS_PER_STEP * CHUNK
    else:
        nseq, rows = (WKV_CELLS_PER_STEP if t <= CHUNK and b % WKV_CELLS_PER_STEP == 0 else 1), min(t, CHUNK)
    assert t % rows == 0 and rows % SUBLANES == 0
    nslab = d // SLAB
    heads = d // HEAD
    seq = pl.BlockSpec((nseq, rows, d), lambda i, c: (i, c, 0))
    vec = pl.BlockSpec((1, d), lambda i, c: (0, 0))
    hspec = pl.BlockSpec((nseq, heads, HEAD, HEAD), lambda i, c: (i, 0, 0, 0))
    return pl.pallas_call(
        functools.partial(_wkv_kernel, last_chunk=t // rows - 1),
        grid=(b // nseq, t // rows),
        in_specs=[seq] * 5 + [vec] * 5 + [hspec],
        out_specs=[seq, hspec],
        out_shape=[jax.ShapeDtypeStruct((b, t, d), F32),
                   jax.ShapeDtypeStruct((b, heads, HEAD, HEAD), F32)],
        scratch_shapes=[pltpu.VMEM((nseq * nslab, SLAB, SLAB), F32)],
        compiler_params=_params("parallel", "arbitrary"),
        name="wkv_chunked",
    )(r, lw, k, v, a, k_k, k_a, r_k, lnx_g, lnx_b, h0)


def _post_kernel(*refs, alpha, gated, ff_chunk):
    if gated:
        pre_ref, gate_ref, x_ref, wo_ref, w1_ref, w2_ref, ln_ref, out_ref = refs
        pre = pre_ref[...] * gate_ref[...]
    else:
        pre_ref, x_ref, wo_ref, w1_ref, w2_ref, ln_ref, out_ref = refs
        pre = pre_ref[...]
    h = _dot(pre.astype(BF16), wo_ref[...])
    x1 = _layer_norm(alpha * x_ref[...] + h, ln_ref[0:1, :], ln_ref[1:2, :])
    x1b = x1.astype(BF16)
    m = jnp.zeros_like(x1)
    for f in range(w1_ref.shape[1] // ff_chunk):
        fs = slice(f * ff_chunk, (f + 1) * ff_chunk)
        hid = jnp.maximum(_dot(x1b, w1_ref[:, fs]), 0.0)
        m = m + _dot((hid * hid).astype(BF16), w2_ref[fs, :])
    out_ref[...] = _layer_norm(alpha * x1 + m, ln_ref[2:3, :], ln_ref[3:4, :])


def _post(pre, gate, x, wo, w1_all, w2_all, layer, ln, alpha, tm):
    n, d = x.shape
    tm = _row_tile(n, tm)
    row = pl.BlockSpec((tm, d), lambda i: (i, 0))
    acts = [pre, x] if gate is None else [pre, gate, x]
    layer_spec = lambda w: pl.BlockSpec((None,) + w.shape[1:], lambda i: (layer, 0, 0), pipeline_mode=pl.Buffered(1))
    return pl.pallas_call(
        functools.partial(_post_kernel, alpha=alpha, gated=gate is not None, ff_chunk=min(1024, w1_all.shape[2])),
        grid=(n // tm,),
        in_specs=[row] * len(acts) + [_const_spec(wo.shape), layer_spec(w1_all), layer_spec(w2_all),
                                      _const_spec(ln.shape)],
        out_specs=row,
        out_shape=jax.ShapeDtypeStruct((n, d), F32),
        compiler_params=_params("parallel"),
        name="post_mlp",
    )(*acts, wo, w1_all, w2_all, ln)


def _fox_proj_kernel(x_ref, wq_ref, wk_ref, wv_ref, wf_ref, bf_ref,
                     q_ref, k_ref, v_ref, kb_ref, vb_ref, lf_ref, *lft_ref, channel_major):
    nb, rows, d = x_ref.shape
    xb = x_ref[...].reshape(nb * rows, d).astype(BF16)
    out = lambda z: z.reshape(nb, rows, z.shape[-1])
    q_ref[...] = out((_dot(xb, wq_ref[...]) * (HEAD ** -0.5 * LOG2E)).astype(BF16))
    k = _dot(xb, wk_ref[...])
    kb_ref[...] = out(k.astype(BF16))
    if channel_major:
        k_ref[0] = k.T
        v = _dot_nt(wv_ref[...], xb)
        v_ref[0] = v
        vb_ref[0] = v.astype(BF16)
    else:
        v = _dot(xb, wv_ref[...])
        k_ref[...] = out(k)
        v_ref[...] = out(v)
        vb_ref[...] = out(v.astype(BF16))
    f = _dot(xb, wf_ref[...]) + bf_ref[...]
    lf = jnp.minimum(f, 0.0) - jnp.log1p(jnp.exp(-jnp.abs(f)))
    lf_ref[...] = out(lf)
    if channel_major:
        lft_ref[0][0] = lf.T


def _fox_proj(x, wq, wk, wv, wf, bf, tm, channel_major):
    b, t, d = x.shape
    nb, tm = _seq_tile(b, t, tm)
    assert nb == 1 or not channel_major
    row = pl.BlockSpec((nb, tm, d), lambda bi, i: (bi, i, 0))
    rowf = pl.BlockSpec((nb, tm, LANES), lambda bi, i: (bi, i, 0))
    kv_spec = pl.BlockSpec((1, d, tm), lambda bi, i: (bi, 0, i)) if channel_major else row
    kv_shape = jax.ShapeDtypeStruct((b, d, t) if channel_major else (b, t, d), F32)
    act_bf = jax.ShapeDtypeStruct((b, t, d), BF16)
    vb_shape = jax.ShapeDtypeStruct(kv_shape.shape, BF16)
    consts = [wq, wk, wv, wf, bf]
    return pl.pallas_call(
        functools.partial(_fox_proj_kernel, channel_major=channel_major),
        grid=(b // nb, t // tm),
        in_specs=[row] + [_const_spec(c.shape) for c in consts],
        out_specs=[row, kv_spec, kv_spec, row, kv_spec, rowf]
        + [pl.BlockSpec((1, LANES, tm), lambda bi, i: (bi, 0, i))] * channel_major,
        out_shape=[act_bf, kv_shape, kv_shape, act_bf, vb_shape, jax.ShapeDtypeStruct((b, t, LANES), F32)]
        + [jax.ShapeDtypeStruct((b, LANES, t), F32)] * channel_major,
        compiler_params=_params("parallel", "parallel"),
        name="fox_proj",
    )(x, *consts)


CS_TILE = 512
SUM_ROWS = 16


def _cumsum_kernel(lf_ref, c_ref, carry_ref):
    @pl.when(pl.program_id(1) == 0)
    def _():
        carry_ref[...] = jnp.zeros_like(carry_ref)

    n = lf_ref.shape[1]
    cs = _dot3(lf_ref[0], _tri(n, upper=False), dot=lambda p, t: _dot(t, p)) + carry_ref[0:1, :]
    c_ref[0] = cs
    carry_ref[...] = jnp.broadcast_to(cs[n - 1:n, :], carry_ref.shape)


def _cumsum(lf):
    b, l, w = lf.shape
    tile = _row_tile(l, CS_TILE)
    spec = pl.BlockSpec((1, tile, w), lambda i, j: (i, j, 0))
    return pl.pallas_call(
        _cumsum_kernel,
        grid=(b, l // tile),
        in_specs=[spec],
        out_specs=spec,
        out_shape=jax.ShapeDtypeStruct((b, l, w), F32),
        scratch_shapes=[pltpu.VMEM((SUBLANES, w), F32)],
        compiler_params=_params("parallel", "arbitrary"),
        name="logf_cumsum",
    )(lf)


def _bias_lanes(c, head0, width, key_side):
    nh = width // HEAD
    src = lax.broadcasted_iota(jnp.int32, (LANES, width), 0)
    dst = lax.broadcasted_iota(jnp.int32, (LANES, width), 1)
    lane = lax.broadcasted_iota(jnp.int32, (1, width), 1)
    lo, sign = (3, -1.0) if key_side else (0, 1.0)
    base = [((hh + 1) % nh) * HEAD for hh in range(nh)]
    ones = functools.reduce(jnp.logical_or, [(lane >= bs + 3 - lo) & (lane < bs + 6 - lo) for bs in base])
    out = jnp.where(ones, 1.0, 0.0)
    for n, p in enumerate(_split3(c * LOG2E)):
        hit = functools.reduce(jnp.logical_or,
                               [(src == head0 + hh) & (dst == base[hh] + lo + n) for hh in range(nh)])
        out = out + _dot(p, jnp.where(hit, sign, 0.0).astype(BF16))
    return out


def _flash_kernel(q_ref, k_ref, vt_ref, c_ref, o_ref, qaug_scr, kaug_scr, *, tq):
    g = pl.program_id(1)
    i = pl.program_id(2)
    t = k_ref.shape[1]
    width = k_ref.shape[2]
    nh = width // HEAD
    lane = lax.broadcasted_iota(jnp.int32, (1, width), 1)
    in_head = [(lane >> HEAD_SHIFT) == hh for hh in range(nh)]
    bias_lanes = [(lane >= ((hh + 1) % nh) * HEAD) & (lane < ((hh + 1) % nh) * HEAD + 6) for hh in range(nh)]

    def augment(x, bias):
        bias = bias.astype(BF16)
        return [jnp.where(in_head[hh], x, jnp.where(bias_lanes[hh], bias, jnp.zeros((), BF16)))
                for hh in range(nh)]

    @pl.when(i == 0)
    def _():
        for blk in range(t // tq):
            rows = slice(blk * tq, (blk + 1) * tq)
            c = c_ref[0, rows, :]
            qa = augment(q_ref[0, rows, :], _bias_lanes(c, nh * g, width, key_side=False))
            ka = augment(k_ref[0, rows, :], _bias_lanes(c, nh * g, width, key_side=True))
            for hh in range(nh):
                qaug_scr[hh, rows, :] = qa[hh]
                kaug_scr[hh, rows, :] = ka[hh]

    row0 = pl.multiple_of(i * tq, tq)

    qg = min(tq, SLAB)
    chains = [(hh, qs) for hh in range(nh) for qs in range(0, tq, qg)]
    nc = len(chains)
    qa_c = [qaug_scr[hh, pl.ds(row0 + qs, qg), :] for hh, qs in chains]

    ones_rows = jnp.ones((SUM_ROWS, tq), BF16)

    def step(off, carry, mask):
        ks = [kaug_scr[hh, pl.ds(off, tq), :] for hh in range(nh)]
        vts = [jnp.concatenate([vt_ref[0, hh * HEAD:(hh + 1) * HEAD, pl.ds(off, tq)], ones_rows], axis=0)
               for hh in range(nh)]
        kext = [tq if mask is None else qs + qg for _, qs in chains]
        st = [_dot_nt(ks[hh][:kext[n]], qa_c[n]) for n, (hh, _) in enumerate(chains)]
        if mask is not None:
            st = [jnp.where(mask[:kext[n], qs:qs + qg], st[n], -jnp.inf) for n, (_, qs) in enumerate(chains)]
        m_new = [jnp.maximum(carry[n][0], jnp.max(st[n], axis=0, keepdims=True)) for n in range(nc)]
        alpha = [jnp.exp2(carry[n][0] - m_new[n]) for n in range(nc)]
        p = [jnp.exp2(st[n] - m_new[n]) for n in range(nc)]
        acc = [alpha[n] * carry[n][1] + _dot(vts[hh][:, :kext[n]], p[n].astype(BF16))
               for n, (hh, _) in enumerate(chains)]
        return tuple(zip(m_new, acc))

    init = tuple((jnp.full((1, qg), -jnp.inf, F32), jnp.zeros((HEAD + SUM_ROWS, qg), F32)) for _ in chains)
    carry = lax.fori_loop(0, i, lambda j, c: step(pl.multiple_of(j * tq, tq), c, None), init)
    causal = lax.broadcasted_iota(jnp.int32, (tq, tq), 0) <= lax.broadcasted_iota(jnp.int32, (tq, tq), 1)
    carry = step(row0, carry, causal)
    per_head = [jnp.concatenate([carry[n][1][:HEAD] / carry[n][1][HEAD:HEAD + 1]
                                 for n, (h2, _) in enumerate(chains) if h2 == hh], axis=1) for hh in range(nh)]
    o_ref[0] = jnp.concatenate(per_head, axis=0).T.astype(o_ref.dtype)


def _flash(q, k, vt, c, tq, width):
    b, t, d = q.shape
    tq = _row_tile(t, tq)
    return pl.pallas_call(
        functools.partial(_flash_kernel, tq=tq),
        grid=(b, d // width, t // tq),
        in_specs=[pl.BlockSpec((1, t, width), lambda bi, gi, i: (bi, 0, gi)),
                  pl.BlockSpec((1, t, width), lambda bi, gi, i: (bi, 0, gi)),
                  pl.BlockSpec((1, width, t), lambda bi, gi, i: (bi, gi, 0)),
                  pl.BlockSpec((1, t, LANES), lambda bi, gi, i: (bi, 0, 0))],
        out_specs=pl.BlockSpec((1, tq, width), lambda bi, gi, i: (bi, i, gi)),
        out_shape=jax.ShapeDtypeStruct((b, t, d), BF16),
        scratch_shapes=[pltpu.VMEM((width // HEAD, t, width), BF16)] * 2,
        compiler_params=_params("parallel", "parallel", "arbitrary"),
        name="fox_flash",
    )(q, k, vt, c)


DEC_TILE = 512
DEC_WIDTH = 512


def _decode_attn_kernel(q_ref, ckt_ref, cvt_ref, kn_ref, vn_ref, clf_ref, lfn_ref, o_ref, c_scr):
    g = pl.program_id(1)
    past = ckt_ref.shape[2]
    t = q_ref.shape[1]

    @pl.when(g == 0)
    def _():
        tri = _tri(DEC_TILE, upper=True)
        carry = jnp.zeros((clf_ref.shape[1], 1), F32)
        for blk in range(past // DEC_TILE):
            cols = slice(blk * DEC_TILE, (blk + 1) * DEC_TILE)
            cs = _dot3(clf_ref[0, :, cols], tri) + carry
            c_scr[:, cols] = cs
            carry = cs[:, DEC_TILE - 1:DEC_TILE]

    width = q_ref.shape[2]
    nh = width // HEAD
    lane = lax.broadcasted_iota(jnp.int32, (1, width), 1)
    hlane = lax.broadcasted_iota(jnp.int32, (1, LANES), 1)
    q = q_ref[0]
    kct = ckt_ref[0].astype(BF16)
    vct = cvt_ref[0].astype(BF16)
    kn = kn_ref[0]
    vn = vn_ref[0]
    causal = lax.broadcasted_iota(jnp.int32, (t, t), 1) <= lax.broadcasted_iota(jnp.int32, (t, t), 0)
    lf_new = _dot3(lfn_ref[0], _tri(t, upper=False), dot=lambda p, tr: _dot(tr, p))

    H = range(nh)
    head = [nh * g + hh for hh in H]
    in_head = [(lane >> HEAD_SHIFT) == hh for hh in H]
    rows = lambda parts: jnp.concatenate(parts, axis=0)
    qs = rows([jnp.where(in_head[hh], q, jnp.zeros((), BF16)) for hh in H])
    c_cache = [c_scr[pl.ds(head[hh], 1), :] for hh in H]
    total = [c_cache[hh][:, past - 1:past] for hh in H]
    onehot = [jnp.where(hlane == head[hh], 1.0, 0.0) for hh in H]
    cn_col = [jnp.sum(lf_new * onehot[hh], axis=-1, keepdims=True) + total[hh] for hh in H]
    cn_row = [_dot3(lf_new, jnp.broadcast_to(onehot[hh], (SUBLANES, LANES)).astype(BF16),
                    dot=lambda p, e: _dot_nt(e, p))[0:1] + total[hh] for hh in H]
    s1 = _dot(qs, kct) + rows([cn_col[hh] - c_cache[hh] for hh in H]) * LOG2E
    s2 = _dot_nt(qs, kn) + rows([cn_col[hh] - cn_row[hh] for hh in H]) * LOG2E
    s2 = jnp.where(rows([causal] * nh), s2, -jnp.inf)
    m = jnp.maximum(jnp.max(s1, axis=-1, keepdims=True), jnp.max(s2, axis=-1, keepdims=True))
    e1 = jnp.exp2(s1 - m)
    e2 = jnp.exp2(s2 - m)
    l = jnp.sum(e1, axis=-1, keepdims=True) + jnp.sum(e2, axis=-1, keepdims=True)
    o = (_dot_nt(e1.astype(BF16), vct) + _dot(e2.astype(BF16), vn)) / l
    out = o[(nh - 1) * t:]
    for hh in range(nh - 2, -1, -1):
        out = jnp.where(in_head[hh], o[hh * t:(hh + 1) * t], out)
    o_ref[0] = out.astype(o_ref.dtype)


def _decode_attn(q, cache_kt, cache_vt, kn, vn, cache_lf, lf_new):
    b, t, d = q.shape
    p = cache_kt.shape[2]
    h = cache_lf.shape[1]
    assert p % DEC_TILE == 0
    new = pl.BlockSpec((1, t, DEC_WIDTH), lambda bi, gi: (bi, 0, gi))
    old = pl.BlockSpec((1, DEC_WIDTH, p), lambda bi, gi: (bi, gi, 0))
    return pl.pallas_call(
        _decode_attn_kernel,
        grid=(b, d // DEC_WIDTH),
        in_specs=[new, old, old, new, new,
                  pl.BlockSpec((1, h, p), lambda bi, gi: (bi, 0, 0)),
                  pl.BlockSpec((1, t, LANES), lambda bi, gi: (bi, 0, 0))],
        out_specs=new,
        out_shape=jax.ShapeDtypeStruct((b, t, d), BF16),
        scratch_shapes=[pltpu.VMEM((h, p), F32)],
        compiler_params=_params("parallel", "arbitrary"),
        name="fox_decode_attn",
    )(q, cache_kt, cache_vt, kn, vn, cache_lf, lf_new)


def _pad_cols(w, mult):
    pad = (-w.shape[-1]) % mult
    return jnp.pad(w, ((0, 0), (0, pad))) if pad else w


def _pad_rows(w, mult):
    pad = (-w.shape[0]) % mult
    return jnp.pad(w, ((0, pad), (0, 0))) if pad else w


def kernel(x_prompt, x_sample, state_wkv, state_shift, cache_k, cache_v, cache_logf, rwkv_mu, rwkv_w0, rwkv_w1, rwkv_w2, rwkv_a0, rwkv_a1, rwkv_a2, rwkv_g1, rwkv_g2, rwkv_k_k, rwkv_k_a, rwkv_r_k, rwkv_w_r, rwkv_w_k, rwkv_w_v, rwkv_w_o, rwkv_lnx_g, rwkv_lnx_b, fox_w_in, fox_b_f, fox_w_o, ffn_w1, ffn_w2, ln_mix_g, ln_mix_b, ln_ffn_g, ln_ffn_b):
    depth = ln_mix_g.shape[0]
    alpha = (2 * depth) ** 0.25
    bp, tp, d = x_prompt.shape
    bs, ts, _ = x_sample.shape
    heads = d // HEAD
    past = cache_k.shape[2]
    tm = 512

    xs_by_group = {"p": x_prompt, "s": x_sample}
    outs = {grp: {n: [] for n in ("wkv", "shift", "k", "v", "lf")} for grp in ("p", "s")}
    w1_all = ffn_w1.astype(BF16)
    w2_all = ffn_w2.astype(BF16)
    for i in range(depth):
        j = i // 2
        ln = jnp.stack([ln_mix_g[i], ln_mix_b[i], ln_ffn_g[i], ln_ffn_b[i]])
        if i % 2 == 0:
            row = lambda z: z.reshape(1, d)
            proj = (rwkv_mu[j], row(rwkv_w0[j]), row(rwkv_a0[j]),
                    rwkv_w_r[j].astype(BF16), rwkv_w_k[j].astype(BF16), rwkv_w_v[j].astype(BF16),
                    _pad_cols(rwkv_w1[j], LANES).astype(BF16), _pad_rows(rwkv_w2[j], LANES).astype(BF16),
                    _pad_cols(rwkv_a1[j], LANES).astype(BF16), _pad_rows(rwkv_a2[j], LANES).astype(BF16),
                    _pad_cols(rwkv_g1[j], LANES).astype(BF16), _pad_rows(rwkv_g2[j], LANES).astype(BF16))
            wkv_vecs = (row(rwkv_k_k[j]), row(rwkv_k_a[j]), row(rwkv_r_k[j]), row(rwkv_lnx_g[j]), row(rwkv_lnx_b[j]))
            wo = rwkv_w_o[j].astype(BF16)
            starts = {"p": (jnp.zeros((bp, d), x_prompt.dtype), jnp.zeros((bp, heads, HEAD, HEAD), state_wkv.dtype)),
                      "s": (state_shift[j], state_wkv[j])}
            for grp in ("p", "s"):
                x = xs_by_group[grp]
                b, t, _ = x.shape
                shift0, wkv0 = starts[grp]
                r, lw, k, v, a, gate = _rwkv_proj(x, shift0.reshape(b, 1, d), *proj, tm=tm)
                o, s_fin = _wkv(r, lw, k, v, a, *wkv_vecs, wkv0.astype(F32))
                outs[grp]["wkv"].append(s_fin.astype(wkv0.dtype))
                outs[grp]["shift"].append(x[:, -1])
                xs_by_group[grp] = _post(o.reshape(b * t, d), gate.reshape(b * t, d), x.reshape(b * t, d),
                                         wo, w1_all, w2_all, i, ln, alpha, tm).reshape(b, t, d)
        else:
            w_in = fox_w_in[j]
            wq = w_in[:, :d].astype(BF16)
            wk = w_in[:, d:2 * d].astype(BF16)
            wv = w_in[:, 2 * d:3 * d].astype(BF16)
            wf = _pad_cols(w_in[:, 3 * d:], LANES).astype(BF16)
            bf = _pad_cols(fox_b_f[j].reshape(1, heads), LANES)
            wo = fox_w_o[j].astype(BF16)

            xp = xs_by_group["p"]
            q, kt, vt, kb, vtb, lf, lft = _fox_proj(xp, wq, wk, wv.T, wf, bf, tm, channel_major=True)
            o = _flash(q, kb, vtb, _cumsum(lf), tq=512, width=SLAB)
            to_heads = lambda z: jnp.transpose(z.reshape(bp, heads, HEAD, tp), (0, 3, 1, 2))
            outs["p"]["k"].append(to_heads(kt))
            outs["p"]["v"].append(to_heads(vt))
            outs["p"]["lf"].append(jnp.transpose(lft[:, :heads, :], (0, 2, 1)))
            xs_by_group["p"] = _post(o.reshape(bp * tp, d), None, xp.reshape(bp * tp, d),
                                     wo, w1_all, w2_all, i, ln, alpha, tm).reshape(bp, tp, d)

            xs = xs_by_group["s"]
            q, k, v, kb, vb, lf = _fox_proj(xs, wq, wk, wv, wf, bf, tm, channel_major=False)
            channel_major = lambda z: jnp.transpose(z, (0, 2, 3, 1)).reshape(bs, d, past)
            o = _decode_attn(q, channel_major(cache_k[j]), channel_major(cache_v[j]), kb, vb,
                             jnp.transpose(cache_logf[j].astype(F32), (0, 2, 1)), lf)
            outs["s"]["k"].append(k.reshape(bs, ts, heads, HEAD))
            outs["s"]["v"].append(v.reshape(bs, ts, heads, HEAD))
            outs["s"]["lf"].append(lf[:, :, :heads])
            xs_by_group["s"] = _post(o.reshape(bs * ts, d), None, xs.reshape(bs * ts, d),
                                     wo, w1_all, w2_all, i, ln, alpha, tm).reshape(bs, ts, d)

    st = jnp.stack
    op, os_ = outs["p"], outs["s"]
    return (xs_by_group["p"], xs_by_group["s"],
            st(op["wkv"]), st(op["shift"]), st(op["k"]), st(op["v"]), st(op["lf"]),
            st(os_["wkv"]), st(os_["shift"]), st(os_["k"]), st(os_["v"]), st(os_["lf"]))
```
